```python
import math
import jax, jax.numpy as jnp
from jax import lax
import numpy as np

D_MODEL = 2048
BATCH = 2
SEQ = 4096
DEPTH = 1

D_MIX = D_MODEL
D_SSM = D_MIX // 2
SSM_GROUP = 16
SSM_GROUPS = D_SSM // SSM_GROUP
SSM_STATE = 64
D_SGU = D_MIX - D_SSM
SGU_CHUNK = 128
SGU_HEADS = 8
SGU_HEAD_DIM = D_SGU // SGU_HEADS
D_FFN = -(-8 * D_MODEL // (3 * 256)) * 256
PLE_DIM = 256
EPS = 1e-6
DT_MIN = 1e-3
DT_MAX = 1e-1
LAMBDA_RE_MAX = -1e-4

kernel_name = "hybrid_s5_sgu_parallel_heads"


def rmsnorm(x, g):
    xf = x.astype(jnp.float32)
    r = lax.rsqrt(jnp.mean(xf * xf, axis=-1, keepdims=True) + EPS)
    return (xf * r).astype(x.dtype) * g


def layernorm(x, g, b):
    xf = x.astype(jnp.float32)
    mu = jnp.mean(xf, axis=-1, keepdims=True)
    xc = xf - mu
    r = lax.rsqrt(jnp.mean(xc * xc, axis=-1, keepdims=True) + EPS)
    return (xc * r).astype(x.dtype) * g + b


def _complex_linear_combine(e1, e2):
    a1r, a1i, b1r, b1i = e1
    a2r, a2i, b2r, b2i = e2
    ar = a2r * a1r - a2i * a1i
    ai = a2r * a1i + a2i * a1r
    br = a2r * b1r - a2i * b1i + b2r
    bi = a2r * b1i + a2i * b1r + b2i
    return (ar, ai, br, bi)


def s5_mixer(u, lam_re, lam_im, log_step, b_re, b_im, c_re, c_im, d, glu_w, glu_b):
    bsz, L, _ = u.shape
    f32 = jnp.float32
    ug = u.reshape(bsz, L, SSM_GROUPS, SSM_GROUP).astype(f32)
    lr = jnp.minimum(lam_re.astype(f32), LAMBDA_RE_MAX)
    li = lam_im.astype(f32)
    dt = jnp.exp(log_step.astype(f32))[:, None]
    mag = jnp.exp(lr * dt)
    ang = li * dt
    abar_re = mag * jnp.cos(ang)
    abar_im = mag * jnp.sin(ang)
    nr = abar_re - 1.0
    ni = abar_im
    den = lr * lr + li * li
    q_re = (nr * lr + ni * li) / den
    q_im = (ni * lr - nr * li) / den
    b_re32 = b_re.astype(f32)
    b_im32 = b_im.astype(f32)
    bbar_re = q_re[..., None] * b_re32 - q_im[..., None] * b_im32
    bbar_im = q_re[..., None] * b_im32 + q_im[..., None] * b_re32
    bu_re = jnp.einsum('blgh,gph->blgp', ug, bbar_re)
    bu_im = jnp.einsum('blgh,gph->blgp', ug, bbar_im)
    a_re = jnp.broadcast_to(abar_re, bu_re.shape)
    a_im = jnp.broadcast_to(abar_im, bu_im.shape)
    _, _, s_re, s_im = lax.associative_scan(
        _complex_linear_combine, (a_re, a_im, bu_re, bu_im), axis=1)
    y = (jnp.einsum('blgp,ghp->blgh', s_re, c_re.astype(f32))
         - jnp.einsum('blgp,ghp->blgh', s_im, c_im.astype(f32))
         + d.astype(f32) * ug)
    y = y.reshape(bsz, L, D_SSM).astype(u.dtype)
    y = jax.nn.gelu(y)
    return y * jax.nn.sigmoid(y @ glu_w + glu_b)


def sgu_mixer(u, v, ln_g, ln_b, w_s, b_s):
    bsz, L, _ = u.shape
    u = jax.nn.gelu(u)
    v = layernorm(jax.nn.gelu(v), ln_g, ln_b)
    vc = v.reshape(bsz, L // SGU_CHUNK, SGU_CHUNK, SGU_HEADS, SGU_HEAD_DIM)
    mask = jnp.tril(jnp.ones((SGU_CHUNK, SGU_CHUNK), dtype=bool))
    w = jnp.where(mask[None], w_s, jnp.zeros_like(w_s))
    s = jnp.einsum('hts,bnshc->bnthc', w, vc) + b_s.T[None, None, :, :, None]
    return u * s.reshape(bsz, L, D_SGU)


def setup_inputs(seed: int = 0) -> dict:
    key = jax.random.key(seed)
    ks = jax.random.split(key, 32)
    f32 = jnp.float32
    nrm = lambda k, shape, scale: jax.random.normal(k, shape, f32) * scale
    gain = lambda k, shape: 1.0 + 0.01 * jax.random.normal(k, shape, f32)
    x = jax.random.normal(ks[0], (BATCH, SEQ, D_MODEL), f32)
    p = jax.random.normal(ks[1], (DEPTH, BATCH, SEQ, PLE_DIM), f32)
    norm_mix_g = gain(ks[2], (DEPTH, D_MODEL))
    w_in = nrm(ks[3], (DEPTH, D_MODEL, D_SSM + 2 * D_SGU), D_MODEL ** -0.5)
    ssm_lambda_re = -0.5 + 0.01 * jax.random.normal(ks[4], (DEPTH, SSM_GROUPS, SSM_STATE), f32)
    ssm_lambda_im = (jnp.pi * jnp.arange(SSM_STATE, dtype=f32))[None, None, :] \
        + 0.01 * jax.random.normal(ks[5], (DEPTH, SSM_GROUPS, SSM_STATE), f32)
    ssm_log_step = math.log(DT_MIN) + jax.random.uniform(ks[6], (DEPTH, SSM_GROUPS), f32) \
        * (math.log(DT_MAX) - math.log(DT_MIN))
    bs = (2.0 * SSM_GROUP) ** -0.5
    cs = (2.0 * SSM_STATE) ** -0.5
    ssm_b_re = nrm(ks[7], (DEPTH, SSM_GROUPS, SSM_STATE, SSM_GROUP), bs)
    ssm_b_im = nrm(ks[8], (DEPTH, SSM_GROUPS, SSM_STATE, SSM_GROUP), bs)
    ssm_c_re = nrm(ks[9], (DEPTH, SSM_GROUPS, SSM_GROUP, SSM_STATE), cs)
    ssm_c_im = nrm(ks[10], (DEPTH, SSM_GROUPS, SSM_GROUP, SSM_STATE), cs)
    ssm_d = nrm(ks[11], (DEPTH, SSM_GROUPS, SSM_GROUP), 0.5)
    ssm_glu_w = nrm(ks[12], (DEPTH, D_SSM, D_SSM), D_SSM ** -0.5)
    ssm_glu_b = nrm(ks[13], (DEPTH, D_SSM), 0.01)
    sgu_ln_g = gain(ks[14], (DEPTH, D_SGU))
    sgu_ln_b = nrm(ks[15], (DEPTH, D_SGU), 0.01)
    sgu_w = nrm(ks[16], (DEPTH, SGU_HEADS, SGU_CHUNK, SGU_CHUNK), SGU_CHUNK ** -0.5)
    sgu_b = gain(ks[17], (DEPTH, SGU_HEADS, SGU_CHUNK))
    out_norm_ssm_g = gain(ks[18], (DEPTH, D_SSM))
    out_norm_sgu_g = gain(ks[19], (DEPTH, D_SGU))
    w_out = nrm(ks[20], (DEPTH, D_MIX, D_MODEL), D_MIX ** -0.5)
    norm_ffn_g = gain(ks[21], (DEPTH, D_MODEL))
    w_ffn_in = nrm(ks[22], (DEPTH, D_MODEL, 2 * D_FFN), D_MODEL ** -0.5)
    w_ffn_out = nrm(ks[23], (DEPTH, D_FFN, D_MODEL), D_FFN ** -0.5)
    norm_ple_g = gain(ks[24], (DEPTH, D_MODEL))
    w_ple_gate = nrm(ks[25], (DEPTH, D_MODEL, D_MODEL), D_MODEL ** -0.5)
    b_ple_gate = nrm(ks[26], (DEPTH, D_MODEL), 0.01)
    w_ple_proj = nrm(ks[27], (DEPTH, PLE_DIM, D_MODEL), PLE_DIM ** -0.5)
    final_norm_g = gain(ks[28], (D_MODEL,))
    return {
        "x": x, "p": p, "norm_mix_g": norm_mix_g, "w_in": w_in,
        "ssm_lambda_re": ssm_lambda_re, "ssm_lambda_im": ssm_lambda_im,
        "ssm_log_step": ssm_log_step, "ssm_b_re": ssm_b_re, "ssm_b_im": ssm_b_im,
        "ssm_c_re": ssm_c_re, "ssm_c_im": ssm_c_im, "ssm_d": ssm_d,
        "ssm_glu_w": ssm_glu_w, "ssm_glu_b": ssm_glu_b,
        "sgu_ln_g": sgu_ln_g, "sgu_ln_b": sgu_ln_b, "sgu_w": sgu_w, "sgu_b": sgu_b,
        "out_norm_ssm_g": out_norm_ssm_g, "out_norm_sgu_g": out_norm_sgu_g,
        "w_out": w_out, "norm_ffn_g": norm_ffn_g, "w_ffn_in": w_ffn_in,
        "w_ffn_out": w_ffn_out, "norm_ple_g": norm_ple_g, "w_ple_gate": w_ple_gate,
        "b_ple_gate": b_ple_gate, "w_ple_proj": w_ple_proj, "final_norm_g": final_norm_g,
    }


def reference(x, p, norm_mix_g, w_in, ssm_lambda_re, ssm_lambda_im, ssm_log_step,
              ssm_b_re, ssm_b_im, ssm_c_re, ssm_c_im, ssm_d, ssm_glu_w, ssm_glu_b,
              sgu_ln_g, sgu_ln_b, sgu_w, sgu_b, out_norm_ssm_g, out_norm_sgu_g,
              w_out, norm_ffn_g, w_ffn_in, w_ffn_out, norm_ple_g, w_ple_gate,
              b_ple_gate, w_ple_proj, final_norm_g):
    for i in range(DEPTH):
        h = rmsnorm(x, norm_mix_g[i])
        z = h @ w_in[i]
        z_ssm = z[..., :D_SSM]
        z_u = z[..., D_SSM:D_SSM + D_SGU]
        z_v = z[..., D_SSM + D_SGU:]
        y_a = s5_mixer(z_ssm, ssm_lambda_re[i], ssm_lambda_im[i], ssm_log_step[i],
                       ssm_b_re[i], ssm_b_im[i], ssm_c_re[i], ssm_c_im[i], ssm_d[i],
                       ssm_glu_w[i], ssm_glu_b[i])
        y_b = sgu_mixer(z_u, z_v, sgu_ln_g[i], sgu_ln_b[i], sgu_w[i], sgu_b[i])
        y = jnp.concatenate([rmsnorm(y_a, out_norm_ssm_g[i]),
                             rmsnorm(y_b, out_norm_sgu_g[i])], axis=-1)
        x = x + y @ w_out[i]
        h = rmsnorm(x, norm_ffn_g[i])
        gu = h @ w_ffn_in[i]
        x = x + (jax.nn.silu(gu[..., :D_FFN]) * gu[..., D_FFN:]) @ w_ffn_out[i]
        h = rmsnorm(x, norm_ple_g[i])
        gate = jax.nn.sigmoid(h @ w_ple_gate[i] + b_ple_gate[i])
        x = x + gate * (p[i] @ w_ple_proj[i])
    return rmsnorm(x, final_norm_g)
```

```python
import functools

import numpy as np
import jax
import jax.numpy as jnp
from jax import lax
from jax.experimental import pallas as pl
from jax.experimental.pallas import tpu as pltpu

D_MODEL = 2048
D_SSM = 1024
D_SGU = 1024
SSM_GROUP = 16
SSM_GROUPS = 64
SSM_STATE = 64
SGU_CHUNK = 128
SGU_HEADS = 8
SGU_HEAD_DIM = D_SGU // SGU_HEADS
D_FFN = 5632
PLE_DIM = 256
EPS = 1e-6
LAMBDA_RE_MAX = -1e-4

SUBLANES = 8
N_STATE = SSM_GROUPS * SSM_STATE
N_BLK = 4
BLK_CH = D_SSM // N_BLK
BLK_ST = N_STATE // N_BLK
BLK_W = 2 * BLK_ST
T_CHUNK = 256
K_STEPS = T_CHUNK // SUBLANES
SCAN_W = 512

ROWS_IN = 512
ROWS_FFN = 512
FFN_BLK = 512
VMEM_LIMIT = 56 * 1024 * 1024

F32 = jnp.float32
BF16 = jnp.bfloat16


def _rms(x, g):
    r = lax.rsqrt(jnp.mean(x * x, axis=-1, keepdims=True) + EPS)
    return (x * r) * g


def _dot(a, b):
    return jnp.dot(a, b, preferred_element_type=F32)


def _resident(shape):
    nd = len(shape)
    return pl.BlockSpec(shape, lambda *_: (0,) * nd, pipeline_mode=pl.Buffered(1))


def _s5_params_kernel(lr_ref, li_ref, ls_ref, n_ref, bre_ref, bim_ref,
                      pwr_ref, pwi_ref, bbr_ref, bbi_ref):
    lr = jnp.minimum(lr_ref[...], LAMBDA_RE_MAX)
    li = li_ref[...]
    dt = jnp.exp(ls_ref[...])
    n = n_ref[...]
    mag = jnp.exp(n * (lr * dt))
    ang = n * (li * dt)
    pwr_ref[...] = mag * jnp.cos(ang)
    pwi_ref[...] = mag * jnp.sin(ang)
    mag1 = jnp.exp(lr * dt)
    ang1 = li * dt
    nr = mag1 * jnp.cos(ang1) - 1.0
    ni = mag1 * jnp.sin(ang1)
    den = lr * lr + li * li
    q_re = (nr * lr + ni * li) / den
    q_im = (ni * lr - nr * li) / den
    bre = bre_ref[...]
    bim = bim_ref[...]
    bbr_ref[...] = q_re * bre - q_im * bim
    bbi_ref[...] = q_re * bim + q_im * bre


def _s5_params(lam_re, lam_im, log_step, b_re, b_im, exponents):
    n_rows = exponents.shape[0]
    lr = lam_re.reshape(1, N_STATE)
    li = lam_im.reshape(1, N_STATE)
    ls = jnp.repeat(log_step, SSM_STATE).reshape(1, N_STATE)
    bre = b_re.transpose(2, 0, 1).reshape(SSM_GROUP, N_STATE)
    bim = b_im.transpose(2, 0, 1).reshape(SSM_GROUP, N_STATE)
    return pl.pallas_call(
        _s5_params_kernel,
        out_shape=(jax.ShapeDtypeStruct((n_rows, N_STATE), F32),
                   jax.ShapeDtypeStruct((n_rows, N_STATE), F32),
                   jax.ShapeDtypeStruct((SSM_GROUP, N_STATE), F32),
                   jax.ShapeDtypeStruct((SSM_GROUP, N_STATE), F32)),
    )(lr, li, ls, exponents, bre, bim)


def _block_layout(re, im):
    r = re.shape[0]
    cat = jnp.concatenate([re.reshape(r, N_BLK, BLK_ST), im.reshape(r, N_BLK, BLK_ST)], axis=-1)
    return cat.reshape(r, N_BLK * BLK_W)


def _in_proj_kernel(x_ref, g_ref, w_ref, perm_ref, lng_ref, lnb_ref, sw_ref, sbt_ref, og_ref,
                    zs_ref, yb_ref, s_scr):
    rows = x_ref.shape[0]
    h = _rms(x_ref[...], g_ref[...]).astype(BF16)
    perm = perm_ref[...]
    hp = jnp.concatenate(
        [_dot(perm, h[c * T_CHUNK:(c + 1) * T_CHUNK]).astype(BF16) for c in range(rows // T_CHUNK)],
        axis=0)
    zs_ref[...] = _dot(hp, w_ref[:, 0:D_SSM])
    u = jax.nn.gelu(_dot(h, w_ref[:, D_SSM:D_SSM + D_SGU]))
    gv = jax.nn.gelu(_dot(h, w_ref[:, D_SSM + D_SGU:]))
    mu = jnp.mean(gv, axis=-1, keepdims=True)
    xc = gv - mu
    r = lax.rsqrt(jnp.mean(xc * xc, axis=-1, keepdims=True) + EPS)
    v = ((xc * r) * lng_ref[...] + lnb_ref[...]).astype(BF16)
    ti = lax.broadcasted_iota(jnp.int32, (SGU_CHUNK, SGU_CHUNK), 0)
    si = lax.broadcasted_iota(jnp.int32, (SGU_CHUNK, SGU_CHUNK), 1)
    causal = si <= ti
    sbt = sbt_ref[...]
    for hd in range(SGU_HEADS):
        wm = jnp.where(causal, sw_ref[hd], jnp.zeros((), BF16))
        cols = slice(hd * SGU_HEAD_DIM, (hd + 1) * SGU_HEAD_DIM)
        bias = sbt[:, hd:hd + 1]
        for c in range(rows // SGU_CHUNK):
            rs = slice(c * SGU_CHUNK, (c + 1) * SGU_CHUNK)
            s_scr[rs, cols] = _dot(wm, v[rs, cols]) + bias
    yb = u * s_scr[...]
    yb_ref[...] = _rms(yb, og_ref[...]).astype(BF16)


def _in_proj(x2, g, w_in, perm, ln_g, ln_b, sgu_w, sgu_bt, og):
    n = x2.shape[0]
    row = lambda i: (i, 0)
    return pl.pallas_call(
        _in_proj_kernel,
        grid=(n // ROWS_IN,),
        in_specs=[
            pl.BlockSpec((ROWS_IN, D_MODEL), row),
            _resident((1, D_MODEL)),
            _resident(w_in.shape),
            _resident(perm.shape),
            _resident((1, D_SGU)),
            _resident((1, D_SGU)),
            _resident(sgu_w.shape),
            _resident(sgu_bt.shape),
            _resident((1, D_SGU)),
        ],
        out_specs=(pl.BlockSpec((ROWS_IN, D_SSM), row), pl.BlockSpec((ROWS_IN, D_SGU), row)),
        out_shape=(jax.ShapeDtypeStruct((n, D_SSM), F32), jax.ShapeDtypeStruct((n, D_SGU), BF16)),
        scratch_shapes=[pltpu.VMEM((ROWS_IN, D_SGU), F32)],
        compiler_params=pltpu.CompilerParams(
            dimension_semantics=("arbitrary",), vmem_limit_bytes=VMEM_LIMIT),
    )(x2, g, w_in, perm, ln_g, ln_b, sgu_w, sgu_bt, og)


def _s5_kernel(zp_ref, bd_ref, cd_ref, a1_ref, akm_ref, p8_ref, apow_ref, d_ref,
               gluw_ref, glub_ref, og_ref, permt_ref,
               out_ref, st_ref, sbf_ref, carry_ref, y_ref):
    @pl.when(pl.program_id(1) == 0)
    def _():
        carry_ref[...] = jnp.zeros_like(carry_ref)

    zp = zp_ref[...]
    zpb = zp.astype(BF16)
    seg = lax.broadcasted_iota(jnp.int32, (SUBLANES, BLK_ST), 0)
    for j in range(N_BLK):
        lo = j * BLK_W
        st_ref[...] = _dot(zpb[:, j * BLK_CH:(j + 1) * BLK_CH], bd_ref[j])

        for c0 in range(0, BLK_ST, SCAN_W):
            re_cols = slice(c0, c0 + SCAN_W)
            im_cols = slice(BLK_ST + c0, BLK_ST + c0 + SCAN_W)
            ar = jnp.broadcast_to(a1_ref[:, lo + c0:lo + c0 + SCAN_W], (SUBLANES, SCAN_W))
            ai = jnp.broadcast_to(a1_ref[:, lo + BLK_ST + c0:lo + BLK_ST + c0 + SCAN_W],
                                  (SUBLANES, SCAN_W))

            def scan_step(k, s, re_cols=re_cols, im_cols=im_cols, ar=ar, ai=ai):
                sr, si = s
                rows = pl.ds(pl.multiple_of(k * SUBLANES, SUBLANES), SUBLANES)
                nr = ar * sr - ai * si + st_ref[rows, re_cols]
                ni = ar * si + ai * sr + st_ref[rows, im_cols]
                st_ref[rows, re_cols] = nr
                st_ref[rows, im_cols] = ni
                return nr, ni

            zero = jnp.zeros((SUBLANES, SCAN_W), F32)
            lax.fori_loop(0, K_STEPS, scan_step, (zero, zero), unroll=4)

        xr = st_ref[T_CHUNK - SUBLANES:T_CHUNK, 0:BLK_ST]
        xi = st_ref[T_CHUNK - SUBLANES:T_CHUNK, BLK_ST:BLK_W]
        for lvl, m in enumerate((1, 2, 4)):
            mr = akm_ref[lvl:lvl + 1, lo:lo + BLK_ST]
            mi = akm_ref[lvl:lvl + 1, lo + BLK_ST:lo + BLK_W]
            rr = pltpu.roll(xr, m, 0)
            ri = pltpu.roll(xi, m, 0)
            keep = seg >= m
            xr, xi = (xr + jnp.where(keep, mr * rr - mi * ri, 0.0),
                      xi + jnp.where(keep, mr * ri + mi * rr, 0.0))
        cr = carry_ref[:, lo:lo + BLK_ST]
        ci = carry_ref[:, lo + BLK_ST:lo + BLK_W]
        pr = p8_ref[:, lo:lo + BLK_ST]
        pi = p8_ref[:, lo + BLK_ST:lo + BLK_W]
        xr, xi = xr + pr * cr - pi * ci, xi + pr * ci + pi * cr
        s0r = jnp.where(seg >= 1, pltpu.roll(xr, 1, 0), cr)
        s0i = jnp.where(seg >= 1, pltpu.roll(xi, 1, 0), ci)
        carry_ref[:, lo:lo + BLK_ST] = jnp.broadcast_to(xr[SUBLANES - 1:SUBLANES], (SUBLANES, BLK_ST))
        carry_ref[:, lo + BLK_ST:lo + BLK_W] = jnp.broadcast_to(
            xi[SUBLANES - 1:SUBLANES], (SUBLANES, BLK_ST))

        def fix_step(m, _, lo=lo, s0r=s0r, s0i=s0i):
            parts = []
            for dk in range(2):
                k = 2 * m + dk
                rows = pl.ds(pl.multiple_of(k * SUBLANES, SUBLANES), SUBLANES)
                wr = apow_ref[pl.ds(k, 1), lo:lo + BLK_ST]
                wi = apow_ref[pl.ds(k, 1), lo + BLK_ST:lo + BLK_W]
                fr = st_ref[rows, 0:BLK_ST] + (wr * s0r - wi * s0i)
                fi = st_ref[rows, BLK_ST:BLK_W] + (wr * s0i + wi * s0r)
                parts.append(jnp.concatenate([fr, fi], axis=1))
            tile_rows = pl.ds(pl.multiple_of(m * 2 * SUBLANES, 2 * SUBLANES), 2 * SUBLANES)
            sbf_ref[tile_rows, :] = jnp.concatenate(parts, axis=0).astype(BF16)
            return 0

        lax.fori_loop(0, K_STEPS // 2, fix_step, 0, unroll=2)

        y_ref[:, j * BLK_CH:(j + 1) * BLK_CH] = _dot(sbf_ref[...], cd_ref[j])

    y = jax.nn.gelu(y_ref[...] + d_ref[...] * zp)
    gate = jax.nn.sigmoid(_dot(y.astype(BF16), gluw_ref[...]) + glub_ref[...])
    ya = _rms(y * gate, og_ref[...]).astype(BF16)
    out_ref[...] = _dot(permt_ref[...], ya).astype(BF16)


def _s5(zs, bsz, seq, bd, cd, a1, akm, p8, apow, dvec, glu_w, glu_b, og, permt):
    chunks = seq // T_CHUNK
    row = lambda b, c: (b * chunks + c, 0)
    return pl.pallas_call(
        _s5_kernel,
        grid=(bsz, chunks),
        in_specs=[
            pl.BlockSpec((T_CHUNK, D_SSM), row),
            _resident(bd.shape), _resident(cd.shape),
            _resident(a1.shape), _resident(akm.shape), _resident(p8.shape), _resident(apow.shape),
            _resident(dvec.shape), _resident(glu_w.shape), _resident(glu_b.shape),
            _resident(og.shape), _resident(permt.shape),
        ],
        out_specs=pl.BlockSpec((T_CHUNK, D_SSM), row),
        out_shape=jax.ShapeDtypeStruct((bsz * seq, D_SSM), BF16),
        scratch_shapes=[
            pltpu.VMEM((T_CHUNK, BLK_W), F32),
            pltpu.VMEM((T_CHUNK, BLK_W), BF16),
            pltpu.VMEM((SUBLANES, N_BLK * BLK_W), F32),
            pltpu.VMEM((T_CHUNK, D_SSM), F32),
        ],
        compiler_params=pltpu.CompilerParams(
            dimension_semantics=("arbitrary", "arbitrary"), vmem_limit_bytes=VMEM_LIMIT),
    )(zs, bd, cd, a1, akm, p8, apow, dvec, glu_w, glu_b, og, permt)


def _out_proj_kernel(x_ref, ya_ref, yb_ref, w_ref, g_ref, x1_ref, h_ref):
    x1 = x_ref[...] + _dot(ya_ref[...], w_ref[0:D_SSM, :]) + _dot(yb_ref[...], w_ref[D_SSM:, :])
    x1_ref[...] = x1
    h_ref[...] = _rms(x1, g_ref[...]).astype(BF16)


def _out_proj(x2, ya, yb, w_out, g):
    n = x2.shape[0]
    row = lambda i: (i, 0)
    return pl.pallas_call(
        _out_proj_kernel,
        grid=(n // ROWS_IN,),
        in_specs=[
            pl.BlockSpec((ROWS_IN, D_MODEL), row),
            pl.BlockSpec((ROWS_IN, D_SSM), row),
            pl.BlockSpec((ROWS_IN, D_SGU), row),
            _resident(w_out.shape),
            _resident((1, D_MODEL)),
        ],
        out_specs=(pl.BlockSpec((ROWS_IN, D_MODEL), row), pl.BlockSpec((ROWS_IN, D_MODEL), row)),
        out_shape=(jax.ShapeDtypeStruct((n, D_MODEL), F32), jax.ShapeDtypeStruct((n, D_MODEL), BF16)),
        compiler_params=pltpu.CompilerParams(
            dimension_semantics=("arbitrary",), vmem_limit_bytes=VMEM_LIMIT),
    )(x2, ya, yb, w_out, g)


def _ffn_kernel(h_ref, x1_ref, wg_ref, wu_ref, wo_ref, out_ref):
    @pl.when(pl.program_id(1) == 0)
    def _():
        out_ref[...] = x1_ref[...]

    h = h_ref[...]
    act = jax.nn.silu(_dot(h, wg_ref[...])) * _dot(h, wu_ref[...])
    out_ref[...] += _dot(act.astype(BF16), wo_ref[...])


def _ffn(h, x1, w_in, w_out):
    n = h.shape[0]
    nf = D_FFN // FFN_BLK
    return pl.pallas_call(
        _ffn_kernel,
        grid=(n // ROWS_FFN, nf),
        in_specs=[
            pl.BlockSpec((ROWS_FFN, D_MODEL), lambda i, f: (i, 0)),
            pl.BlockSpec((ROWS_FFN, D_MODEL), lambda i, f: (i, 0)),
            pl.BlockSpec((D_MODEL, FFN_BLK), lambda i, f: (0, f)),
            pl.BlockSpec((D_MODEL, FFN_BLK), lambda i, f: (0, f + nf)),
            pl.BlockSpec((FFN_BLK, D_MODEL), lambda i, f: (f, 0)),
        ],
        out_specs=pl.BlockSpec((ROWS_FFN, D_MODEL), lambda i, f: (i, 0)),
        out_shape=jax.ShapeDtypeStruct((n, D_MODEL), F32),
        compiler_params=pltpu.CompilerParams(
            dimension_semantics=("arbitrary", "arbitrary"), vmem_limit_bytes=VMEM_LIMIT),
    )(h, x1, w_in, w_in, w_out)


def _ple_kernel(x_ref, p_ref, wg_ref, bg_ref, wp_ref, g_ref, gf_ref, out_ref, *, final):
    x = x_ref[...]
    h = _rms(x, g_ref[...]).astype(BF16)
    gate = jax.nn.sigmoid(_dot(h, wg_ref[...]) + bg_ref[...])
    x3 = x + gate * _dot(p_ref[...].astype(BF16), wp_ref[...])
    out_ref[...] = _rms(x3, gf_ref[...]) if final else x3


def _ple(x2, p2, w_gate, b_gate, w_proj, g, gf, final):
    n = x2.shape[0]
    row = lambda i: (i, 0)
    return pl.pallas_call(
        functools.partial(_ple_kernel, final=final),
        grid=(n // ROWS_IN,),
        in_specs=[
            pl.BlockSpec((ROWS_IN, D_MODEL), row),
            pl.BlockSpec((ROWS_IN, PLE_DIM), row),
            _resident(w_gate.shape),
            _resident((1, D_MODEL)),
            _resident(w_proj.shape),
            _resident((1, D_MODEL)),
            _resident((1, D_MODEL)),
        ],
        out_specs=pl.BlockSpec((ROWS_IN, D_MODEL), row),
        out_shape=jax.ShapeDtypeStruct((n, D_MODEL), F32),
        compiler_params=pltpu.CompilerParams(
            dimension_semantics=("arbitrary",), vmem_limit_bytes=VMEM_LIMIT),
    )(x2, p2, w_gate, b_gate, w_proj, g, gf)


def _segment_permutation():
    r = np.arange(T_CHUNK)
    src = (r % SUBLANES) * K_STEPS + r // SUBLANES
    perm = np.zeros((T_CHUNK, T_CHUNK), np.float32)
    perm[r, src] = 1.0
    return perm


def _scan_exponents():
    n = [1.0]
    n += [float(K_STEPS * m) for m in (1, 2, 4)]
    n += [float(K_STEPS * (i + 1)) for i in range(SUBLANES)]
    n += [float(k + 1) for k in range(K_STEPS)]
    return np.asarray(n, np.float32).reshape(-1, 1)


def kernel(x, p, norm_mix_g, w_in, ssm_lambda_re, ssm_lambda_im, ssm_log_step, ssm_b_re, ssm_b_im, ssm_c_re, ssm_c_im, ssm_d, ssm_glu_w, ssm_glu_b, sgu_ln_g, sgu_ln_b, sgu_w, sgu_b, out_norm_ssm_g, out_norm_sgu_g, w_out, norm_ffn_g, w_ffn_in, w_ffn_out, norm_ple_g, w_ple_gate, b_ple_gate, w_ple_proj, final_norm_g):
    bsz, seq, _ = x.shape
    depth = w_in.shape[0]
    perm_np = _segment_permutation()
    perm = jnp.asarray(perm_np, BF16)
    permt = jnp.asarray(perm_np.T, BF16)
    exponents = jnp.asarray(_scan_exponents())
    eye = jnp.eye(SSM_GROUP, dtype=F32)
    x2 = x.reshape(bsz * seq, D_MODEL)
    vec = lambda a: a.reshape(1, -1)
    for i in range(depth):
        pw_re, pw_im, bb_re, bb_im = _s5_params(
            ssm_lambda_re[i], ssm_lambda_im[i], ssm_log_step[i], ssm_b_re[i], ssm_b_im[i], exponents)
        tables = _block_layout(pw_re, pw_im)
        a1, akm, p8, apow = tables[0:1], tables[1:4], tables[4:12], tables[12:]
        shape_b = (SSM_GROUP, N_BLK, SSM_GROUP, SSM_STATE)
        bd = jnp.concatenate(
            [jnp.einsum('hjgp,gq->jghqp', t.reshape(shape_b), eye).reshape(N_BLK, BLK_CH, BLK_ST)
             for t in (bb_re, bb_im)], axis=-1).astype(BF16)
        shape_c = (N_BLK, SSM_GROUP, SSM_GROUP, SSM_STATE)
        cd = jnp.concatenate(
            [jnp.einsum('jghp,gq->jqpgh', t.reshape(shape_c), eye).reshape(N_BLK, BLK_ST, BLK_CH)
             for t in (ssm_c_re[i], -ssm_c_im[i])], axis=1).astype(BF16)

        zs, yb = _in_proj(x2, vec(norm_mix_g[i]), w_in[i].astype(BF16), perm,
                          vec(sgu_ln_g[i]), vec(sgu_ln_b[i]), sgu_w[i].astype(BF16),
                          sgu_b[i].T, vec(out_norm_sgu_g[i]))
        ya = _s5(zs, bsz, seq, bd, cd, a1, akm, p8, apow, vec(ssm_d[i]),
                 ssm_glu_w[i].astype(BF16), vec(ssm_glu_b[i]), vec(out_norm_ssm_g[i]), permt)
        x1, h = _out_proj(x2, ya, yb, w_out[i].astype(BF16), vec(norm_ffn_g[i]))
        x2 = _ffn(h, x1, w_ffn_in[i].astype(BF16), w_ffn_out[i].astype(BF16))
        x2 = _ple(x2, p[i].reshape(bsz * seq, PLE_DIM), w_ple_gate[i].astype(BF16),
                  vec(b_ple_gate[i]), w_ple_proj[i].astype(BF16), vec(norm_ple_g[i]),
                  vec(final_norm_g), final=(i == depth - 1))
    return x2.reshape(bsz, seq, D_MODEL)
```

```python
import functools

import numpy as np
import jax
import jax.numpy as jnp
from jax import lax
from jax.experimental import pallas as pl
from jax.experimental.pallas import tpu as pltpu

D_MODEL = 2048
D_SSM = 1024
D_SGU = 1024
SSM_GROUP = 16
SSM_GROUPS = 64
SSM_STATE = 64
SGU_CHUNK = 128
SGU_HEADS = 8
SGU_HEAD_DIM = D_SGU // SGU_HEADS
D_FFN = 5632
PLE_DIM = 256
EPS = 1e-6
LAMBDA_RE_MAX = -1e-4

SUBLANES = 8
N_STATE = SSM_GROUPS * SSM_STATE
N_BLK = 4
BLK_CH = D_SSM // N_BLK
BLK_ST = N_STATE // N_BLK
BLK_W = 2 * BLK_ST
T_CHUNK = 256
K_STEPS = T_CHUNK // SUBLANES
S5_ROWS = 512
SCAN_W = 512

ROWS_IN = 512
ROWS_FFN = 512
FFN_BLK = 512
VMEM_LIMIT = 56 * 1024 * 1024

F32 = jnp.float32
BF16 = jnp.bfloat16


def _rms(x, g):
    r = lax.rsqrt(jnp.mean(x * x, axis=-1, keepdims=True) + EPS)
    return (x * r) * g


def _dot(a, b):
    return jnp.dot(a, b, preferred_element_type=F32)


def _resident(shape):
    nd = len(shape)
    return pl.BlockSpec(shape, lambda *_: (0,) * nd, pipeline_mode=pl.Buffered(1))


def _s5_params_kernel(lr_ref, li_ref, ls_ref, n_ref, bre_ref, bim_ref,
                      pwr_ref, pwi_ref, bbr_ref, bbi_ref):
    lr = jnp.minimum(lr_ref[...], LAMBDA_RE_MAX)
    li = li_ref[...]
    dt = jnp.exp(ls_ref[...])
    n = n_ref[...]
    mag = jnp.exp(n * (lr * dt))
    ang = n * (li * dt)
    pwr_ref[...] = mag * jnp.cos(ang)
    pwi_ref[...] = mag * jnp.sin(ang)
    mag1 = jnp.exp(lr * dt)
    ang1 = li * dt
    nr = mag1 * jnp.cos(ang1) - 1.0
    ni = mag1 * jnp.sin(ang1)
    den = lr * lr + li * li
    q_re = (nr * lr + ni * li) / den
    q_im = (ni * lr - nr * li) / den
    bre = bre_ref[...]
    bim = bim_ref[...]
    bbr_ref[...] = q_re * bre - q_im * bim
    bbi_ref[...] = q_re * bim + q_im * bre


def _s5_params(lam_re, lam_im, log_step, b_re, b_im, exponents):
    n_rows = exponents.shape[0]
    lr = lam_re.reshape(1, N_STATE)
    li = lam_im.reshape(1, N_STATE)
    ls = jnp.repeat(log_step, SSM_STATE).reshape(1, N_STATE)
    bre = b_re.transpose(2, 0, 1).reshape(SSM_GROUP, N_STATE)
    bim = b_im.transpose(2, 0, 1).reshape(SSM_GROUP, N_STATE)
    return pl.pallas_call(
        _s5_params_kernel,
        name="s5_params",
        out_shape=(jax.ShapeDtypeStruct((n_rows, N_STATE), F32),
                   jax.ShapeDtypeStruct((n_rows, N_STATE), F32),
                   jax.ShapeDtypeStruct((SSM_GROUP, N_STATE), F32),
                   jax.ShapeDtypeStruct((SSM_GROUP, N_STATE), F32)),
    )(lr, li, ls, exponents, bre, bim)


def _block_layout(re, im):
    r = re.shape[0]
    cat = jnp.concatenate([re.reshape(r, N_BLK, BLK_ST), im.reshape(r, N_BLK, BLK_ST)], axis=-1)
    return cat.reshape(r, N_BLK * BLK_W)


def _in_proj_kernel(x_ref, g_ref, w_ref, perm_ref, lng_ref, lnb_ref, sw_ref, sbt_ref, og_ref,
                    zs_ref, yb_ref, s_scr):
    rows = x_ref.shape[0]
    h = _rms(x_ref[...], g_ref[...]).astype(BF16)
    perm = perm_ref[...]
    hp = jnp.concatenate(
        [_dot(perm, h[c * T_CHUNK:(c + 1) * T_CHUNK]).astype(BF16) for c in range(rows // T_CHUNK)],
        axis=0)
    zs_ref[...] = _dot(hp, w_ref[:, 0:D_SSM])
    u = jax.nn.gelu(_dot(h, w_ref[:, D_SSM:D_SSM + D_SGU]))
    gv = jax.nn.gelu(_dot(h, w_ref[:, D_SSM + D_SGU:]))
    mu = jnp.mean(gv, axis=-1, keepdims=True)
    xc = gv - mu
    r = lax.rsqrt(jnp.mean(xc * xc, axis=-1, keepdims=True) + EPS)
    v = ((xc * r) * lng_ref[...] + lnb_ref[...]).astype(BF16)
    ti = lax.broadcasted_iota(jnp.int32, (SGU_CHUNK, SGU_CHUNK), 0)
    si = lax.broadcasted_iota(jnp.int32, (SGU_CHUNK, SGU_CHUNK), 1)
    causal = si <= ti
    sbt = sbt_ref[...]
    for hd in range(SGU_HEADS):
        wm = jnp.where(causal, sw_ref[hd], jnp.zeros((), BF16))
        cols = slice(hd * SGU_HEAD_DIM, (hd + 1) * SGU_HEAD_DIM)
        bias = sbt[:, hd:hd + 1]
        for c in range(rows // SGU_CHUNK):
            rs = slice(c * SGU_CHUNK, (c + 1) * SGU_CHUNK)
            s_scr[rs, cols] = _dot(wm, v[rs, cols]) + bias
    yb = u * s_scr[...]
    yb_ref[...] = _rms(yb, og_ref[...]).astype(BF16)


def _in_proj(x2, g, w_in, perm, ln_g, ln_b, sgu_w, sgu_bt, og):
    n = x2.shape[0]
    row = lambda i: (i, 0)
    return pl.pallas_call(
        _in_proj_kernel,
        name="in_proj",
        grid=(n // ROWS_IN,),
        in_specs=[
            pl.BlockSpec((ROWS_IN, D_MODEL), row),
            _resident((1, D_MODEL)),
            _resident(w_in.shape),
            _resident(perm.shape),
            _resident((1, D_SGU)),
            _resident((1, D_SGU)),
            _resident(sgu_w.shape),
            _resident(sgu_bt.shape),
            _resident((1, D_SGU)),
        ],
        out_specs=(pl.BlockSpec((ROWS_IN, D_SSM), row), pl.BlockSpec((ROWS_IN, D_SGU), row)),
        out_shape=(jax.ShapeDtypeStruct((n, D_SSM), F32), jax.ShapeDtypeStruct((n, D_SGU), BF16)),
        scratch_shapes=[pltpu.VMEM((ROWS_IN, D_SGU), F32)],
        compiler_params=pltpu.CompilerParams(
            dimension_semantics=("arbitrary",), vmem_limit_bytes=VMEM_LIMIT),
    )(x2, g, w_in, perm, ln_g, ln_b, sgu_w, sgu_bt, og)


def _s5_kernel(zp_ref, bd_ref, cd_ref, a1_ref, akm_ref, p8_ref, d_ref,
               gluw_ref, glub_ref, og_ref, permt_ref,
               out_ref, bu_ref, sbf_ref, carry_ref, y_ref):
    @pl.when(pl.program_id(1) == 0)
    def _():
        carry_ref[...] = jnp.zeros_like(carry_ref)

    zp = zp_ref[...]
    zpb = zp.astype(BF16)
    seg = lax.broadcasted_iota(jnp.int32, (SUBLANES, SCAN_W), 0)
    zero = jnp.zeros((SUBLANES, SCAN_W), F32)
    for j in range(N_BLK):
        bu_ref[...] = _dot(zpb[:, j * BLK_CH:(j + 1) * BLK_CH], bd_ref[j])

        for c in range(S5_ROWS // T_CHUNK):
            base = c * T_CHUNK
            for c0 in range(0, BLK_ST, SCAN_W):
                re_cols = slice(c0, c0 + SCAN_W)
                im_cols = slice(BLK_ST + c0, BLK_ST + c0 + SCAN_W)
                tre = slice(j * BLK_W + c0, j * BLK_W + c0 + SCAN_W)
                tim = slice(j * BLK_W + BLK_ST + c0, j * BLK_W + BLK_ST + c0 + SCAN_W)
                ar = jnp.broadcast_to(a1_ref[:, tre], (SUBLANES, SCAN_W))
                ai = jnp.broadcast_to(a1_ref[:, tim], (SUBLANES, SCAN_W))

                def step(k, sr, si, base=base, re_cols=re_cols, im_cols=im_cols, ar=ar, ai=ai):
                    rows = pl.ds(pl.multiple_of(base + k * SUBLANES, SUBLANES), SUBLANES)
                    return (ar * sr - ai * si + bu_ref[rows, re_cols],
                            ar * si + ai * sr + bu_ref[rows, im_cols])

                xr, xi = lax.fori_loop(0, K_STEPS, lambda k, s, step=step: step(k, *s),
                                       (zero, zero), unroll=4)

                for lvl, m in enumerate((1, 2, 4)):
                    mr = akm_ref[lvl:lvl + 1, tre]
                    mi = akm_ref[lvl:lvl + 1, tim]
                    rr = pltpu.roll(xr, m, 0)
                    ri = pltpu.roll(xi, m, 0)
                    keep = seg >= m
                    xr, xi = (xr + jnp.where(keep, mr * rr - mi * ri, 0.0),
                              xi + jnp.where(keep, mr * ri + mi * rr, 0.0))
                cr = carry_ref[:, tre]
                ci = carry_ref[:, tim]
                pr = p8_ref[:, tre]
                pi = p8_ref[:, tim]
                xr, xi = xr + pr * cr - pi * ci, xi + pr * ci + pi * cr
                s0r = jnp.where(seg >= 1, pltpu.roll(xr, 1, 0), cr)
                s0i = jnp.where(seg >= 1, pltpu.roll(xi, 1, 0), ci)
                carry_ref[:, tre] = jnp.broadcast_to(xr[SUBLANES - 1:SUBLANES], (SUBLANES, SCAN_W))
                carry_ref[:, tim] = jnp.broadcast_to(xi[SUBLANES - 1:SUBLANES], (SUBLANES, SCAN_W))

                def emit(m, s, base=base, re_cols=re_cols, im_cols=im_cols, step=step):
                    r1, i1 = step(2 * m, *s)
                    r2, i2 = step(2 * m + 1, r1, i1)
                    rows = pl.ds(pl.multiple_of(base + m * 2 * SUBLANES, 2 * SUBLANES), 2 * SUBLANES)
                    sbf_ref[rows, re_cols] = jnp.concatenate([r1, r2], axis=0).astype(BF16)
                    sbf_ref[rows, im_cols] = jnp.concatenate([i1, i2], axis=0).astype(BF16)
                    return r2, i2

                lax.fori_loop(0, K_STEPS // 2, emit, (s0r, s0i), unroll=2)

        y_ref[:, j * BLK_CH:(j + 1) * BLK_CH] = _dot(sbf_ref[...], cd_ref[j])

    y = jax.nn.gelu(y_ref[...] + d_ref[...] * zp)
    gate = jax.nn.sigmoid(_dot(y.astype(BF16), gluw_ref[...]) + glub_ref[...])
    ya = _rms(y * gate, og_ref[...]).astype(BF16)
    permt = permt_ref[...]
    for c in range(S5_ROWS // T_CHUNK):
        rows = slice(c * T_CHUNK, (c + 1) * T_CHUNK)
        out_ref[rows, :] = _dot(permt, ya[rows]).astype(BF16)


def _s5(zs, bsz, seq, bd, cd, a1, akm, p8, dvec, glu_w, glu_b, og, permt):
    steps = seq // S5_ROWS
    row = lambda b, c: (b * steps + c, 0)
    return pl.pallas_call(
        _s5_kernel,
        name="s5",
        grid=(bsz, steps),
        in_specs=[
            pl.BlockSpec((S5_ROWS, D_SSM), row),
            _resident(bd.shape), _resident(cd.shape),
            _resident(a1.shape), _resident(akm.shape), _resident(p8.shape),
            _resident(dvec.shape), _resident(glu_w.shape), _resident(glu_b.shape),
            _resident(og.shape), _resident(permt.shape),
        ],
        out_specs=pl.BlockSpec((S5_ROWS, D_SSM), row),
        out_shape=jax.ShapeDtypeStruct((bsz * seq, D_SSM), BF16),
        scratch_shapes=[
            pltpu.VMEM((S5_ROWS, BLK_W), F32),
            pltpu.VMEM((S5_ROWS, BLK_W), BF16),
            pltpu.VMEM((SUBLANES, N_BLK * BLK_W), F32),
            pltpu.VMEM((S5_ROWS, D_SSM), F32),
        ],
        compiler_params=pltpu.CompilerParams(
            dimension_semantics=("arbitrary", "arbitrary"), vmem_limit_bytes=VMEM_LIMIT),
    )(zs, bd, cd, a1, akm, p8, dvec, glu_w, glu_b, og, permt)


def _out_proj_kernel(x_ref, ya_ref, yb_ref, w_ref, g_ref, x1_ref, h_ref):
    x1 = x_ref[...] + _dot(ya_ref[...], w_ref[0:D_SSM, :]) + _dot(yb_ref[...], w_ref[D_SSM:, :])
    x1_ref[...] = x1
    h_ref[...] = _rms(x1, g_ref[...]).astype(BF16)


def _out_proj(x2, ya, yb, w_out, g):
    n = x2.shape[0]
    row = lambda i: (i, 0)
    return pl.pallas_call(
        _out_proj_kernel,
        name="out_proj",
        grid=(n // ROWS_IN,),
        in_specs=[
            pl.BlockSpec((ROWS_IN, D_MODEL), row),
            pl.BlockSpec((ROWS_IN, D_SSM), row),
            pl.BlockSpec((ROWS_IN, D_SGU), row),
            _resident(w_out.shape),
            _resident((1, D_MODEL)),
        ],
        out_specs=(pl.BlockSpec((ROWS_IN, D_MODEL), row), pl.BlockSpec((ROWS_IN, D_MODEL), row)),
        out_shape=(jax.ShapeDtypeStruct((n, D_MODEL), F32), jax.ShapeDtypeStruct((n, D_MODEL), BF16)),
        compiler_params=pltpu.CompilerParams(
            dimension_semantics=("arbitrary",), vmem_limit_bytes=VMEM_LIMIT),
    )(x2, ya, yb, w_out, g)


def _ffn_kernel(h_ref, x1_ref, wg_ref, wu_ref, wo_ref, out_ref):
    @pl.when(pl.program_id(1) == 0)
    def _():
        out_ref[...] = x1_ref[...]

    h = h_ref[...]
    act = jax.nn.silu(_dot(h, wg_ref[...])) * _dot(h, wu_ref[...])
    out_ref[...] += _dot(act.astype(BF16), wo_ref[...])


def _ffn(h, x1, w_in, w_out):
    n = h.shape[0]
    nf = D_FFN // FFN_BLK
    return pl.pallas_call(
        _ffn_kernel,
        name="ffn",
        grid=(n // ROWS_FFN, nf),
        in_specs=[
            pl.BlockSpec((ROWS_FFN, D_MODEL), lambda i, f: (i, 0)),
            pl.BlockSpec((ROWS_FFN, D_MODEL), lambda i, f: (i, 0)),
            pl.BlockSpec((D_MODEL, FFN_BLK), lambda i, f: (0, f)),
            pl.BlockSpec((D_MODEL, FFN_BLK), lambda i, f: (0, f + nf)),
            pl.BlockSpec((FFN_BLK, D_MODEL), lambda i, f: (f, 0)),
        ],
        out_specs=pl.BlockSpec((ROWS_FFN, D_MODEL), lambda i, f: (i, 0)),
        out_shape=jax.ShapeDtypeStruct((n, D_MODEL), F32),
        compiler_params=pltpu.CompilerParams(
            dimension_semantics=("arbitrary", "arbitrary"), vmem_limit_bytes=VMEM_LIMIT),
    )(h, x1, w_in, w_in, w_out)


def _ple_kernel(x_ref, p_ref, wg_ref, bg_ref, wp_ref, g_ref, gf_ref, out_ref, *, final):
    x = x_ref[...]
    h = _rms(x, g_ref[...]).astype(BF16)
    gate = jax.nn.sigmoid(_dot(h, wg_ref[...]) + bg_ref[...])
    x3 = x + gate * _dot(p_ref[...].astype(BF16), wp_ref[...])
    out_ref[...] = _rms(x3, gf_ref[...]) if final else x3


def _ple(x2, p2, w_gate, b_gate, w_proj, g, gf, final):
    n = x2.shape[0]
    row = lambda i: (i, 0)
    return pl.pallas_call(
        functools.partial(_ple_kernel, final=final),
        name="ple",
        grid=(n // ROWS_IN,),
        in_specs=[
            pl.BlockSpec((ROWS_IN, D_MODEL), row),
            pl.BlockSpec((ROWS_IN, PLE_DIM), row),
            _resident(w_gate.shape),
            _resident((1, D_MODEL)),
            _resident(w_proj.shape),
            _resident((1, D_MODEL)),
            _resident((1, D_MODEL)),
        ],
        out_specs=pl.BlockSpec((ROWS_IN, D_MODEL), row),
        out_shape=jax.ShapeDtypeStruct((n, D_MODEL), F32),
        compiler_params=pltpu.CompilerParams(
            dimension_semantics=("arbitrary",), vmem_limit_bytes=VMEM_LIMIT),
    )(x2, p2, w_gate, b_gate, w_proj, g, gf)


def _segment_permutation():
    r = np.arange(T_CHUNK)
    src = (r % SUBLANES) * K_STEPS + r // SUBLANES
    perm = np.zeros((T_CHUNK, T_CHUNK), np.float32)
    perm[r, src] = 1.0
    return perm


def _scan_exponents():
    n = [1.0]
    n += [float(K_STEPS * m) for m in (1, 2, 4)]
    n += [float(K_STEPS * (i + 1)) for i in range(SUBLANES)]
    return np.asarray(n, np.float32).reshape(-1, 1)


def _block_diag_b(t):
    t = jnp.broadcast_to(t.reshape(SSM_GROUP, N_BLK, 1, BLK_ST),
                         (SSM_GROUP, N_BLK, SSM_GROUP, BLK_ST)).transpose(1, 2, 0, 3)
    same = (np.arange(SSM_GROUP)[:, None, None] == (np.arange(BLK_ST) // SSM_STATE)[None, None, :])
    return jnp.where(same, t, 0.0).reshape(N_BLK, BLK_CH, BLK_ST)


def _block_diag_c(t):
    t = t.reshape(N_BLK, SSM_GROUP, SSM_GROUP, SSM_STATE).transpose(0, 1, 3, 2)
    t = jnp.broadcast_to(t.reshape(N_BLK, BLK_ST, 1, SSM_GROUP), (N_BLK, BLK_ST, SSM_GROUP, SSM_GROUP))
    same = ((np.arange(BLK_ST) // SSM_STATE)[:, None, None] == np.arange(SSM_GROUP)[None, :, None])
    return jnp.where(same, t, 0.0).reshape(N_BLK, BLK_ST, BLK_CH)


def kernel(x, p, norm_mix_g, w_in, ssm_lambda_re, ssm_lambda_im, ssm_log_step, ssm_b_re, ssm_b_im, ssm_c_re, ssm_c_im, ssm_d, ssm_glu_w, ssm_glu_b, sgu_ln_g, sgu_ln_b, sgu_w, sgu_b, out_norm_ssm_g, out_norm_sgu_g, w_out, norm_ffn_g, w_ffn_in, w_ffn_out, norm_ple_g, w_ple_gate, b_ple_gate, w_ple_proj, final_norm_g):
    bsz, seq, _ = x.shape
    depth = w_in.shape[0]
    perm_np = _segment_permutation()
    perm = jnp.asarray(perm_np, BF16)
    permt = jnp.asarray(perm_np.T, BF16)
    exponents = jnp.asarray(_scan_exponents())
    x2 = x.reshape(bsz * seq, D_MODEL)
    vec = lambda a: a.reshape(1, -1)
    for i in range(depth):
        pw_re, pw_im, bb_re, bb_im = _s5_params(
            ssm_lambda_re[i], ssm_lambda_im[i], ssm_log_step[i], ssm_b_re[i], ssm_b_im[i], exponents)
        tables = _block_layout(pw_re, pw_im)
        a1, akm, p8 = tables[0:1], tables[1:4], tables[4:12]
        bd = jnp.concatenate([_block_diag_b(bb_re), _block_diag_b(bb_im)], axis=-1).astype(BF16)
        cd = jnp.concatenate([_block_diag_c(ssm_c_re[i]), _block_diag_c(-ssm_c_im[i])],
                             axis=1).astype(BF16)

        zs, yb = _in_proj(x2, vec(norm_mix_g[i]), w_in[i].astype(BF16), perm,
                          vec(sgu_ln_g[i]), vec(sgu_ln_b[i]), sgu_w[i].astype(BF16),
                          sgu_b[i].T, vec(out_norm_sgu_g[i]))
        ya = _s5(zs, bsz, seq, bd, cd, a1, akm, p8, vec(ssm_d[i]),
                 ssm_glu_w[i].astype(BF16), vec(ssm_glu_b[i]), vec(out_norm_ssm_g[i]), permt)
        x1, h = _out_proj(x2, ya, yb, w_out[i].astype(BF16), vec(norm_ffn_g[i]))
        x2 = _ffn(h, x1, w_ffn_in[i].astype(BF16), w_ffn_out[i].astype(BF16))
        x2 = _ple(x2, p[i].reshape(bsz * seq, PLE_DIM), w_ple_gate[i].astype(BF16),
                  vec(b_ple_gate[i]), w_ple_proj[i].astype(BF16), vec(norm_ple_g[i]),
                  vec(final_norm_g), final=(i == depth - 1))
    return x2.reshape(bsz, seq, D_MODEL)
```

```python
import functools

import numpy as np
import jax
import jax.numpy as jnp
from jax import lax
from jax.experimental import pallas as pl
from jax.experimental.pallas import tpu as pltpu

D_MODEL = 2048
D_SSM = 1024
D_SGU = 1024
SSM_GROUP = 16
SSM_GROUPS = 64
SSM_STATE = 64
SGU_CHUNK = 128
SGU_HEADS = 8
SGU_HEAD_DIM = D_SGU // SGU_HEADS
D_FFN = 5632
PLE_DIM = 256
EPS = 1e-6
LAMBDA_RE_MAX = -1e-4

SUBLANES = 8
N_STATE = SSM_GROUPS * SSM_STATE
N_BLK = 4
BLK_CH = D_SSM // N_BLK
BLK_ST = N_STATE // N_BLK
BLK_W = 2 * BLK_ST
T_CHUNK = 256
K_STEPS = T_CHUNK // SUBLANES
S5_ROWS = 512
SCAN_W = 512

ROWS_IN = 512
ROWS_FFN = 1024
FFN_BLK = 512
VMEM_LIMIT = 56 * 1024 * 1024

F32 = jnp.float32
BF16 = jnp.bfloat16


def _rms(x, g):
    r = lax.rsqrt(jnp.mean(x * x, axis=-1, keepdims=True) + EPS)
    return (x * r) * g


def _dot(a, b):
    return jnp.dot(a, b, preferred_element_type=F32)


def _resident(shape):
    nd = len(shape)
    return pl.BlockSpec(shape, lambda *_: (0,) * nd, pipeline_mode=pl.Buffered(1))


def _cast_block_spec(w, steps, index):
    return pl.BlockSpec((w.shape[0] // steps, w.shape[1]), index)


def _cast_out_shape(w):
    return jax.ShapeDtypeStruct(w.shape, BF16)


def _s5_params_kernel(lr_ref, li_ref, ls_ref, n_ref, bre_ref, bim_ref,
                      pwr_ref, pwi_ref, bbr_ref, bbi_ref):
    lr = jnp.minimum(lr_ref[...], LAMBDA_RE_MAX)
    li = li_ref[...]
    dt = jnp.exp(ls_ref[...])
    n = n_ref[...]
    mag = jnp.exp(n * (lr * dt))
    ang = n * (li * dt)
    pwr_ref[...] = mag * jnp.cos(ang)
    pwi_ref[...] = mag * jnp.sin(ang)
    mag1 = jnp.exp(lr * dt)
    ang1 = li * dt
    nr = mag1 * jnp.cos(ang1) - 1.0
    ni = mag1 * jnp.sin(ang1)
    den = lr * lr + li * li
    q_re = (nr * lr + ni * li) / den
    q_im = (ni * lr - nr * li) / den
    bre = bre_ref[...]
    bim = bim_ref[...]
    bbr_ref[...] = q_re * bre - q_im * bim
    bbi_ref[...] = q_re * bim + q_im * bre


def _s5_params(lam_re, lam_im, log_step, b_re, b_im, exponents):
    n_rows = exponents.shape[0]
    lr = lam_re.reshape(1, N_STATE)
    li = lam_im.reshape(1, N_STATE)
    ls = jnp.repeat(log_step, SSM_STATE).reshape(1, N_STATE)
    bre = b_re.transpose(2, 0, 1).reshape(SSM_GROUP, N_STATE)
    bim = b_im.transpose(2, 0, 1).reshape(SSM_GROUP, N_STATE)
    return pl.pallas_call(
        _s5_params_kernel,
        name="s5_params",
        out_shape=(jax.ShapeDtypeStruct((n_rows, N_STATE), F32),
                   jax.ShapeDtypeStruct((n_rows, N_STATE), F32),
                   jax.ShapeDtypeStruct((SSM_GROUP, N_STATE), F32),
                   jax.ShapeDtypeStruct((SSM_GROUP, N_STATE), F32)),
    )(lr, li, ls, exponents, bre, bim)


def _block_layout(re, im):
    r = re.shape[0]
    cat = jnp.concatenate([re.reshape(r, N_BLK, BLK_ST), im.reshape(r, N_BLK, BLK_ST)], axis=-1)
    return cat.reshape(r, N_BLK * BLK_W)


def _in_proj_kernel(x_ref, g_ref, w_ref, perm_ref, lng_ref, lnb_ref, sw_ref, sbt_ref, og_ref,
                    c0_ref, c1_ref, c2_ref,
                    zs_ref, yb_ref, c0_out, c1_out, c2_out, s_scr):
    c0_out[...] = c0_ref[...].astype(BF16)
    c1_out[...] = c1_ref[...].astype(BF16)
    c2_out[...] = c2_ref[...].astype(BF16)
    rows = x_ref.shape[0]
    h = _rms(x_ref[...], g_ref[...]).astype(BF16)
    perm = perm_ref[...]
    hp = jnp.concatenate(
        [_dot(perm, h[c * T_CHUNK:(c + 1) * T_CHUNK]).astype(BF16) for c in range(rows // T_CHUNK)],
        axis=0)
    zs_ref[...] = _dot(hp, w_ref[:, 0:D_SSM])
    u = jax.nn.gelu(_dot(h, w_ref[:, D_SSM:D_SSM + D_SGU]))
    gv = jax.nn.gelu(_dot(h, w_ref[:, D_SSM + D_SGU:]))
    mu = jnp.mean(gv, axis=-1, keepdims=True)
    xc = gv - mu
    r = lax.rsqrt(jnp.mean(xc * xc, axis=-1, keepdims=True) + EPS)
    v = ((xc * r) * lng_ref[...] + lnb_ref[...]).astype(BF16)
    ti = lax.broadcasted_iota(jnp.int32, (SGU_CHUNK, SGU_CHUNK), 0)
    si = lax.broadcasted_iota(jnp.int32, (SGU_CHUNK, SGU_CHUNK), 1)
    causal = si <= ti
    sbt = sbt_ref[...]
    for hd in range(SGU_HEADS):
        wm = jnp.where(causal, sw_ref[hd], jnp.zeros((), BF16))
        cols = slice(hd * SGU_HEAD_DIM, (hd + 1) * SGU_HEAD_DIM)
        bias = sbt[:, hd:hd + 1]
        for c in range(rows // SGU_CHUNK):
            rs = slice(c * SGU_CHUNK, (c + 1) * SGU_CHUNK)
            s_scr[rs, cols] = _dot(wm, v[rs, cols]) + bias
    yb = u * s_scr[...]
    yb_ref[...] = _rms(yb, og_ref[...]).astype(BF16)


def _in_proj(x2, g, w_in, perm, ln_g, ln_b, sgu_w, sgu_bt, og, to_cast):
    n = x2.shape[0]
    steps = n // ROWS_IN
    row = lambda i: (i, 0)
    return pl.pallas_call(
        _in_proj_kernel,
        name="in_proj",
        grid=(n // ROWS_IN,),
        in_specs=[
            pl.BlockSpec((ROWS_IN, D_MODEL), row),
            _resident((1, D_MODEL)),
            _resident(w_in.shape),
            _resident(perm.shape),
            _resident((1, D_SGU)),
            _resident((1, D_SGU)),
            _resident(sgu_w.shape),
            _resident(sgu_bt.shape),
            _resident((1, D_SGU)),
        ] + [_cast_block_spec(w, steps, row) for w in to_cast],
        out_specs=(pl.BlockSpec((ROWS_IN, D_SSM), row), pl.BlockSpec((ROWS_IN, D_SGU), row))
        + tuple(_cast_block_spec(w, steps, row) for w in to_cast),
        out_shape=(jax.ShapeDtypeStruct((n, D_SSM), F32), jax.ShapeDtypeStruct((n, D_SGU), BF16))
        + tuple(_cast_out_shape(w) for w in to_cast),
        scratch_shapes=[pltpu.VMEM((ROWS_IN, D_SGU), F32)],
        compiler_params=pltpu.CompilerParams(
            dimension_semantics=("arbitrary",), vmem_limit_bytes=VMEM_LIMIT),
    )(x2, g, w_in, perm, ln_g, ln_b, sgu_w, sgu_bt, og, *to_cast)


def _s5_kernel(zp_ref, bd_ref, cd_ref, a1_ref, akm_ref, p8_ref, d_ref,
               gluw_ref, glub_ref, og_ref, permt_ref, c0_ref,
               out_ref, c0_out, bu_ref, sbf_ref, carry_ref, y_ref):
    c0_out[...] = c0_ref[...].astype(BF16)

    @pl.when(pl.program_id(1) == 0)
    def _():
        carry_ref[...] = jnp.zeros_like(carry_ref)

    zp = zp_ref[...]
    zpb = zp.astype(BF16)
    seg = lax.broadcasted_iota(jnp.int32, (SUBLANES, SCAN_W), 0)
    zero = jnp.zeros((SUBLANES, SCAN_W), F32)
    for j in range(N_BLK):
        bu_ref[...] = _dot(zpb[:, j * BLK_CH:(j + 1) * BLK_CH], bd_ref[j])

        for c in range(S5_ROWS // T_CHUNK):
            base = c * T_CHUNK
            for c0 in range(0, BLK_ST, SCAN_W):
                re_cols = slice(c0, c0 + SCAN_W)
                im_cols = slice(BLK_ST + c0, BLK_ST + c0 + SCAN_W)
                tre = slice(j * BLK_W + c0, j * BLK_W + c0 + SCAN_W)
                tim = slice(j * BLK_W + BLK_ST + c0, j * BLK_W + BLK_ST + c0 + SCAN_W)
                ar = jnp.broadcast_to(a1_ref[:, tre], (SUBLANES, SCAN_W))
                ai = jnp.broadcast_to(a1_ref[:, tim], (SUBLANES, SCAN_W))

                def step(k, sr, si, base=base, re_cols=re_cols, im_cols=im_cols, ar=ar, ai=ai):
                    rows = pl.ds(pl.multiple_of(base + k * SUBLANES, SUBLANES), SUBLANES)
                    return (ar * sr - ai * si + bu_ref[rows, re_cols],
                            ar * si + ai * sr + bu_ref[rows, im_cols])

                xr, xi = lax.fori_loop(0, K_STEPS, lambda k, s, step=step: step(k, *s),
                                       (zero, zero), unroll=4)

                for lvl, m in enumerate((1, 2, 4)):
                    mr = akm_ref[lvl:lvl + 1, tre]
                    mi = akm_ref[lvl:lvl + 1, tim]
                    rr = pltpu.roll(xr, m, 0)
                    ri = pltpu.roll(xi, m, 0)
                    keep = seg >= m
                    xr, xi = (xr + jnp.where(keep, mr * rr - mi * ri, 0.0),
                              xi + jnp.where(keep, mr * ri + mi * rr, 0.0))
                cr = carry_ref[:, tre]
                ci = carry_ref[:, tim]
                pr = p8_ref[:, tre]
                pi = p8_ref[:, tim]
                xr, xi = xr + pr * cr - pi * ci, xi + pr * ci + pi * cr
                s0r = jnp.where(seg >= 1, pltpu.roll(xr, 1, 0), cr)
                s0i = jnp.where(seg >= 1, pltpu.roll(xi, 1, 0), ci)
                carry_ref[:, tre] = jnp.broadcast_to(xr[SUBLANES - 1:SUBLANES], (SUBLANES, SCAN_W))
                carry_ref[:, tim] = jnp.broadcast_to(xi[SUBLANES - 1:SUBLANES], (SUBLANES, SCAN_W))

                def emit(m, s, base=base, re_cols=re_cols, im_cols=im_cols, step=step):
                    r1, i1 = step(2 * m, *s)
                    r2, i2 = step(2 * m + 1, r1, i1)
                    rows = pl.ds(pl.multiple_of(base + m * 2 * SUBLANES, 2 * SUBLANES), 2 * SUBLANES)
                    sbf_ref[rows, re_cols] = jnp.concatenate([r1, r2], axis=0).astype(BF16)
                    sbf_ref[rows, im_cols] = jnp.concatenate([i1, i2], axis=0).astype(BF16)
                    return r2, i2

                lax.fori_loop(0, K_STEPS // 2, emit, (s0r, s0i), unroll=2)

        y_ref[:, j * BLK_CH:(j + 1) * BLK_CH] = _dot(sbf_ref[...], cd_ref[j])

    y = jax.nn.gelu(y_ref[...] + d_ref[...] * zp)
    gate = jax.nn.sigmoid(_dot(y.astype(BF16), gluw_ref[...]) + glub_ref[...])
    ya = _rms(y * gate, og_ref[...]).astype(BF16)
    permt = permt_ref[...]
    for c in range(S5_ROWS // T_CHUNK):
        rows = slice(c * T_CHUNK, (c + 1) * T_CHUNK)
        out_ref[rows, :] = _dot(permt, ya[rows]).astype(BF16)


def _s5(zs, bsz, seq, bd, cd, a1, akm, p8, dvec, glu_w, glu_b, og, permt, to_cast):
    steps = seq // S5_ROWS
    row = lambda b, c: (b * steps + c, 0)
    return pl.pallas_call(
        _s5_kernel,
        name="s5",
        grid=(bsz, steps),
        in_specs=[
            pl.BlockSpec((S5_ROWS, D_SSM), row),
            _resident(bd.shape), _resident(cd.shape),
            _resident(a1.shape), _resident(akm.shape), _resident(p8.shape),
            _resident(dvec.shape), _resident(glu_w.shape), _resident(glu_b.shape),
            _resident(og.shape), _resident(permt.shape),
            _cast_block_spec(to_cast, bsz * steps, row),
        ],
        out_specs=(pl.BlockSpec((S5_ROWS, D_SSM), row), _cast_block_spec(to_cast, bsz * steps, row)),
        out_shape=(jax.ShapeDtypeStruct((bsz * seq, D_SSM), BF16), _cast_out_shape(to_cast)),
        scratch_shapes=[
            pltpu.VMEM((S5_ROWS, BLK_W), F32),
            pltpu.VMEM((S5_ROWS, BLK_W), BF16),
            pltpu.VMEM((SUBLANES, N_BLK * BLK_W), F32),
            pltpu.VMEM((S5_ROWS, D_SSM), F32),
        ],
        compiler_params=pltpu.CompilerParams(
            dimension_semantics=("arbitrary", "arbitrary"), vmem_limit_bytes=VMEM_LIMIT),
    )(zs, bd, cd, a1, akm, p8, dvec, glu_w, glu_b, og, permt, to_cast)


def _out_proj_kernel(x_ref, ya_ref, yb_ref, w_ref, g_ref, c0_ref, x1_ref, h_ref, c0_out):
    c0_out[...] = c0_ref[...].astype(BF16)
    x1 = x_ref[...] + _dot(ya_ref[...], w_ref[0:D_SSM, :]) + _dot(yb_ref[...], w_ref[D_SSM:, :])
    x1_ref[...] = x1
    h_ref[...] = _rms(x1, g_ref[...]).astype(BF16)


def _out_proj(x2, ya, yb, w_out, g, to_cast):
    n = x2.shape[0]
    steps = n // ROWS_IN
    row = lambda i: (i, 0)
    return pl.pallas_call(
        _out_proj_kernel,
        name="out_proj",
        grid=(n // ROWS_IN,),
        in_specs=[
            pl.BlockSpec((ROWS_IN, D_MODEL), row),
            pl.BlockSpec((ROWS_IN, D_SSM), row),
            pl.BlockSpec((ROWS_IN, D_SGU), row),
            _resident(w_out.shape),
            _resident((1, D_MODEL)),
            _cast_block_spec(to_cast, steps, row),
        ],
        out_specs=(pl.BlockSpec((ROWS_IN, D_MODEL), row), pl.BlockSpec((ROWS_IN, D_MODEL), row),
                   _cast_block_spec(to_cast, steps, row)),
        out_shape=(jax.ShapeDtypeStruct((n, D_MODEL), F32), jax.ShapeDtypeStruct((n, D_MODEL), BF16),
                   _cast_out_shape(to_cast)),
        compiler_params=pltpu.CompilerParams(
            dimension_semantics=("arbitrary",), vmem_limit_bytes=VMEM_LIMIT),
    )(x2, ya, yb, w_out, g, to_cast)


def _ffn_kernel(h_ref, wg_ref, wu_ref, wo_ref, out_ref):
    @pl.when(pl.program_id(1) == 0)
    def _():
        out_ref[...] = jnp.zeros_like(out_ref)

    h = h_ref[...]
    act = jax.nn.silu(_dot(h, wg_ref[...])) * _dot(h, wu_ref[...])
    out_ref[...] += _dot(act.astype(BF16), wo_ref[...])


def _ffn(h, w_in, w_out):
    n = h.shape[0]
    nf = D_FFN // FFN_BLK
    return pl.pallas_call(
        _ffn_kernel,
        name="ffn",
        grid=(n // ROWS_FFN, nf),
        in_specs=[
            pl.BlockSpec((ROWS_FFN, D_MODEL), lambda i, f: (i, 0)),
            pl.BlockSpec((D_MODEL, FFN_BLK), lambda i, f: (0, f)),
            pl.BlockSpec((D_MODEL, FFN_BLK), lambda i, f: (0, f + nf)),
            pl.BlockSpec((FFN_BLK, D_MODEL), lambda i, f: (f, 0)),
        ],
        out_specs=pl.BlockSpec((ROWS_FFN, D_MODEL), lambda i, f: (i, 0)),
        out_shape=jax.ShapeDtypeStruct((n, D_MODEL), F32),
        compiler_params=pltpu.CompilerParams(
            dimension_semantics=("arbitrary", "arbitrary"), vmem_limit_bytes=VMEM_LIMIT),
    )(h, w_in, w_in, w_out)


def _ple_kernel(x_ref, f_ref, p_ref, wg_ref, bg_ref, wp_ref, g_ref, gf_ref, out_ref, *, final):
    x = x_ref[...] + f_ref[...]
    h = _rms(x, g_ref[...]).astype(BF16)
    gate = jax.nn.sigmoid(_dot(h, wg_ref[...]) + bg_ref[...])
    x3 = x + gate * _dot(p_ref[...].astype(BF16), wp_ref[...])
    out_ref[...] = _rms(x3, gf_ref[...]) if final else x3


def _ple(x1, ffn, p2, w_gate, b_gate, w_proj, g, gf, final):
    n = x1.shape[0]
    row = lambda i: (i, 0)
    return pl.pallas_call(
        functools.partial(_ple_kernel, final=final),
        name="ple",
        grid=(n // ROWS_IN,),
        in_specs=[
            pl.BlockSpec((ROWS_IN, D_MODEL), row),
            pl.BlockSpec((ROWS_IN, D_MODEL), row),
            pl.BlockSpec((ROWS_IN, PLE_DIM), row),
            _resident(w_gate.shape),
            _resident((1, D_MODEL)),
            _resident(w_proj.shape),
            _resident((1, D_MODEL)),
            _resident((1, D_MODEL)),
        ],
        out_specs=pl.BlockSpec((ROWS_IN, D_MODEL), row),
        out_shape=jax.ShapeDtypeStruct((n, D_MODEL), F32),
        compiler_params=pltpu.CompilerParams(
            dimension_semantics=("arbitrary",), vmem_limit_bytes=VMEM_LIMIT),
    )(x1, ffn, p2, w_gate, b_gate, w_proj, g, gf)


def _segment_permutation():
    r = np.arange(T_CHUNK)
    src = (r % SUBLANES) * K_STEPS + r // SUBLANES
    perm = np.zeros((T_CHUNK, T_CHUNK), np.float32)
    perm[r, src] = 1.0
    return perm


def _scan_exponents():
    n = [1.0]
    n += [float(K_STEPS * m) for m in (1, 2, 4)]
    n += [float(K_STEPS * (i + 1)) for i in range(SUBLANES)]
    return np.asarray(n, np.float32).reshape(-1, 1)


def _block_diag_b(t):
    t = jnp.broadcast_to(t.reshape(SSM_GROUP, N_BLK, 1, BLK_ST),
                         (SSM_GROUP, N_BLK, SSM_GROUP, BLK_ST)).transpose(1, 2, 0, 3)
    same = (np.arange(SSM_GROUP)[:, None, None] == (np.arange(BLK_ST) // SSM_STATE)[None, None, :])
    return jnp.where(same, t, 0.0).reshape(N_BLK, BLK_CH, BLK_ST)


def _block_diag_c(t):
    t = t.reshape(N_BLK, SSM_GROUP, SSM_GROUP, SSM_STATE).transpose(0, 1, 3, 2)
    t = jnp.broadcast_to(t.reshape(N_BLK, BLK_ST, 1, SSM_GROUP), (N_BLK, BLK_ST, SSM_GROUP, SSM_GROUP))
    same = ((np.arange(BLK_ST) // SSM_STATE)[:, None, None] == np.arange(SSM_GROUP)[None, :, None])
    return jnp.where(same, t, 0.0).reshape(N_BLK, BLK_ST, BLK_CH)


def kernel(x, p, norm_mix_g, w_in, ssm_lambda_re, ssm_lambda_im, ssm_log_step, ssm_b_re, ssm_b_im, ssm_c_re, ssm_c_im, ssm_d, ssm_glu_w, ssm_glu_b, sgu_ln_g, sgu_ln_b, sgu_w, sgu_b, out_norm_ssm_g, out_norm_sgu_g, w_out, norm_ffn_g, w_ffn_in, w_ffn_out, norm_ple_g, w_ple_gate, b_ple_gate, w_ple_proj, final_norm_g):
    bsz, seq, _ = x.shape
    depth = w_in.shape[0]
    perm_np = _segment_permutation()
    perm = jnp.asarray(perm_np, BF16)
    permt = jnp.asarray(perm_np.T, BF16)
    exponents = jnp.asarray(_scan_exponents())
    x2 = x.reshape(bsz * seq, D_MODEL)
    vec = lambda a: a.reshape(1, -1)
    for i in range(depth):
        pw_re, pw_im, bb_re, bb_im = _s5_params(
            ssm_lambda_re[i], ssm_lambda_im[i], ssm_log_step[i], ssm_b_re[i], ssm_b_im[i], exponents)
        tables = _block_layout(pw_re, pw_im)
        a1, akm, p8 = tables[0:1], tables[1:4], tables[4:12]
        bd = jnp.concatenate([_block_diag_b(bb_re), _block_diag_b(bb_im)], axis=-1).astype(BF16)
        cd = jnp.concatenate([_block_diag_c(ssm_c_re[i]), _block_diag_c(-ssm_c_im[i])],
                             axis=1).astype(BF16)

        zs, yb, w_out_bf, w_gate_bf, glu_bf = _in_proj(
            x2, vec(norm_mix_g[i]), w_in[i].astype(BF16), perm, vec(sgu_ln_g[i]), vec(sgu_ln_b[i]),
            sgu_w[i].astype(BF16), sgu_b[i].T, vec(out_norm_sgu_g[i]),
            to_cast=(w_out[i], w_ple_gate[i], ssm_glu_w[i]))
        ya, w_ffn_in_bf = _s5(zs, bsz, seq, bd, cd, a1, akm, p8, vec(ssm_d[i]), glu_bf,
                              vec(ssm_glu_b[i]), vec(out_norm_ssm_g[i]), permt, to_cast=w_ffn_in[i])
        x1, h, w_ffn_out_bf = _out_proj(x2, ya, yb, w_out_bf, vec(norm_ffn_g[i]), to_cast=w_ffn_out[i])
        ffn = _ffn(h, w_ffn_in_bf, w_ffn_out_bf)
        x2 = _ple(x1, ffn, p[i].reshape(bsz * seq, PLE_DIM), w_gate_bf, vec(b_ple_gate[i]),
                  w_ple_proj[i].astype(BF16), vec(norm_ple_g[i]), vec(final_norm_g),
                  final=(i == depth - 1))
    return x2.reshape(bsz, seq, D_MODEL)
```

```python
import functools

import numpy as np
import jax
import jax.numpy as jnp
from jax import lax
from jax.experimental import pallas as pl
from jax.experimental.pallas import tpu as pltpu

D_MODEL = 2048
D_SSM = 1024
D_SGU = 1024
SSM_GROUP = 16
SSM_GROUPS = 64
SSM_STATE = 64
SGU_CHUNK = 128
SGU_HEADS = 8
SGU_HEAD_DIM = D_SGU // SGU_HEADS
D_FFN = 5632
PLE_DIM = 256
EPS = 1e-6
LAMBDA_RE_MAX = -1e-4

SUBLANES = 8
N_STATE = SSM_GROUPS * SSM_STATE
N_BLK = 4
BLK_CH = D_SSM // N_BLK
BLK_ST = N_STATE // N_BLK
BLK_W = 2 * BLK_ST
T_CHUNK = 256
K_STEPS = T_CHUNK // SUBLANES
S5_ROWS = 512
SCAN_W = 512
N_GRP = BLK_ST // SCAN_W
N_PIECE = 2 * N_GRP

ROWS_IN = 512
ROWS_FFN = 1024
FFN_BLK = 512
VMEM_LIMIT = 56 * 1024 * 1024

F32 = jnp.float32
BF16 = jnp.bfloat16


def _rms(x, g):
    r = lax.rsqrt(jnp.mean(x * x, axis=-1, keepdims=True) + EPS)
    return (x * r) * g


def _dot(a, b):
    return jnp.dot(a, b, preferred_element_type=F32)


def _resident(shape):
    nd = len(shape)
    return pl.BlockSpec(shape, lambda *_: (0,) * nd, pipeline_mode=pl.Buffered(1))


def _cast_block_spec(w, steps, index):
    return pl.BlockSpec((w.shape[0] // steps, w.shape[1]), index)


def _cast_out_shape(w):
    return jax.ShapeDtypeStruct(w.shape, BF16)


def _s5_params_kernel(lr_ref, li_ref, ls_ref, n_ref, bre_ref, bim_ref,
                      pwr_ref, pwi_ref, bbr_ref, bbi_ref):
    lr = jnp.minimum(lr_ref[...], LAMBDA_RE_MAX)
    li = li_ref[...]
    dt = jnp.exp(ls_ref[...])
    n = n_ref[...]
    mag = jnp.exp(n * (lr * dt))
    ang = n * (li * dt)
    pwr_ref[...] = mag * jnp.cos(ang)
    pwi_ref[...] = mag * jnp.sin(ang)
    mag1 = jnp.exp(lr * dt)
    ang1 = li * dt
    nr = mag1 * jnp.cos(ang1) - 1.0
    ni = mag1 * jnp.sin(ang1)
    den = lr * lr + li * li
    q_re = (nr * lr + ni * li) / den
    q_im = (ni * lr - nr * li) / den
    bre = bre_ref[...]
    bim = bim_ref[...]
    bbr_ref[...] = q_re * bre - q_im * bim
    bbi_ref[...] = q_re * bim + q_im * bre


def _s5_params(lam_re, lam_im, log_step, b_re, b_im, exponents):
    n_rows = exponents.shape[0]
    lr = lam_re.reshape(1, N_STATE)
    li = lam_im.reshape(1, N_STATE)
    ls = jnp.repeat(log_step, SSM_STATE).reshape(1, N_STATE)
    bre = b_re.transpose(2, 0, 1).reshape(SSM_GROUP, N_STATE)
    bim = b_im.transpose(2, 0, 1).reshape(SSM_GROUP, N_STATE)
    return pl.pallas_call(
        _s5_params_kernel,
        name="s5_params",
        out_shape=(jax.ShapeDtypeStruct((n_rows, N_STATE), F32),
                   jax.ShapeDtypeStruct((n_rows, N_STATE), F32),
                   jax.ShapeDtypeStruct((SSM_GROUP, N_STATE), F32),
                   jax.ShapeDtypeStruct((SSM_GROUP, N_STATE), F32)),
    )(lr, li, ls, exponents, bre, bim)


def _table_layout(re, im):
    r = re.shape[0]
    cat = jnp.concatenate([re.reshape(r, N_BLK, N_GRP, SCAN_W), im.reshape(r, N_BLK, N_GRP, SCAN_W)],
                          axis=-1)
    return cat.transpose(1, 2, 0, 3)


def _pieces(re, im, axis):
    parts = []
    for g in range(N_GRP):
        idx = [slice(None)] * re.ndim
        idx[axis] = slice(g * SCAN_W, (g + 1) * SCAN_W)
        parts += [re[tuple(idx)], im[tuple(idx)]]
    return jnp.stack(parts, axis=1)


def _in_proj_kernel(x_ref, g_ref, w_ref, perm_ref, lng_ref, lnb_ref, sw_ref, sbt_ref, og_ref,
                    c0_ref, c1_ref, c2_ref,
                    zs_ref, yb_ref, c0_out, c1_out, c2_out, s_scr):
    c0_out[...] = c0_ref[...].astype(BF16)
    c1_out[...] = c1_ref[...].astype(BF16)
    c2_out[...] = c2_ref[...].astype(BF16)
    rows = x_ref.shape[0]
    h = _rms(x_ref[...], g_ref[...]).astype(BF16)
    perm = perm_ref[...]
    hp = jnp.concatenate(
        [_dot(perm, h[c * T_CHUNK:(c + 1) * T_CHUNK]).astype(BF16) for c in range(rows // T_CHUNK)],
        axis=0)
    zs_ref[...] = _dot(hp, w_ref[:, 0:D_SSM])
    u = jax.nn.gelu(_dot(h, w_ref[:, D_SSM:D_SSM + D_SGU]))
    gv = jax.nn.gelu(_dot(h, w_ref[:, D_SSM + D_SGU:]))
    mu = jnp.mean(gv, axis=-1, keepdims=True)
    xc = gv - mu
    r = lax.rsqrt(jnp.mean(xc * xc, axis=-1, keepdims=True) + EPS)
    v = ((xc * r) * lng_ref[...] + lnb_ref[...]).astype(BF16)
    ti = lax.broadcasted_iota(jnp.int32, (SGU_CHUNK, SGU_CHUNK), 0)
    si = lax.broadcasted_iota(jnp.int32, (SGU_CHUNK, SGU_CHUNK), 1)
    causal = si <= ti
    sbt = sbt_ref[...]
    for hd in range(SGU_HEADS):
        wm = jnp.where(causal, sw_ref[hd], jnp.zeros((), BF16))
        cols = slice(hd * SGU_HEAD_DIM, (hd + 1) * SGU_HEAD_DIM)
        bias = sbt[:, hd:hd + 1]
        for c in range(rows // SGU_CHUNK):
            rs = slice(c * SGU_CHUNK, (c + 1) * SGU_CHUNK)
            s_scr[rs, cols] = _dot(wm, v[rs, cols]) + bias
    yb = u * s_scr[...]
    yb_ref[...] = _rms(yb, og_ref[...]).astype(BF16)


def _in_proj(x2, g, w_in, perm, ln_g, ln_b, sgu_w, sgu_bt, og, to_cast):
    n = x2.shape[0]
    steps = n // ROWS_IN
    row = lambda i: (i, 0)
    return pl.pallas_call(
        _in_proj_kernel,
        name="in_proj",
        grid=(n // ROWS_IN,),
        in_specs=[
            pl.BlockSpec((ROWS_IN, D_MODEL), row),
            _resident((1, D_MODEL)),
            _resident(w_in.shape),
            _resident(perm.shape),
            _resident((1, D_SGU)),
            _resident((1, D_SGU)),
            _resident(sgu_w.shape),
            _resident(sgu_bt.shape),
            _resident((1, D_SGU)),
        ] + [_cast_block_spec(w, steps, row) for w in to_cast],
        out_specs=(pl.BlockSpec((ROWS_IN, D_SSM), row), pl.BlockSpec((ROWS_IN, D_SGU), row))
        + tuple(_cast_block_spec(w, steps, row) for w in to_cast),
        out_shape=(jax.ShapeDtypeStruct((n, D_SSM), F32), jax.ShapeDtypeStruct((n, D_SGU), BF16))
        + tuple(_cast_out_shape(w) for w in to_cast),
        scratch_shapes=[pltpu.VMEM((ROWS_IN, D_SGU), F32)],
        compiler_params=pltpu.CompilerParams(
            dimension_semantics=("arbitrary",), vmem_limit_bytes=VMEM_LIMIT),
    )(x2, g, w_in, perm, ln_g, ln_b, sgu_w, sgu_bt, og, *to_cast)


def _s5_kernel(zp_ref, bd_ref, cd_ref, tab_ref, d_ref, gluw_ref, glub_ref, og_ref, permt_ref, c0_ref,
               out_ref, c0_out, bu_a, bu_b, sb_a, sb_b, carry_ref, y_ref):
    c0_out[...] = c0_ref[...].astype(BF16)

    @pl.when(pl.program_id(1) == 0)
    def _():
        carry_ref[...] = jnp.zeros_like(carry_ref)

    zp = zp_ref[...]
    zpb = zp.astype(BF16)
    bu = (bu_a, bu_b)
    sb = (sb_a, sb_b)
    seg = lax.broadcasted_iota(jnp.int32, (SUBLANES, SCAN_W), 0)
    zero = jnp.zeros((SUBLANES, SCAN_W), F32)
    blk = lambda j: slice(j * BLK_CH, (j + 1) * BLK_CH)

    for p in range(N_PIECE):
        bu[0][p] = _dot(zpb[:, blk(0)], bd_ref[0, p])
    for j in range(N_BLK):
        y_ref[j] = d_ref[:, blk(j)] * zp[:, blk(j)]

    for j in range(N_BLK):
        cur, oth = j % 2, (j + 1) % 2

        def body(it, _, j=j, cur=cur, oth=oth):
            if j + 1 < N_BLK:
                bu[oth][it] = _dot(zpb[:, blk(j + 1)], bd_ref[j + 1, it])
            if j >= 1:
                y_ref[j - 1] += _dot(sb[oth][it], cd_ref[j - 1, it])

            c = lax.shift_right_logical(it, 1)
            g = lax.bitwise_and(it, 1)
            base = pl.multiple_of(c * T_CHUNK, T_CHUNK)
            re_p, im_p = 2 * g, 2 * g + 1
            tre, tim = slice(0, SCAN_W), slice(SCAN_W, 2 * SCAN_W)
            ar = jnp.broadcast_to(tab_ref[j, g, 0:1, tre], (SUBLANES, SCAN_W))
            ai = jnp.broadcast_to(tab_ref[j, g, 0:1, tim], (SUBLANES, SCAN_W))

            def step(k, sr, si):
                rows = pl.ds(base + k * SUBLANES, SUBLANES)
                return (ar * sr - ai * si + bu[cur][re_p, rows, :],
                        ar * si + ai * sr + bu[cur][im_p, rows, :])

            xr, xi = zero, zero
            for k in range(K_STEPS):
                xr, xi = step(k, xr, xi)

            for lvl, m in enumerate((1, 2, 4)):
                mr = tab_ref[j, g, 1 + lvl:2 + lvl, tre]
                mi = tab_ref[j, g, 1 + lvl:2 + lvl, tim]
                rr = pltpu.roll(xr, m, 0)
                ri = pltpu.roll(xi, m, 0)
                keep = seg >= m
                xr, xi = (xr + jnp.where(keep, mr * rr - mi * ri, 0.0),
                          xi + jnp.where(keep, mr * ri + mi * rr, 0.0))
            cr = carry_ref[j, g, :, tre]
            ci = carry_ref[j, g, :, tim]
            pr = tab_ref[j, g, 4:4 + SUBLANES, tre]
            pi = tab_ref[j, g, 4:4 + SUBLANES, tim]
            xr, xi = xr + pr * cr - pi * ci, xi + pr * ci + pi * cr
            sr = jnp.where(seg >= 1, pltpu.roll(xr, 1, 0), cr)
            si = jnp.where(seg >= 1, pltpu.roll(xi, 1, 0), ci)
            carry_ref[j, g, :, tre] = jnp.broadcast_to(xr[SUBLANES - 1:SUBLANES], (SUBLANES, SCAN_W))
            carry_ref[j, g, :, tim] = jnp.broadcast_to(xi[SUBLANES - 1:SUBLANES], (SUBLANES, SCAN_W))

            for m in range(K_STEPS // 2):
                r1, i1 = step(2 * m, sr, si)
                sr, si = step(2 * m + 1, r1, i1)
                rows = pl.ds(base + m * 2 * SUBLANES, 2 * SUBLANES)
                sb[cur][re_p, rows, :] = jnp.concatenate([r1, sr], axis=0).astype(BF16)
                sb[cur][im_p, rows, :] = jnp.concatenate([i1, si], axis=0).astype(BF16)
            return 0

        lax.fori_loop(0, N_PIECE, body, 0)

    last = N_BLK - 1
    for p in range(N_PIECE):
        y_ref[last] += _dot(sb[last % 2][p], cd_ref[last, p])

    y = jax.nn.gelu(jnp.concatenate([y_ref[j] for j in range(N_BLK)], axis=1))
    gate = jax.nn.sigmoid(_dot(y.astype(BF16), gluw_ref[...]) + glub_ref[...])
    ya = _rms(y * gate, og_ref[...]).astype(BF16)
    permt = permt_ref[...]
    for c in range(S5_ROWS // T_CHUNK):
        rows = slice(c * T_CHUNK, (c + 1) * T_CHUNK)
        out_ref[rows, :] = _dot(permt, ya[rows]).astype(BF16)


def _s5(zs, bsz, seq, bd, cd, tab, dvec, glu_w, glu_b, og, permt, to_cast):
    steps = seq // S5_ROWS
    row = lambda b, c: (b * steps + c, 0)
    return pl.pallas_call(
        _s5_kernel,
        name="s5",
        grid=(bsz, steps),
        in_specs=[
            pl.BlockSpec((S5_ROWS, D_SSM), row),
            _resident(bd.shape), _resident(cd.shape), _resident(tab.shape),
            _resident(dvec.shape), _resident(glu_w.shape), _resident(glu_b.shape),
            _resident(og.shape), _resident(permt.shape),
            _cast_block_spec(to_cast, bsz * steps, row),
        ],
        out_specs=(pl.BlockSpec((S5_ROWS, D_SSM), row), _cast_block_spec(to_cast, bsz * steps, row)),
        out_shape=(jax.ShapeDtypeStruct((bsz * seq, D_SSM), BF16), _cast_out_shape(to_cast)),
        scratch_shapes=[
            pltpu.VMEM((N_PIECE, S5_ROWS, SCAN_W), F32),
            pltpu.VMEM((N_PIECE, S5_ROWS, SCAN_W), F32),
            pltpu.VMEM((N_PIECE, S5_ROWS, SCAN_W), BF16),
            pltpu.VMEM((N_PIECE, S5_ROWS, SCAN_W), BF16),
            pltpu.VMEM((N_BLK, N_GRP, SUBLANES, 2 * SCAN_W), F32),
            pltpu.VMEM((N_BLK, S5_ROWS, BLK_CH), F32),
        ],
        compiler_params=pltpu.CompilerParams(
            dimension_semantics=("arbitrary", "arbitrary"), vmem_limit_bytes=VMEM_LIMIT),
    )(zs, bd, cd, tab, dvec, glu_w, glu_b, og, permt, to_cast)


def _out_proj_kernel(x_ref, ya_ref, yb_ref, w_ref, g_ref, c0_ref, x1_ref, h_ref, c0_out):
    c0_out[...] = c0_ref[...].astype(BF16)
    x1 = x_ref[...] + _dot(ya_ref[...], w_ref[0:D_SSM, :]) + _dot(yb_ref[...], w_ref[D_SSM:, :])
    x1_ref[...] = x1
    h_ref[...] = _rms(x1, g_ref[...]).astype(BF16)


def _out_proj(x2, ya, yb, w_out, g, to_cast):
    n = x2.shape[0]
    steps = n // ROWS_IN
    row = lambda i: (i, 0)
    return pl.pallas_call(
        _out_proj_kernel,
        name="out_proj",
        grid=(n // ROWS_IN,),
        in_specs=[
            pl.BlockSpec((ROWS_IN, D_MODEL), row),
            pl.BlockSpec((ROWS_IN, D_SSM), row),
            pl.BlockSpec((ROWS_IN, D_SGU), row),
            _resident(w_out.shape),
            _resident((1, D_MODEL)),
            _cast_block_spec(to_cast, steps, row),
        ],
        out_specs=(pl.BlockSpec((ROWS_IN, D_MODEL), row), pl.BlockSpec((ROWS_IN, D_MODEL), row),
                   _cast_block_spec(to_cast, steps, row)),
        out_shape=(jax.ShapeDtypeStruct((n, D_MODEL), F32), jax.ShapeDtypeStruct((n, D_MODEL), BF16),
                   _cast_out_shape(to_cast)),
        compiler_params=pltpu.CompilerParams(
            dimension_semantics=("arbitrary",), vmem_limit_bytes=VMEM_LIMIT),
    )(x2, ya, yb, w_out, g, to_cast)


def _ffn_kernel(h_ref, wg_ref, wu_ref, wo_ref, out_ref):
    @pl.when(pl.program_id(1) == 0)
    def _():
        out_ref[...] = jnp.zeros_like(out_ref)

    h = h_ref[...]
    act = jax.nn.silu(_dot(h, wg_ref[...])) * _dot(h, wu_ref[...])
    out_ref[...] += _dot(act.astype(BF16), wo_ref[...])


def _ffn(h, w_in, w_out):
    n = h.shape[0]
    nf = D_FFN // FFN_BLK
    return pl.pallas_call(
        _ffn_kernel,
        name="ffn",
        grid=(n // ROWS_FFN, nf),
        in_specs=[
            pl.BlockSpec((ROWS_FFN, D_MODEL), lambda i, f: (i, 0)),
            pl.BlockSpec((D_MODEL, FFN_BLK), lambda i, f: (0, f)),
            pl.BlockSpec((D_MODEL, FFN_BLK), lambda i, f: (0, f + nf)),
            pl.BlockSpec((FFN_BLK, D_MODEL), lambda i, f: (f, 0)),
        ],
        out_specs=pl.BlockSpec((ROWS_FFN, D_MODEL), lambda i, f: (i, 0)),
        out_shape=jax.ShapeDtypeStruct((n, D_MODEL), F32),
        compiler_params=pltpu.CompilerParams(
            dimension_semantics=("arbitrary", "arbitrary"), vmem_limit_bytes=VMEM_LIMIT),
    )(h, w_in, w_in, w_out)


def _ple_kernel(x_ref, f_ref, p_ref, wg_ref, bg_ref, wp_ref, g_ref, gf_ref, out_ref, *, final):
    x = x_ref[...] + f_ref[...]
    h = _rms(x, g_ref[...]).astype(BF16)
    gate = jax.nn.sigmoid(_dot(h, wg_ref[...]) + bg_ref[...])
    x3 = x + gate * _dot(p_ref[...].astype(BF16), wp_ref[...])
    out_ref[...] = _rms(x3, gf_ref[...]) if final else x3


def _ple(x1, ffn, p2, w_gate, b_gate, w_proj, g, gf, final):
    n = x1.shape[0]
    row = lambda i: (i, 0)
    return pl.pallas_call(
        functools.partial(_ple_kernel, final=final),
        name="ple",
        grid=(n // ROWS_IN,),
        in_specs=[
            pl.BlockSpec((ROWS_IN, D_MODEL), row),
            pl.BlockSpec((ROWS_IN, D_MODEL), row),
            pl.BlockSpec((ROWS_IN, PLE_DIM), row),
            _resident(w_gate.shape),
            _resident((1, D_MODEL)),
            _resident(w_proj.shape),
            _resident((1, D_MODEL)),
            _resident((1, D_MODEL)),
        ],
        out_specs=pl.BlockSpec((ROWS_IN, D_MODEL), row),
        out_shape=jax.ShapeDtypeStruct((n, D_MODEL), F32),
        compiler_params=pltpu.CompilerParams(
            dimension_semantics=("arbitrary",), vmem_limit_bytes=VMEM_LIMIT),
    )(x1, ffn, p2, w_gate, b_gate, w_proj, g, gf)


def _segment_permutation():
    r = np.arange(T_CHUNK)
    src = (r % SUBLANES) * K_STEPS + r // SUBLANES
    perm = np.zeros((T_CHUNK, T_CHUNK), np.float32)
    perm[r, src] = 1.0
    return perm


def _scan_exponents():
    n = [1.0]
    n += [float(K_STEPS * m) for m in (1, 2, 4)]
    n += [float(K_STEPS * (i + 1)) for i in range(SUBLANES)]
    return np.asarray(n, np.float32).reshape(-1, 1)


def _block_diag_b(t):
    t = jnp.broadcast_to(t.reshape(SSM_GROUP, N_BLK, 1, BLK_ST),
                         (SSM_GROUP, N_BLK, SSM_GROUP, BLK_ST)).transpose(1, 2, 0, 3)
    same = (np.arange(SSM_GROUP)[:, None, None] == (np.arange(BLK_ST) // SSM_STATE)[None, None, :])
    return jnp.where(same, t, 0.0).reshape(N_BLK, BLK_CH, BLK_ST)


def _block_diag_c(t):
    t = t.reshape(N_BLK, SSM_GROUP, SSM_GROUP, SSM_STATE).transpose(0, 1, 3, 2)
    t = jnp.broadcast_to(t.reshape(N_BLK, BLK_ST, 1, SSM_GROUP), (N_BLK, BLK_ST, SSM_GROUP, SSM_GROUP))
    same = ((np.arange(BLK_ST) // SSM_STATE)[:, None, None] == np.arange(SSM_GROUP)[None, :, None])
    return jnp.where(same, t, 0.0).reshape(N_BLK, BLK_ST, BLK_CH)


def kernel(x, p, norm_mix_g, w_in, ssm_lambda_re, ssm_lambda_im, ssm_log_step, ssm_b_re, ssm_b_im, ssm_c_re, ssm_c_im, ssm_d, ssm_glu_w, ssm_glu_b, sgu_ln_g, sgu_ln_b, sgu_w, sgu_b, out_norm_ssm_g, out_norm_sgu_g, w_out, norm_ffn_g, w_ffn_in, w_ffn_out, norm_ple_g, w_ple_gate, b_ple_gate, w_ple_proj, final_norm_g):
    bsz, seq, _ = x.shape
    depth = w_in.shape[0]
    perm_np = _segment_permutation()
    perm = jnp.asarray(perm_np, BF16)
    permt = jnp.asarray(perm_np.T, BF16)
    exponents = jnp.asarray(_scan_exponents())
    x2 = x.reshape(bsz * seq, D_MODEL)
    vec = lambda a: a.reshape(1, -1)
    for i in range(depth):
        pw_re, pw_im, bb_re, bb_im = _s5_params(
            ssm_lambda_re[i], ssm_lambda_im[i], ssm_log_step[i], ssm_b_re[i], ssm_b_im[i], exponents)
        tab = _table_layout(pw_re, pw_im)
        bd = _pieces(_block_diag_b(bb_re), _block_diag_b(bb_im), axis=2).astype(BF16)
        cd = _pieces(_block_diag_c(ssm_c_re[i]), _block_diag_c(-ssm_c_im[i]), axis=1).astype(BF16)

        zs, yb, w_out_bf, w_gate_bf, glu_bf = _in_proj(
            x2, vec(norm_mix_g[i]), w_in[i].astype(BF16), perm, vec(sgu_ln_g[i]), vec(sgu_ln_b[i]),
            sgu_w[i].astype(BF16), sgu_b[i].T, vec(out_norm_sgu_g[i]),
            to_cast=(w_out[i], w_ple_gate[i], ssm_glu_w[i]))
        ya, w_ffn_in_bf = _s5(zs, bsz, seq, bd, cd, tab, vec(ssm_d[i]), glu_bf,
                              vec(ssm_glu_b[i]), vec(out_norm_ssm_g[i]), permt, to_cast=w_ffn_in[i])
        x1, h, w_ffn_out_bf = _out_proj(x2, ya, yb, w_out_bf, vec(norm_ffn_g[i]), to_cast=w_ffn_out[i])
        ffn = _ffn(h, w_ffn_in_bf, w_ffn_out_bf)
        x2 = _ple(x1, ffn, p[i].reshape(bsz * seq, PLE_DIM), w_gate_bf, vec(b_ple_gate[i]),
                  w_ple_proj[i].astype(BF16), vec(norm_ple_g[i]), vec(final_norm_g),
                  final=(i == depth - 1))
    return x2.reshape(bsz, seq, D_MODEL)
```

```python
import functools

import numpy as np
import jax
import jax.numpy as jnp
from jax import lax
from jax.experimental import pallas as pl
from jax.experimental.pallas import tpu as pltpu

D_MODEL = 2048
D_SSM = 1024
D_SGU = 1024
SSM_GROUP = 16
SSM_GROUPS = 64
SSM_STATE = 64
SGU_CHUNK = 128
SGU_HEADS = 8
SGU_HEAD_DIM = D_SGU // SGU_HEADS
D_FFN = 5632
PLE_DIM = 256
EPS = 1e-6
LAMBDA_RE_MAX = -1e-4

SUBLANES = 8
N_STATE = SSM_GROUPS * SSM_STATE
N_BLK = 4
BLK_CH = D_SSM // N_BLK
BLK_ST = N_STATE // N_BLK
BLK_W = 2 * BLK_ST
T_CHUNK = 256
K_STEPS = T_CHUNK // SUBLANES
S5_ROWS = 512
SCAN_W = 512
N_GRP = BLK_ST // SCAN_W
N_PIECE = 2 * N_GRP

ROWS_IN = 512
ROWS_FFN = 1024
FFN_BLK = 512
VMEM_LIMIT = 56 * 1024 * 1024

F32 = jnp.float32
BF16 = jnp.bfloat16


def _rms(x, g):
    r = lax.rsqrt(jnp.mean(x * x, axis=-1, keepdims=True) + EPS)
    return (x * r) * g


def _dot(a, b):
    return jnp.dot(a, b, preferred_element_type=F32)


def _resident(shape):
    nd = len(shape)
    return pl.BlockSpec(shape, lambda *_: (0,) * nd, pipeline_mode=pl.Buffered(1))


def _cast_block_spec(w, steps, index):
    return pl.BlockSpec((w.shape[0] // steps, w.shape[1]), index)


def _cast_out_shape(w):
    return jax.ShapeDtypeStruct(w.shape, BF16)


def _s5_params_kernel(lr_ref, li_ref, ls_ref, n_ref, bre_ref, bim_ref,
                      pwr_ref, pwi_ref, bbr_ref, bbi_ref):
    lr = jnp.minimum(lr_ref[...], LAMBDA_RE_MAX)
    li = li_ref[...]
    dt = jnp.exp(ls_ref[...])
    n = n_ref[...]
    mag = jnp.exp(n * (lr * dt))
    ang = n * (li * dt)
    pwr_ref[...] = mag * jnp.cos(ang)
    pwi_ref[...] = mag * jnp.sin(ang)
    mag1 = jnp.exp(lr * dt)
    ang1 = li * dt
    nr = mag1 * jnp.cos(ang1) - 1.0
    ni = mag1 * jnp.sin(ang1)
    den = lr * lr + li * li
    q_re = (nr * lr + ni * li) / den
    q_im = (ni * lr - nr * li) / den
    bre = bre_ref[...]
    bim = bim_ref[...]
    bbr_ref[...] = q_re * bre - q_im * bim
    bbi_ref[...] = q_re * bim + q_im * bre


def _s5_params(lam_re, lam_im, log_step, b_re, b_im, exponents):
    n_rows = exponents.shape[0]
    lr = lam_re.reshape(1, N_STATE)
    li = lam_im.reshape(1, N_STATE)
    ls = jnp.repeat(log_step, SSM_STATE).reshape(1, N_STATE)
    bre = b_re.transpose(2, 0, 1).reshape(SSM_GROUP, N_STATE)
    bim = b_im.transpose(2, 0, 1).reshape(SSM_GROUP, N_STATE)
    return pl.pallas_call(
        _s5_params_kernel,
        name="s5_params",
        out_shape=(jax.ShapeDtypeStruct((n_rows, N_STATE), F32),
                   jax.ShapeDtypeStruct((n_rows, N_STATE), F32),
                   jax.ShapeDtypeStruct((SSM_GROUP, N_STATE), F32),
                   jax.ShapeDtypeStruct((SSM_GROUP, N_STATE), F32)),
    )(lr, li, ls, exponents, bre, bim)


def _table_layout(re, im):
    r = re.shape[0]
    cat = jnp.concatenate([re.reshape(r, N_BLK, N_GRP, SCAN_W), im.reshape(r, N_BLK, N_GRP, SCAN_W)],
                          axis=-1)
    return cat.transpose(1, 2, 0, 3)


def _pieces(re, im, axis):
    parts = []
    for g in range(N_GRP):
        idx = [slice(None)] * re.ndim
        idx[axis] = slice(g * SCAN_W, (g + 1) * SCAN_W)
        parts += [re[tuple(idx)], im[tuple(idx)]]
    return jnp.stack(parts, axis=1)


def _in_proj_kernel(x_ref, g_ref, w_ref, perm_ref, lng_ref, lnb_ref, sw_ref, sbt_ref, og_ref,
                    c0_ref, c1_ref, c2_ref, c3_ref,
                    zs_ref, yb_ref, c0_out, c1_out, c2_out, c3_out, s_scr):
    c0_out[...] = c0_ref[...].astype(BF16)
    c1_out[...] = c1_ref[...].astype(BF16)
    c2_out[...] = c2_ref[...].astype(BF16)
    c3_out[...] = c3_ref[...].astype(BF16)
    rows = x_ref.shape[0]
    h = _rms(x_ref[...], g_ref[...]).astype(BF16)
    perm = perm_ref[...]
    hp = jnp.concatenate(
        [_dot(perm, h[c * T_CHUNK:(c + 1) * T_CHUNK]).astype(BF16) for c in range(rows // T_CHUNK)],
        axis=0)
    zs_ref[...] = _dot(hp, w_ref[:, 0:D_SSM])
    u = jax.nn.gelu(_dot(h, w_ref[:, D_SSM:D_SSM + D_SGU]))
    gv = jax.nn.gelu(_dot(h, w_ref[:, D_SSM + D_SGU:]))
    mu = jnp.mean(gv, axis=-1, keepdims=True)
    xc = gv - mu
    r = lax.rsqrt(jnp.mean(xc * xc, axis=-1, keepdims=True) + EPS)
    v = ((xc * r) * lng_ref[...] + lnb_ref[...]).astype(BF16)
    ti = lax.broadcasted_iota(jnp.int32, (SGU_CHUNK, SGU_CHUNK), 0)
    si = lax.broadcasted_iota(jnp.int32, (SGU_CHUNK, SGU_CHUNK), 1)
    causal = si <= ti
    sbt = sbt_ref[...]
    for hd in range(SGU_HEADS):
        wm = jnp.where(causal, sw_ref[hd], jnp.zeros((), BF16))
        cols = slice(hd * SGU_HEAD_DIM, (hd + 1) * SGU_HEAD_DIM)
        bias = sbt[:, hd:hd + 1]
        for c in range(rows // SGU_CHUNK):
            rs = slice(c * SGU_CHUNK, (c + 1) * SGU_CHUNK)
            s_scr[rs, cols] = _dot(wm, v[rs, cols]) + bias
    yb = u * s_scr[...]
    yb_ref[...] = _rms(yb, og_ref[...]).astype(BF16)


def _in_proj(x2, g, w_in, perm, ln_g, ln_b, sgu_w, sgu_bt, og, to_cast):
    n = x2.shape[0]
    steps = n // ROWS_IN
    row = lambda i: (i, 0)
    return pl.pallas_call(
        _in_proj_kernel,
        name="in_proj",
        grid=(n // ROWS_IN,),
        in_specs=[
            pl.BlockSpec((ROWS_IN, D_MODEL), row),
            _resident((1, D_MODEL)),
            _resident(w_in.shape),
            _resident(perm.shape),
            _resident((1, D_SGU)),
            _resident((1, D_SGU)),
            _resident(sgu_w.shape),
            _resident(sgu_bt.shape),
            _resident((1, D_SGU)),
        ] + [_cast_block_spec(w, steps, row) for w in to_cast],
        out_specs=(pl.BlockSpec((ROWS_IN, D_SSM), row), pl.BlockSpec((ROWS_IN, D_SGU), row))
        + tuple(_cast_block_spec(w, steps, row) for w in to_cast),
        out_shape=(jax.ShapeDtypeStruct((n, D_SSM), F32), jax.ShapeDtypeStruct((n, D_SGU), BF16))
        + tuple(_cast_out_shape(w) for w in to_cast),
        scratch_shapes=[pltpu.VMEM((ROWS_IN, D_SGU), F32)],
        compiler_params=pltpu.CompilerParams(
            dimension_semantics=("arbitrary",), vmem_limit_bytes=VMEM_LIMIT),
    )(x2, g, w_in, perm, ln_g, ln_b, sgu_w, sgu_bt, og, *to_cast)


def _s5_kernel(zp_ref, bd_ref, cd_ref, tab_ref, d_ref, c0_ref,
               out_ref, c0_out, bu_a, bu_b, sb_a, sb_b, carry_ref):
    c0_out[...] = c0_ref[...].astype(BF16)

    @pl.when(pl.program_id(1) == 0)
    def _():
        carry_ref[...] = jnp.zeros_like(carry_ref)

    zp = zp_ref[...]
    zpb = zp.astype(BF16)
    bu = (bu_a, bu_b)
    sb = (sb_a, sb_b)
    seg = lax.broadcasted_iota(jnp.int32, (SUBLANES, SCAN_W), 0)
    zero = jnp.zeros((SUBLANES, SCAN_W), F32)
    blk = lambda j: slice(j * BLK_CH, (j + 1) * BLK_CH)

    for p in range(N_PIECE):
        bu[0][p] = _dot(zpb[:, blk(0)], bd_ref[0, p])
    for j in range(N_BLK):
        out_ref[:, blk(j)] = d_ref[:, blk(j)] * zp[:, blk(j)]

    for j in range(N_BLK):
        cur, oth = j % 2, (j + 1) % 2

        def body(g, _, j=j, cur=cur, oth=oth):
            for part in range(2):
                p = 2 * g + part
                if j + 1 < N_BLK:
                    bu[oth][p] = _dot(zpb[:, blk(j + 1)], bd_ref[j + 1, p])
                if j >= 1:
                    out_ref[:, blk(j - 1)] += _dot(sb[oth][p], cd_ref[j - 1, p])

            re_p, im_p = 2 * g, 2 * g + 1
            tre, tim = slice(0, SCAN_W), slice(SCAN_W, 2 * SCAN_W)
            ar = jnp.broadcast_to(tab_ref[j, g, 0:1, tre], (SUBLANES, SCAN_W))
            ai = jnp.broadcast_to(tab_ref[j, g, 0:1, tim], (SUBLANES, SCAN_W))

            def step(row0, sr, si):
                rows = pl.ds(row0, SUBLANES)
                return (ar * sr - ai * si + bu[cur][re_p, rows, :],
                        ar * si + ai * sr + bu[cur][im_p, rows, :])

            def start_states(xr, xi):
                for lvl, m in enumerate((1, 2, 4)):
                    mr = tab_ref[j, g, 1 + lvl:2 + lvl, tre]
                    mi = tab_ref[j, g, 1 + lvl:2 + lvl, tim]
                    rr = pltpu.roll(xr, m, 0)
                    ri = pltpu.roll(xi, m, 0)
                    keep = seg >= m
                    xr, xi = (xr + jnp.where(keep, mr * rr - mi * ri, 0.0),
                              xi + jnp.where(keep, mr * ri + mi * rr, 0.0))
                cr = carry_ref[j, g, :, tre]
                ci = carry_ref[j, g, :, tim]
                pr = tab_ref[j, g, 4:4 + SUBLANES, tre]
                pi = tab_ref[j, g, 4:4 + SUBLANES, tim]
                xr, xi = xr + pr * cr - pi * ci, xi + pr * ci + pi * cr
                carry_ref[j, g, :, tre] = jnp.broadcast_to(xr[SUBLANES - 1:SUBLANES], (SUBLANES, SCAN_W))
                carry_ref[j, g, :, tim] = jnp.broadcast_to(xi[SUBLANES - 1:SUBLANES], (SUBLANES, SCAN_W))
                return (jnp.where(seg >= 1, pltpu.roll(xr, 1, 0), cr),
                        jnp.where(seg >= 1, pltpu.roll(xi, 1, 0), ci))

            def emit_pair(base, m, sr, si):
                r1, i1 = step(base + 2 * m * SUBLANES, sr, si)
                sr, si = step(base + (2 * m + 1) * SUBLANES, r1, i1)
                rows = pl.ds(base + m * 2 * SUBLANES, 2 * SUBLANES)
                sb[cur][re_p, rows, :] = jnp.concatenate([r1, sr], axis=0).astype(BF16)
                sb[cur][im_p, rows, :] = jnp.concatenate([i1, si], axis=0).astype(BF16)
                return sr, si

            n_chunk = S5_ROWS // T_CHUNK
            state = None
            for c in range(n_chunk + 1):
                xr, xi = zero, zero
                for m in range(K_STEPS // 2):
                    if c < n_chunk:
                        xr, xi = step(c * T_CHUNK + 2 * m * SUBLANES, xr, xi)
                        xr, xi = step(c * T_CHUNK + (2 * m + 1) * SUBLANES, xr, xi)
                    if c >= 1:
                        state = emit_pair((c - 1) * T_CHUNK, m, *state)
                if c < n_chunk:
                    state = start_states(xr, xi)
            return 0

        lax.fori_loop(0, N_GRP, body, 0)

    last = N_BLK - 1
    for p in range(N_PIECE):
        out_ref[:, blk(last)] += _dot(sb[last % 2][p], cd_ref[last, p])


def _s5(zs, bsz, seq, bd, cd, tab, dvec, to_cast):
    steps = seq // S5_ROWS
    row = lambda b, c: (b * steps + c, 0)
    return pl.pallas_call(
        _s5_kernel,
        name="s5",
        grid=(bsz, steps),
        in_specs=[
            pl.BlockSpec((S5_ROWS, D_SSM), row),
            _resident(bd.shape), _resident(cd.shape), _resident(tab.shape),
            _resident(dvec.shape),
            _cast_block_spec(to_cast, bsz * steps, row),
        ],
        out_specs=(pl.BlockSpec((S5_ROWS, D_SSM), row), _cast_block_spec(to_cast, bsz * steps, row)),
        out_shape=(jax.ShapeDtypeStruct((bsz * seq, D_SSM), F32), _cast_out_shape(to_cast)),
        scratch_shapes=[
            pltpu.VMEM((N_PIECE, S5_ROWS, SCAN_W), F32),
            pltpu.VMEM((N_PIECE, S5_ROWS, SCAN_W), F32),
            pltpu.VMEM((N_PIECE, S5_ROWS, SCAN_W), BF16),
            pltpu.VMEM((N_PIECE, S5_ROWS, SCAN_W), BF16),
            pltpu.VMEM((N_BLK, N_GRP, SUBLANES, 2 * SCAN_W), F32),
        ],
        compiler_params=pltpu.CompilerParams(
            dimension_semantics=("arbitrary", "arbitrary"), vmem_limit_bytes=VMEM_LIMIT),
    )(zs, bd, cd, tab, dvec, to_cast)


def _out_proj_kernel(x_ref, y_ref, yb_ref, w_ref, g_ref, gluw_ref, glub_ref, og_ref, permt_ref,
                     x1_ref, h_ref):
    y = jax.nn.gelu(y_ref[...])
    gate = jax.nn.sigmoid(_dot(y.astype(BF16), gluw_ref[...]) + glub_ref[...])
    ya = _rms(y * gate, og_ref[...]).astype(BF16)
    permt = permt_ref[...]
    ya = jnp.concatenate(
        [_dot(permt, ya[c * T_CHUNK:(c + 1) * T_CHUNK]).astype(BF16)
         for c in range(x_ref.shape[0] // T_CHUNK)], axis=0)
    x1 = x_ref[...] + _dot(ya, w_ref[0:D_SSM, :]) + _dot(yb_ref[...], w_ref[D_SSM:, :])
    x1_ref[...] = x1
    h_ref[...] = _rms(x1, g_ref[...]).astype(BF16)


def _out_proj(x2, y, yb, w_out, g, glu_w, glu_b, og, permt):
    n = x2.shape[0]
    row = lambda i: (i, 0)
    return pl.pallas_call(
        _out_proj_kernel,
        name="out_proj",
        grid=(n // ROWS_IN,),
        in_specs=[
            pl.BlockSpec((ROWS_IN, D_MODEL), row),
            pl.BlockSpec((ROWS_IN, D_SSM), row),
            pl.BlockSpec((ROWS_IN, D_SGU), row),
            _resident(w_out.shape),
            _resident((1, D_MODEL)),
            _resident(glu_w.shape), _resident(glu_b.shape), _resident(og.shape),
            _resident(permt.shape),
        ],
        out_specs=(pl.BlockSpec((ROWS_IN, D_MODEL), row), pl.BlockSpec((ROWS_IN, D_MODEL), row)),
        out_shape=(jax.ShapeDtypeStruct((n, D_MODEL), F32), jax.ShapeDtypeStruct((n, D_MODEL), BF16)),
        compiler_params=pltpu.CompilerParams(
            dimension_semantics=("arbitrary",), vmem_limit_bytes=VMEM_LIMIT),
    )(x2, y, yb, w_out, g, glu_w, glu_b, og, permt)


def _ffn_kernel(h_ref, wg_ref, wu_ref, wo_ref, out_ref):
    @pl.when(pl.program_id(1) == 0)
    def _():
        out_ref[...] = jnp.zeros_like(out_ref)

    h = h_ref[...]
    act = jax.nn.silu(_dot(h, wg_ref[...])) * _dot(h, wu_ref[...])
    out_ref[...] += _dot(act.astype(BF16), wo_ref[...])


def _ffn(h, w_in, w_out):
    n = h.shape[0]
    nf = D_FFN // FFN_BLK
    return pl.pallas_call(
        _ffn_kernel,
        name="ffn",
        grid=(n // ROWS_FFN, nf),
        in_specs=[
            pl.BlockSpec((ROWS_FFN, D_MODEL), lambda i, f: (i, 0)),
            pl.BlockSpec((D_MODEL, FFN_BLK), lambda i, f: (0, f)),
            pl.BlockSpec((D_MODEL, FFN_BLK), lambda i, f: (0, f + nf)),
            pl.BlockSpec((FFN_BLK, D_MODEL), lambda i, f: (f, 0)),
        ],
        out_specs=pl.BlockSpec((ROWS_FFN, D_MODEL), lambda i, f: (i, 0)),
        out_shape=jax.ShapeDtypeStruct((n, D_MODEL), F32),
        compiler_params=pltpu.CompilerParams(
            dimension_semantics=("arbitrary", "arbitrary"), vmem_limit_bytes=VMEM_LIMIT),
    )(h, w_in, w_in, w_out)


def _ple_kernel(x_ref, f_ref, p_ref, wg_ref, bg_ref, wp_ref, g_ref, gf_ref, out_ref, *, final):
    x = x_ref[...] + f_ref[...]
    h = _rms(x, g_ref[...]).astype(BF16)
    gate = jax.nn.sigmoid(_dot(h, wg_ref[...]) + bg_ref[...])
    x3 = x + gate * _dot(p_ref[...].astype(BF16), wp_ref[...])
    out_ref[...] = _rms(x3, gf_ref[...]) if final else x3


def _ple(x1, ffn, p2, w_gate, b_gate, w_proj, g, gf, final):
    n = x1.shape[0]
    row = lambda i: (i, 0)
    return pl.pallas_call(
        functools.partial(_ple_kernel, final=final),
        name="ple",
        grid=(n // ROWS_IN,),
        in_specs=[
            pl.BlockSpec((ROWS_IN, D_MODEL), row),
            pl.BlockSpec((ROWS_IN, D_MODEL), row),
            pl.BlockSpec((ROWS_IN, PLE_DIM), row),
            _resident(w_gate.shape),
            _resident((1, D_MODEL)),
            _resident(w_proj.shape),
            _resident((1, D_MODEL)),
            _resident((1, D_MODEL)),
        ],
        out_specs=pl.BlockSpec((ROWS_IN, D_MODEL), row),
        out_shape=jax.ShapeDtypeStruct((n, D_MODEL), F32),
        compiler_params=pltpu.CompilerParams(
            dimension_semantics=("arbitrary",), vmem_limit_bytes=VMEM_LIMIT),
    )(x1, ffn, p2, w_gate, b_gate, w_proj, g, gf)


def _segment_permutation():
    r = np.arange(T_CHUNK)
    src = (r % SUBLANES) * K_STEPS + r // SUBLANES
    perm = np.zeros((T_CHUNK, T_CHUNK), np.float32)
    perm[r, src] = 1.0
    return perm


def _scan_exponents():
    n = [1.0]
    n += [float(K_STEPS * m) for m in (1, 2, 4)]
    n += [float(K_STEPS * (i + 1)) for i in range(SUBLANES)]
    return np.asarray(n, np.float32).reshape(-1, 1)


def _block_diag_b(t):
    t = jnp.broadcast_to(t.reshape(SSM_GROUP, N_BLK, 1, BLK_ST),
                         (SSM_GROUP, N_BLK, SSM_GROUP, BLK_ST)).transpose(1, 2, 0, 3)
    same = (np.arange(SSM_GROUP)[:, None, None] == (np.arange(BLK_ST) // SSM_STATE)[None, None, :])
    return jnp.where(same, t, 0.0).reshape(N_BLK, BLK_CH, BLK_ST)


def _block_diag_c(t):
    t = t.reshape(N_BLK, SSM_GROUP, SSM_GROUP, SSM_STATE).transpose(0, 1, 3, 2)
    t = jnp.broadcast_to(t.reshape(N_BLK, BLK_ST, 1, SSM_GROUP), (N_BLK, BLK_ST, SSM_GROUP, SSM_GROUP))
    same = ((np.arange(BLK_ST) // SSM_STATE)[:, None, None] == np.arange(SSM_GROUP)[None, :, None])
    return jnp.where(same, t, 0.0).reshape(N_BLK, BLK_ST, BLK_CH)


def kernel(x, p, norm_mix_g, w_in, ssm_lambda_re, ssm_lambda_im, ssm_log_step, ssm_b_re, ssm_b_im, ssm_c_re, ssm_c_im, ssm_d, ssm_glu_w, ssm_glu_b, sgu_ln_g, sgu_ln_b, sgu_w, sgu_b, out_norm_ssm_g, out_norm_sgu_g, w_out, norm_ffn_g, w_ffn_in, w_ffn_out, norm_ple_g, w_ple_gate, b_ple_gate, w_ple_proj, final_norm_g):
    bsz, seq, _ = x.shape
    depth = w_in.shape[0]
    perm_np = _segment_permutation()
    perm = jnp.asarray(perm_np, BF16)
    permt = jnp.asarray(perm_np.T, BF16)
    exponents = jnp.asarray(_scan_exponents())
    x2 = x.reshape(bsz * seq, D_MODEL)
    vec = lambda a: a.reshape(1, -1)
    for i in range(depth):
        pw_re, pw_im, bb_re, bb_im = _s5_params(
            ssm_lambda_re[i], ssm_lambda_im[i], ssm_log_step[i], ssm_b_re[i], ssm_b_im[i], exponents)
        tab = _table_layout(pw_re, pw_im)
        bd = _pieces(_block_diag_b(bb_re), _block_diag_b(bb_im), axis=2).astype(BF16)
        cd = _pieces(_block_diag_c(ssm_c_re[i]), _block_diag_c(-ssm_c_im[i]), axis=1).astype(BF16)

        zs, yb, w_out_bf, w_gate_bf, glu_bf, w_ffn_out_bf = _in_proj(
            x2, vec(norm_mix_g[i]), w_in[i].astype(BF16), perm, vec(sgu_ln_g[i]), vec(sgu_ln_b[i]),
            sgu_w[i].astype(BF16), sgu_b[i].T, vec(out_norm_sgu_g[i]),
            to_cast=(w_out[i], w_ple_gate[i], ssm_glu_w[i], w_ffn_out[i]))
        y, w_ffn_in_bf = _s5(zs, bsz, seq, bd, cd, tab, vec(ssm_d[i]), to_cast=w_ffn_in[i])
        x1, h = _out_proj(x2, y, yb, w_out_bf, vec(norm_ffn_g[i]), glu_bf, vec(ssm_glu_b[i]),
                          vec(out_norm_ssm_g[i]), permt)
        ffn = _ffn(h, w_ffn_in_bf, w_ffn_out_bf)
        x2 = _ple(x1, ffn, p[i].reshape(bsz * seq, PLE_DIM), w_gate_bf, vec(b_ple_gate[i]),
                  w_ple_proj[i].astype(BF16), vec(norm_ple_g[i]), vec(final_norm_g),
                  final=(i == depth - 1))
    return x2.reshape(bsz, seq, D_MODEL)
```

```python
import functools

import numpy as np
import jax
import jax.numpy as jnp
from jax import lax
from jax.experimental import pallas as pl
from jax.experimental.pallas import tpu as pltpu

D_MODEL = 2048
D_SSM = 1024
D_SGU = 1024
SSM_GROUP = 16
SSM_GROUPS = 64
SSM_STATE = 64
SGU_CHUNK = 128
SGU_HEADS = 8
SGU_HEAD_DIM = D_SGU // SGU_HEADS
D_FFN = 5632
PLE_DIM = 256
EPS = 1e-6
LAMBDA_RE_MAX = -1e-4

SUBLANES = 8
N_STATE = SSM_GROUPS * SSM_STATE
N_BLK = 4
BLK_CH = D_SSM // N_BLK
BLK_ST = N_STATE // N_BLK
BLK_W = 2 * BLK_ST
T_CHUNK = 256
K_STEPS = T_CHUNK // SUBLANES
S5_ROWS = 512
SCAN_W = 512
N_GRP = BLK_ST // SCAN_W
N_PIECE = 2 * N_GRP

ROWS_IN = 512
ROWS_FFN = 1024
FFN_BLK = 512
VMEM_LIMIT = 56 * 1024 * 1024
VMEM_LIMIT_FFN = 60 * 1024 * 1024

F32 = jnp.float32
BF16 = jnp.bfloat16


def _rms(x, g):
    r = lax.rsqrt(jnp.mean(x * x, axis=-1, keepdims=True) + EPS)
    return (x * r) * g


def _dot(a, b):
    return jnp.dot(a, b, preferred_element_type=F32)


def _resident(shape):
    nd = len(shape)
    return pl.BlockSpec(shape, lambda *_: (0,) * nd, pipeline_mode=pl.Buffered(1))


def _cast_block_spec(w, steps, index):
    return pl.BlockSpec((w.shape[0] // steps, w.shape[1]), index)


def _cast_out_shape(w):
    return jax.ShapeDtypeStruct(w.shape, BF16)


def _s5_params_kernel(lr_ref, li_ref, ls_ref, n_ref, bre_ref, bim_ref,
                      pwr_ref, pwi_ref, bbr_ref, bbi_ref):
    lr = jnp.minimum(lr_ref[...], LAMBDA_RE_MAX)
    li = li_ref[...]
    dt = jnp.exp(ls_ref[...])
    n = n_ref[...]
    mag = jnp.exp(n * (lr * dt))
    ang = n * (li * dt)
    pwr_ref[...] = mag * jnp.cos(ang)
    pwi_ref[...] = mag * jnp.sin(ang)
    mag1 = jnp.exp(lr * dt)
    ang1 = li * dt
    nr = mag1 * jnp.cos(ang1) - 1.0
    ni = mag1 * jnp.sin(ang1)
    den = lr * lr + li * li
    q_re = (nr * lr + ni * li) / den
    q_im = (ni * lr - nr * li) / den
    bre = bre_ref[...]
    bim = bim_ref[...]
    bbr_ref[...] = q_re * bre - q_im * bim
    bbi_ref[...] = q_re * bim + q_im * bre


def _s5_params(lam_re, lam_im, log_step, b_re, b_im, exponents):
    n_rows = exponents.shape[0]
    lr = lam_re.reshape(1, N_STATE)
    li = lam_im.reshape(1, N_STATE)
    ls = jnp.repeat(log_step, SSM_STATE).reshape(1, N_STATE)
    bre = b_re.transpose(2, 0, 1).reshape(SSM_GROUP, N_STATE)
    bim = b_im.transpose(2, 0, 1).reshape(SSM_GROUP, N_STATE)
    return pl.pallas_call(
        _s5_params_kernel,
        name="s5_params",
        out_shape=(jax.ShapeDtypeStruct((n_rows, N_STATE), F32),
                   jax.ShapeDtypeStruct((n_rows, N_STATE), F32),
                   jax.ShapeDtypeStruct((SSM_GROUP, N_STATE), F32),
                   jax.ShapeDtypeStruct((SSM_GROUP, N_STATE), F32)),
    )(lr, li, ls, exponents, bre, bim)


def _table_layout(re, im):
    r = re.shape[0]
    cat = jnp.concatenate([re.reshape(r, N_BLK, N_GRP, SCAN_W), im.reshape(r, N_BLK, N_GRP, SCAN_W)],
                          axis=-1)
    return cat.transpose(1, 2, 0, 3)


def _pieces(re, im, axis):
    parts = []
    for g in range(N_GRP):
        idx = [slice(None)] * re.ndim
        idx[axis] = slice(g * SCAN_W, (g + 1) * SCAN_W)
        parts += [re[tuple(idx)], im[tuple(idx)]]
    return jnp.stack(parts, axis=1)


def _in_proj_kernel(x_ref, g_ref, w_ref, perm_ref, lng_ref, lnb_ref, sw_ref, sbt_ref, og_ref,
                    c0_ref, c1_ref, c2_ref, c3_ref,
                    zs_ref, yb_ref, c0_out, c1_out, c2_out, c3_out, s_scr):
    c0_out[...] = c0_ref[...].astype(BF16)
    c1_out[...] = c1_ref[...].astype(BF16)
    c2_out[...] = c2_ref[...].astype(BF16)
    c3_out[...] = c3_ref[...].astype(BF16)
    rows = x_ref.shape[0]
    h = _rms(x_ref[...], g_ref[...]).astype(BF16)
    gv = jax.nn.gelu(_dot(h, w_ref[:, D_SSM + D_SGU:]))
    mu = jnp.mean(gv, axis=-1, keepdims=True)
    xc = gv - mu
    r = lax.rsqrt(jnp.mean(xc * xc, axis=-1, keepdims=True) + EPS)
    v = ((xc * r) * lng_ref[...] + lnb_ref[...]).astype(BF16)
    u = jax.nn.gelu(_dot(h, w_ref[:, D_SSM:D_SSM + D_SGU]))
    ti = lax.broadcasted_iota(jnp.int32, (SGU_CHUNK, SGU_CHUNK), 0)
    si = lax.broadcasted_iota(jnp.int32, (SGU_CHUNK, SGU_CHUNK), 1)
    causal = si <= ti
    sbt = sbt_ref[...]
    for hd in range(SGU_HEADS):
        wm = jnp.where(causal, sw_ref[hd], jnp.zeros((), BF16))
        cols = slice(hd * SGU_HEAD_DIM, (hd + 1) * SGU_HEAD_DIM)
        bias = sbt[:, hd:hd + 1]
        for c in range(rows // SGU_CHUNK):
            rs = slice(c * SGU_CHUNK, (c + 1) * SGU_CHUNK)
            s_scr[rs, cols] = _dot(wm, v[rs, cols]) + bias
    yb = u * s_scr[...]
    yb_ref[...] = _rms(yb, og_ref[...]).astype(BF16)
    perm = perm_ref[...]
    hp = jnp.concatenate(
        [_dot(perm, h[c * T_CHUNK:(c + 1) * T_CHUNK]).astype(BF16) for c in range(rows // T_CHUNK)],
        axis=0)
    zs_ref[...] = _dot(hp, w_ref[:, 0:D_SSM])


def _in_proj(x2, g, w_in, perm, ln_g, ln_b, sgu_w, sgu_bt, og, to_cast):
    n = x2.shape[0]
    steps = n // ROWS_IN
    row = lambda i: (i, 0)
    return pl.pallas_call(
        _in_proj_kernel,
        name="in_proj",
        grid=(n // ROWS_IN,),
        in_specs=[
            pl.BlockSpec((ROWS_IN, D_MODEL), row),
            _resident((1, D_MODEL)),
            _resident(w_in.shape),
            _resident(perm.shape),
            _resident((1, D_SGU)),
            _resident((1, D_SGU)),
            _resident(sgu_w.shape),
            _resident(sgu_bt.shape),
            _resident((1, D_SGU)),
        ] + [_cast_block_spec(w, steps, row) for w in to_cast],
        out_specs=(pl.BlockSpec((ROWS_IN, D_SSM), row), pl.BlockSpec((ROWS_IN, D_SGU), row))
        + tuple(_cast_block_spec(w, steps, row) for w in to_cast),
        out_shape=(jax.ShapeDtypeStruct((n, D_SSM), F32), jax.ShapeDtypeStruct((n, D_SGU), BF16))
        + tuple(_cast_out_shape(w) for w in to_cast),
        scratch_shapes=[pltpu.VMEM((ROWS_IN, D_SGU), F32)],
        compiler_params=pltpu.CompilerParams(
            dimension_semantics=("arbitrary",), vmem_limit_bytes=VMEM_LIMIT),
    )(x2, g, w_in, perm, ln_g, ln_b, sgu_w, sgu_bt, og, *to_cast)


def _s5_kernel(zp_ref, bd_ref, cd_ref, tab_ref, d_ref, c0_ref,
               out_ref, c0_out, bu_a, bu_b, sb_a, sb_b, carry_ref):
    c0_out[...] = c0_ref[...].astype(BF16)

    @pl.when(pl.program_id(1) == 0)
    def _():
        carry_ref[...] = jnp.zeros_like(carry_ref)

    zp = zp_ref[...]
    zpb = zp.astype(BF16)
    bu = (bu_a, bu_b)
    sb = (sb_a, sb_b)
    seg = lax.broadcasted_iota(jnp.int32, (SUBLANES, SCAN_W), 0)
    zero = jnp.zeros((SUBLANES, SCAN_W), F32)
    blk = lambda j: slice(j * BLK_CH, (j + 1) * BLK_CH)

    for p in range(N_PIECE):
        bu[0][p] = _dot(zpb[:, blk(0)], bd_ref[0, p])
    for j in range(N_BLK):
        out_ref[:, blk(j)] = d_ref[:, blk(j)] * zp[:, blk(j)]

    for j in range(N_BLK):
        cur, oth = j % 2, (j + 1) % 2

        def body(g, _, j=j, cur=cur, oth=oth):
            for part in range(2):
                p = 2 * g + part
                if j + 1 < N_BLK:
                    bu[oth][p] = _dot(zpb[:, blk(j + 1)], bd_ref[j + 1, p])
                if j >= 1:
                    out_ref[:, blk(j - 1)] += _dot(sb[oth][p], cd_ref[j - 1, p])

            re_p, im_p = 2 * g, 2 * g + 1
            tre, tim = slice(0, SCAN_W), slice(SCAN_W, 2 * SCAN_W)
            ar = jnp.broadcast_to(tab_ref[j, g, 0:1, tre], (SUBLANES, SCAN_W))
            ai = jnp.broadcast_to(tab_ref[j, g, 0:1, tim], (SUBLANES, SCAN_W))

            def step(row0, sr, si):
                rows = pl.ds(row0, SUBLANES)
                return (ar * sr - ai * si + bu[cur][re_p, rows, :],
                        ar * si + ai * sr + bu[cur][im_p, rows, :])

            def start_states(xr, xi):
                for lvl, m in enumerate((1, 2, 4)):
                    mr = tab_ref[j, g, 1 + lvl:2 + lvl, tre]
                    mi = tab_ref[j, g, 1 + lvl:2 + lvl, tim]
                    rr = pltpu.roll(xr, m, 0)
                    ri = pltpu.roll(xi, m, 0)
                    keep = seg >= m
                    xr, xi = (xr + jnp.where(keep, mr * rr - mi * ri, 0.0),
                              xi + jnp.where(keep, mr * ri + mi * rr, 0.0))
                cr = carry_ref[j, g, :, tre]
                ci = carry_ref[j, g, :, tim]
                pr = tab_ref[j, g, 4:4 + SUBLANES, tre]
                pi = tab_ref[j, g, 4:4 + SUBLANES, tim]
                xr, xi = xr + pr * cr - pi * ci, xi + pr * ci + pi * cr
                carry_ref[j, g, :, tre] = jnp.broadcast_to(xr[SUBLANES - 1:SUBLANES], (SUBLANES, SCAN_W))
                carry_ref[j, g, :, tim] = jnp.broadcast_to(xi[SUBLANES - 1:SUBLANES], (SUBLANES, SCAN_W))
                return (jnp.where(seg >= 1, pltpu.roll(xr, 1, 0), cr),
                        jnp.where(seg >= 1, pltpu.roll(xi, 1, 0), ci))

            def emit_pair(base, m, sr, si):
                r1, i1 = step(base + 2 * m * SUBLANES, sr, si)
                sr, si = step(base + (2 * m + 1) * SUBLANES, r1, i1)
                rows = pl.ds(base + m * 2 * SUBLANES, 2 * SUBLANES)
                sb[cur][re_p, rows, :] = jnp.concatenate([r1, sr], axis=0).astype(BF16)
                sb[cur][im_p, rows, :] = jnp.concatenate([i1, si], axis=0).astype(BF16)
                return sr, si

            n_chunk = S5_ROWS // T_CHUNK
            state = None
            for c in range(n_chunk + 1):
                xr, xi = zero, zero
                for m in range(K_STEPS // 2):
                    if c < n_chunk:
                        xr, xi = step(c * T_CHUNK + 2 * m * SUBLANES, xr, xi)
                        xr, xi = step(c * T_CHUNK + (2 * m + 1) * SUBLANES, xr, xi)
                    if c >= 1:
                        state = emit_pair((c - 1) * T_CHUNK, m, *state)
                if c < n_chunk:
                    state = start_states(xr, xi)
            return 0

        lax.fori_loop(0, N_GRP, body, 0)

    last = N_BLK - 1
    for p in range(N_PIECE):
        out_ref[:, blk(last)] += _dot(sb[last % 2][p], cd_ref[last, p])


def _s5(zs, bsz, seq, bd, cd, tab, dvec, to_cast):
    steps = seq // S5_ROWS
    row = lambda b, c: (b * steps + c, 0)
    return pl.pallas_call(
        _s5_kernel,
        name="s5",
        grid=(bsz, steps),
        in_specs=[
            pl.BlockSpec((S5_ROWS, D_SSM), row),
            _resident(bd.shape), _resident(cd.shape), _resident(tab.shape),
            _resident(dvec.shape),
            _cast_block_spec(to_cast, bsz * steps, row),
        ],
        out_specs=(pl.BlockSpec((S5_ROWS, D_SSM), row), _cast_block_spec(to_cast, bsz * steps, row)),
        out_shape=(jax.ShapeDtypeStruct((bsz * seq, D_SSM), F32), _cast_out_shape(to_cast)),
        scratch_shapes=[
            pltpu.VMEM((N_PIECE, S5_ROWS, SCAN_W), F32),
            pltpu.VMEM((N_PIECE, S5_ROWS, SCAN_W), F32),
            pltpu.VMEM((N_PIECE, S5_ROWS, SCAN_W), BF16),
            pltpu.VMEM((N_PIECE, S5_ROWS, SCAN_W), BF16),
            pltpu.VMEM((N_BLK, N_GRP, SUBLANES, 2 * SCAN_W), F32),
        ],
        compiler_params=pltpu.CompilerParams(
            dimension_semantics=("arbitrary", "arbitrary"), vmem_limit_bytes=VMEM_LIMIT),
    )(zs, bd, cd, tab, dvec, to_cast)


def _out_proj_kernel(x_ref, y_ref, yb_ref, w_ref, g_ref, gluw_ref, glub_ref, og_ref, permt_ref,
                     x1_ref, h_ref):
    half = D_MODEL // 2
    yb = yb_ref[...]
    y = jax.nn.gelu(y_ref[...])
    x1_lo = x_ref[:, 0:half] + _dot(yb, w_ref[D_SSM:, 0:half])
    gate = jax.nn.sigmoid(_dot(y.astype(BF16), gluw_ref[...]) + glub_ref[...])
    ya = _rms(y * gate, og_ref[...]).astype(BF16)
    x1_hi = x_ref[:, half:] + _dot(yb, w_ref[D_SSM:, half:])
    permt = permt_ref[...]
    ya = jnp.concatenate(
        [_dot(permt, ya[c * T_CHUNK:(c + 1) * T_CHUNK]).astype(BF16)
         for c in range(x_ref.shape[0] // T_CHUNK)], axis=0)
    x1 = jnp.concatenate([x1_lo, x1_hi], axis=1) + _dot(ya, w_ref[0:D_SSM, :])
    x1_ref[...] = x1
    h_ref[...] = _rms(x1, g_ref[...]).astype(BF16)


def _out_proj(x2, y, yb, w_out, g, glu_w, glu_b, og, permt):
    n = x2.shape[0]
    row = lambda i: (i, 0)
    return pl.pallas_call(
        _out_proj_kernel,
        name="out_proj",
        grid=(n // ROWS_IN,),
        in_specs=[
            pl.BlockSpec((ROWS_IN, D_MODEL), row),
            pl.BlockSpec((ROWS_IN, D_SSM), row),
            pl.BlockSpec((ROWS_IN, D_SGU), row),
            _resident(w_out.shape),
            _resident((1, D_MODEL)),
            _resident(glu_w.shape), _resident(glu_b.shape), _resident(og.shape),
            _resident(permt.shape),
        ],
        out_specs=(pl.BlockSpec((ROWS_IN, D_MODEL), row), pl.BlockSpec((ROWS_IN, D_MODEL), row)),
        out_shape=(jax.ShapeDtypeStruct((n, D_MODEL), F32), jax.ShapeDtypeStruct((n, D_MODEL), BF16)),
        compiler_params=pltpu.CompilerParams(
            dimension_semantics=("arbitrary",), vmem_limit_bytes=VMEM_LIMIT),
    )(x2, y, yb, w_out, g, glu_w, glu_b, og, permt)


def _ffn_kernel(h_ref, x1_ref, wg_ref, wu_ref, wo_ref, out_ref):
    @pl.when(pl.program_id(1) == 0)
    def _():
        out_ref[...] = x1_ref[...]

    h = h_ref[...]
    act = jax.nn.silu(_dot(h, wg_ref[...])) * _dot(h, wu_ref[...])
    out_ref[...] += _dot(act.astype(BF16), wo_ref[...])


def _ffn(h, x1, w_in, w_out):
    n = h.shape[0]
    nf = D_FFN // FFN_BLK
    return pl.pallas_call(
        _ffn_kernel,
        name="ffn",
        grid=(n // ROWS_FFN, nf),
        in_specs=[
            pl.BlockSpec((ROWS_FFN, D_MODEL), lambda i, f: (i, 0)),
            pl.BlockSpec((ROWS_FFN, D_MODEL), lambda i, f: (i, 0)),
            pl.BlockSpec((D_MODEL, FFN_BLK), lambda i, f: (0, f)),
            pl.BlockSpec((D_MODEL, FFN_BLK), lambda i, f: (0, f + nf)),
            pl.BlockSpec((FFN_BLK, D_MODEL), lambda i, f: (f, 0)),
        ],
        out_specs=pl.BlockSpec((ROWS_FFN, D_MODEL), lambda i, f: (i, 0)),
        out_shape=jax.ShapeDtypeStruct((n, D_MODEL), F32),
        compiler_params=pltpu.CompilerParams(
            dimension_semantics=("arbitrary", "arbitrary"), vmem_limit_bytes=VMEM_LIMIT_FFN),
    )(h, x1, w_in, w_in, w_out)


def _ple_kernel(x_ref, p_ref, wg_ref, bg_ref, wp_ref, g_ref, gf_ref, out_ref, *, final):
    x = x_ref[...]
    h = _rms(x, g_ref[...]).astype(BF16)
    pe = _dot(p_ref[...].astype(BF16), wp_ref[...])
    gate = jax.nn.sigmoid(_dot(h, wg_ref[...]) + bg_ref[...])
    x3 = x + gate * pe
    out_ref[...] = _rms(x3, gf_ref[...]) if final else x3


def _ple(x2, p2, w_gate, b_gate, w_proj, g, gf, final):
    n = x2.shape[0]
    row = lambda i: (i, 0)
    return pl.pallas_call(
        functools.partial(_ple_kernel, final=final),
        name="ple",
        grid=(n // ROWS_IN,),
        in_specs=[
            pl.BlockSpec((ROWS_IN, D_MODEL), row),
            pl.BlockSpec((ROWS_IN, PLE_DIM), row),
            _resident(w_gate.shape),
            _resident((1, D_MODEL)),
            _resident(w_proj.shape),
            _resident((1, D_MODEL)),
            _resident((1, D_MODEL)),
        ],
        out_specs=pl.BlockSpec((ROWS_IN, D_MODEL), row),
        out_shape=jax.ShapeDtypeStruct((n, D_MODEL), F32),
        compiler_params=pltpu.CompilerParams(
            dimension_semantics=("arbitrary",), vmem_limit_bytes=VMEM_LIMIT),
    )(x2, p2, w_gate, b_gate, w_proj, g, gf)


def _segment_permutation():
    r = np.arange(T_CHUNK)
    src = (r % SUBLANES) * K_STEPS + r // SUBLANES
    perm = np.zeros((T_CHUNK, T_CHUNK), np.float32)
    perm[r, src] = 1.0
    return perm


def _scan_exponents():
    n = [1.0]
    n += [float(K_STEPS * m) for m in (1, 2, 4)]
    n += [float(K_STEPS * (i + 1)) for i in range(SUBLANES)]
    return np.asarray(n, np.float32).reshape(-1, 1)


def _block_diag_b(t):
    t = jnp.broadcast_to(t.reshape(SSM_GROUP, N_BLK, 1, BLK_ST),
                         (SSM_GROUP, N_BLK, SSM_GROUP, BLK_ST)).transpose(1, 2, 0, 3)
    same = (np.arange(SSM_GROUP)[:, None, None] == (np.arange(BLK_ST) // SSM_STATE)[None, None, :])
    return jnp.where(same, t, 0.0).reshape(N_BLK, BLK_CH, BLK_ST)


def _block_diag_c(t):
    t = t.reshape(N_BLK, SSM_GROUP, SSM_GROUP, SSM_STATE).transpose(0, 1, 3, 2)
    t = jnp.broadcast_to(t.reshape(N_BLK, BLK_ST, 1, SSM_GROUP), (N_BLK, BLK_ST, SSM_GROUP, SSM_GROUP))
    same = ((np.arange(BLK_ST) // SSM_STATE)[:, None, None] == np.arange(SSM_GROUP)[None, :, None])
    return jnp.where(same, t, 0.0).reshape(N_BLK, BLK_ST, BLK_CH)


def kernel(x, p, norm_mix_g, w_in, ssm_lambda_re, ssm_lambda_im, ssm_log_step, ssm_b_re, ssm_b_im, ssm_c_re, ssm_c_im, ssm_d, ssm_glu_w, ssm_glu_b, sgu_ln_g, sgu_ln_b, sgu_w, sgu_b, out_norm_ssm_g, out_norm_sgu_g, w_out, norm_ffn_g, w_ffn_in, w_ffn_out, norm_ple_g, w_ple_gate, b_ple_gate, w_ple_proj, final_norm_g):
    bsz, seq, _ = x.shape
    depth = w_in.shape[0]
    perm_np = _segment_permutation()
    perm = jnp.asarray(perm_np, BF16)
    permt = jnp.asarray(perm_np.T, BF16)
    exponents = jnp.asarray(_scan_exponents())
    x2 = x.reshape(bsz * seq, D_MODEL)
    vec = lambda a: a.reshape(1, -1)
    for i in range(depth):
        pw_re, pw_im, bb_re, bb_im = _s5_params(
            ssm_lambda_re[i], ssm_lambda_im[i], ssm_log_step[i], ssm_b_re[i], ssm_b_im[i], exponents)
        tab = _table_layout(pw_re, pw_im)
        bd = _pieces(_block_diag_b(bb_re), _block_diag_b(bb_im), axis=2).astype(BF16)
        cd = _pieces(_block_diag_c(ssm_c_re[i]), _block_diag_c(-ssm_c_im[i]), axis=1).astype(BF16)

        zs, yb, w_out_bf, w_gate_bf, glu_bf, w_ffn_out_bf = _in_proj(
            x2, vec(norm_mix_g[i]), w_in[i].astype(BF16), perm, vec(sgu_ln_g[i]), vec(sgu_ln_b[i]),
            sgu_w[i].astype(BF16), sgu_b[i].T, vec(out_norm_sgu_g[i]),
            to_cast=(w_out[i], w_ple_gate[i], ssm_glu_w[i], w_ffn_out[i]))
        y, w_ffn_in_bf = _s5(zs, bsz, seq, bd, cd, tab, vec(ssm_d[i]), to_cast=w_ffn_in[i])
        x1, h = _out_proj(x2, y, yb, w_out_bf, vec(norm_ffn_g[i]), glu_bf, vec(ssm_glu_b[i]),
                          vec(out_norm_ssm_g[i]), permt)
        x2 = _ffn(h, x1, w_ffn_in_bf, w_ffn_out_bf)
        x2 = _ple(x2, p[i].reshape(bsz * seq, PLE_DIM), w_gate_bf, vec(b_ple_gate[i]),
                  w_ple_proj[i].astype(BF16), vec(norm_ple_g[i]), vec(final_norm_g),
                  final=(i == depth - 1))
    return x2.reshape(bsz, seq, D_MODEL)
```

```python
import functools

import numpy as np
import jax
import jax.numpy as jnp
from jax import lax
from jax.experimental import pallas as pl
from jax.experimental.pallas import tpu as pltpu

D_MODEL = 2048
D_SSM = 1024
D_SGU = 1024
SSM_GROUP = 16
SSM_GROUPS = 64
SSM_STATE = 64
SGU_CHUNK = 128
SGU_HEADS = 8
SGU_HEAD_DIM = D_SGU // SGU_HEADS
D_FFN = 5632
PLE_DIM = 256
EPS = 1e-6
LAMBDA_RE_MAX = -1e-4

SUBLANES = 8
N_STATE = SSM_GROUPS * SSM_STATE
N_BLK = 4
BLK_CH = D_SSM // N_BLK
BLK_ST = N_STATE // N_BLK
BLK_W = 2 * BLK_ST
T_CHUNK = 256
K_STEPS = T_CHUNK // SUBLANES
S5_ROWS = 512
SCAN_W = 512
N_GRP = BLK_ST // SCAN_W
N_PIECE = 2 * N_GRP

ROWS_IN = 512
ROWS_FFN = 1024
FFN_BLK = 512
VMEM_LIMIT = 56 * 1024 * 1024
VMEM_LIMIT_FFN = 60 * 1024 * 1024

F32 = jnp.float32
BF16 = jnp.bfloat16


def _rms(x, g):
    r = lax.rsqrt(jnp.mean(x * x, axis=-1, keepdims=True) + EPS)
    return (x * r) * g


def _dot(a, b):
    return jnp.dot(a, b, preferred_element_type=F32)


def _resident(shape):
    nd = len(shape)
    return pl.BlockSpec(shape, lambda *_: (0,) * nd, pipeline_mode=pl.Buffered(1))


def _cast_block_spec(w, steps, index):
    return pl.BlockSpec((w.shape[0] // steps, w.shape[1]), index)


def _cast_out_shape(w):
    return jax.ShapeDtypeStruct(w.shape, BF16)


def _s5_params_kernel(lr_ref, li_ref, ls_ref, n_ref, bre_ref, bim_ref,
                      pwr_ref, pwi_ref, bbr_ref, bbi_ref):
    lr = jnp.minimum(lr_ref[...], LAMBDA_RE_MAX)
    li = li_ref[...]
    dt = jnp.exp(ls_ref[...])
    n = n_ref[...]
    mag = jnp.exp(n * (lr * dt))
    ang = n * (li * dt)
    pwr_ref[...] = mag * jnp.cos(ang)
    pwi_ref[...] = mag * jnp.sin(ang)
    mag1 = jnp.exp(lr * dt)
    ang1 = li * dt
    nr = mag1 * jnp.cos(ang1) - 1.0
    ni = mag1 * jnp.sin(ang1)
    den = lr * lr + li * li
    q_re = (nr * lr + ni * li) / den
    q_im = (ni * lr - nr * li) / den
    bre = bre_ref[...]
    bim = bim_ref[...]
    bbr_ref[...] = q_re * bre - q_im * bim
    bbi_ref[...] = q_re * bim + q_im * bre


def _s5_params(lam_re, lam_im, log_step, b_re, b_im, exponents):
    n_rows = exponents.shape[0]
    lr = lam_re.reshape(1, N_STATE)
    li = lam_im.reshape(1, N_STATE)
    ls = jnp.repeat(log_step, SSM_STATE).reshape(1, N_STATE)
    bre = b_re.transpose(2, 0, 1).reshape(SSM_GROUP, N_STATE)
    bim = b_im.transpose(2, 0, 1).reshape(SSM_GROUP, N_STATE)
    return pl.pallas_call(
        _s5_params_kernel,
        name="s5_params",
        out_shape=(jax.ShapeDtypeStruct((n_rows, N_STATE), F32),
                   jax.ShapeDtypeStruct((n_rows, N_STATE), F32),
                   jax.ShapeDtypeStruct((SSM_GROUP, N_STATE), F32),
                   jax.ShapeDtypeStruct((SSM_GROUP, N_STATE), F32)),
    )(lr, li, ls, exponents, bre, bim)


def _table_layout(re, im):
    r = re.shape[0]
    cat = jnp.concatenate([re.reshape(r, N_BLK, N_GRP, SCAN_W), im.reshape(r, N_BLK, N_GRP, SCAN_W)],
                          axis=-1)
    return cat.transpose(1, 2, 0, 3)


def _pieces(re, im, axis):
    parts = []
    for g in range(N_GRP):
        idx = [slice(None)] * re.ndim
        idx[axis] = slice(g * SCAN_W, (g + 1) * SCAN_W)
        parts += [re[tuple(idx)], im[tuple(idx)]]
    return jnp.stack(parts, axis=1)


def _in_proj_kernel(x_ref, g_ref, w_ref, perm_ref, lng_ref, lnb_ref, sw_ref, sbt_ref, og_ref,
                    c0_ref, c1_ref, c2_ref, c3_ref,
                    zs_ref, yb_ref, c0_out, c1_out, c2_out, c3_out, s_scr):
    c0_out[...] = c0_ref[...].astype(BF16)
    c1_out[...] = c1_ref[...].astype(BF16)
    c2_out[...] = c2_ref[...].astype(BF16)
    c3_out[...] = c3_ref[...].astype(BF16)
    rows = x_ref.shape[0]
    h = _rms(x_ref[...], g_ref[...]).astype(BF16)
    gv = jax.nn.gelu(_dot(h, w_ref[:, D_SSM + D_SGU:]))
    mu = jnp.mean(gv, axis=-1, keepdims=True)
    xc = gv - mu
    r = lax.rsqrt(jnp.mean(xc * xc, axis=-1, keepdims=True) + EPS)
    v = ((xc * r) * lng_ref[...] + lnb_ref[...]).astype(BF16)
    u = jax.nn.gelu(_dot(h, w_ref[:, D_SSM:D_SSM + D_SGU]))
    ti = lax.broadcasted_iota(jnp.int32, (SGU_CHUNK, SGU_CHUNK), 0)
    si = lax.broadcasted_iota(jnp.int32, (SGU_CHUNK, SGU_CHUNK), 1)
    causal = si <= ti
    sbt = sbt_ref[...]
    for hd in range(SGU_HEADS):
        wm = jnp.where(causal, sw_ref[hd], jnp.zeros((), BF16))
        cols = slice(hd * SGU_HEAD_DIM, (hd + 1) * SGU_HEAD_DIM)
        bias = sbt[:, hd:hd + 1]
        for c in range(rows // SGU_CHUNK):
            rs = slice(c * SGU_CHUNK, (c + 1) * SGU_CHUNK)
            s_scr[rs, cols] = _dot(wm, v[rs, cols]) + bias
    yb = u * s_scr[...]
    yb_ref[...] = _rms(yb, og_ref[...]).astype(BF16)
    perm = perm_ref[...]
    hp = jnp.concatenate(
        [_dot(perm, h[c * T_CHUNK:(c + 1) * T_CHUNK]).astype(BF16) for c in range(rows // T_CHUNK)],
        axis=0)
    zs_ref[...] = _dot(hp, w_ref[:, 0:D_SSM])


def _in_proj(x2, g, w_in, perm, ln_g, ln_b, sgu_w, sgu_bt, og, to_cast):
    n = x2.shape[0]
    steps = n // ROWS_IN
    row = lambda i: (i, 0)
    return pl.pallas_call(
        _in_proj_kernel,
        name="in_proj",
        grid=(n // ROWS_IN,),
        in_specs=[
            pl.BlockSpec((ROWS_IN, D_MODEL), row),
            _resident((1, D_MODEL)),
            _resident(w_in.shape),
            _resident(perm.shape),
            _resident((1, D_SGU)),
            _resident((1, D_SGU)),
            _resident(sgu_w.shape),
            _resident(sgu_bt.shape),
            _resident((1, D_SGU)),
        ] + [_cast_block_spec(w, steps, row) for w in to_cast],
        out_specs=(pl.BlockSpec((ROWS_IN, D_SSM), row), pl.BlockSpec((ROWS_IN, D_SGU), row))
        + tuple(_cast_block_spec(w, steps, row) for w in to_cast),
        out_shape=(jax.ShapeDtypeStruct((n, D_SSM), F32), jax.ShapeDtypeStruct((n, D_SGU), BF16))
        + tuple(_cast_out_shape(w) for w in to_cast),
        scratch_shapes=[pltpu.VMEM((ROWS_IN, D_SGU), F32)],
        compiler_params=pltpu.CompilerParams(
            dimension_semantics=("arbitrary",), vmem_limit_bytes=VMEM_LIMIT),
    )(x2, g, w_in, perm, ln_g, ln_b, sgu_w, sgu_bt, og, *to_cast)


def _s5_kernel(zp_ref, bd_ref, cd_ref, tab_ref, d_ref, c0_ref,
               out_ref, c0_out, bu_a, bu_b, sb_a, sb_b, carry_ref):
    for f in range(c0_out.shape[0]):
        c0_out[f] = c0_ref[:, f * FFN_BLK:(f + 1) * FFN_BLK].astype(BF16)

    @pl.when(pl.program_id(1) == 0)
    def _():
        carry_ref[...] = jnp.zeros_like(carry_ref)

    zp = zp_ref[...]
    zpb = zp.astype(BF16)
    bu = (bu_a, bu_b)
    sb = (sb_a, sb_b)
    seg = lax.broadcasted_iota(jnp.int32, (SUBLANES, SCAN_W), 0)
    zero = jnp.zeros((SUBLANES, SCAN_W), F32)
    blk = lambda j: slice(j * BLK_CH, (j + 1) * BLK_CH)

    for p in range(N_PIECE):
        bu[0][p] = _dot(zpb[:, blk(0)], bd_ref[0, p])
    for j in range(N_BLK):
        out_ref[:, blk(j)] = d_ref[:, blk(j)] * zp[:, blk(j)]

    for j in range(N_BLK):
        cur, oth = j % 2, (j + 1) % 2

        def body(g, _, j=j, cur=cur, oth=oth):
            for part in range(2):
                p = 2 * g + part
                if j + 1 < N_BLK:
                    bu[oth][p] = _dot(zpb[:, blk(j + 1)], bd_ref[j + 1, p])
                if j >= 1:
                    out_ref[:, blk(j - 1)] += _dot(sb[oth][p], cd_ref[j - 1, p])

            re_p, im_p = 2 * g, 2 * g + 1
            tre, tim = slice(0, SCAN_W), slice(SCAN_W, 2 * SCAN_W)
            ar = jnp.broadcast_to(tab_ref[j, g, 0:1, tre], (SUBLANES, SCAN_W))
            ai = jnp.broadcast_to(tab_ref[j, g, 0:1, tim], (SUBLANES, SCAN_W))

            def step(row0, sr, si):
                rows = pl.ds(row0, SUBLANES)
                return (ar * sr - ai * si + bu[cur][re_p, rows, :],
                        ar * si + ai * sr + bu[cur][im_p, rows, :])

            def start_states(xr, xi):
                for lvl, m in enumerate((1, 2, 4)):
                    mr = tab_ref[j, g, 1 + lvl:2 + lvl, tre]
                    mi = tab_ref[j, g, 1 + lvl:2 + lvl, tim]
                    rr = pltpu.roll(xr, m, 0)
                    ri = pltpu.roll(xi, m, 0)
                    keep = seg >= m
                    xr, xi = (xr + jnp.where(keep, mr * rr - mi * ri, 0.0),
                              xi + jnp.where(keep, mr * ri + mi * rr, 0.0))
                cr = carry_ref[j, g, :, tre]
                ci = carry_ref[j, g, :, tim]
                pr = tab_ref[j, g, 4:4 + SUBLANES, tre]
                pi = tab_ref[j, g, 4:4 + SUBLANES, tim]
                xr, xi = xr + pr * cr - pi * ci, xi + pr * ci + pi * cr
                carry_ref[j, g, :, tre] = jnp.broadcast_to(xr[SUBLANES - 1:SUBLANES], (SUBLANES, SCAN_W))
                carry_ref[j, g, :, tim] = jnp.broadcast_to(xi[SUBLANES - 1:SUBLANES], (SUBLANES, SCAN_W))
                return (jnp.where(seg >= 1, pltpu.roll(xr, 1, 0), cr),
                        jnp.where(seg >= 1, pltpu.roll(xi, 1, 0), ci))

            def emit_pair(base, m, sr, si):
                r1, i1 = step(base + 2 * m * SUBLANES, sr, si)
                sr, si = step(base + (2 * m + 1) * SUBLANES, r1, i1)
                rows = pl.ds(base + m * 2 * SUBLANES, 2 * SUBLANES)
                sb[cur][re_p, rows, :] = jnp.concatenate([r1, sr], axis=0).astype(BF16)
                sb[cur][im_p, rows, :] = jnp.concatenate([i1, si], axis=0).astype(BF16)
                return sr, si

            n_chunk = S5_ROWS // T_CHUNK
            state = None
            for c in range(n_chunk + 1):
                xr, xi = zero, zero
                for m in range(K_STEPS // 2):
                    if c < n_chunk:
                        xr, xi = step(c * T_CHUNK + 2 * m * SUBLANES, xr, xi)
                        xr, xi = step(c * T_CHUNK + (2 * m + 1) * SUBLANES, xr, xi)
                    if c >= 1:
                        state = emit_pair((c - 1) * T_CHUNK, m, *state)
                if c < n_chunk:
                    state = start_states(xr, xi)
            return 0

        lax.fori_loop(0, N_GRP, body, 0)

    last = N_BLK - 1
    for p in range(N_PIECE):
        out_ref[:, blk(last)] += _dot(sb[last % 2][p], cd_ref[last, p])


def _s5(zs, bsz, seq, bd, cd, tab, dvec, to_cast):
    steps = seq // S5_ROWS
    row = lambda b, c: (b * steps + c, 0)
    n_fblk = to_cast.shape[1] // FFN_BLK
    cast_rows = to_cast.shape[0] // (bsz * steps)
    return pl.pallas_call(
        _s5_kernel,
        name="s5",
        grid=(bsz, steps),
        in_specs=[
            pl.BlockSpec((S5_ROWS, D_SSM), row),
            _resident(bd.shape), _resident(cd.shape), _resident(tab.shape),
            _resident(dvec.shape),
            _cast_block_spec(to_cast, bsz * steps, row),
        ],
        out_specs=(pl.BlockSpec((S5_ROWS, D_SSM), row),
                   pl.BlockSpec((n_fblk, cast_rows, FFN_BLK), lambda b, c: (0, b * steps + c, 0))),
        out_shape=(jax.ShapeDtypeStruct((bsz * seq, D_SSM), F32),
                   jax.ShapeDtypeStruct((n_fblk, to_cast.shape[0], FFN_BLK), BF16)),
        scratch_shapes=[
            pltpu.VMEM((N_PIECE, S5_ROWS, SCAN_W), F32),
            pltpu.VMEM((N_PIECE, S5_ROWS, SCAN_W), F32),
            pltpu.VMEM((N_PIECE, S5_ROWS, SCAN_W), BF16),
            pltpu.VMEM((N_PIECE, S5_ROWS, SCAN_W), BF16),
            pltpu.VMEM((N_BLK, N_GRP, SUBLANES, 2 * SCAN_W), F32),
        ],
        compiler_params=pltpu.CompilerParams(
            dimension_semantics=("arbitrary", "arbitrary"), vmem_limit_bytes=VMEM_LIMIT),
    )(zs, bd, cd, tab, dvec, to_cast)


def _out_proj_kernel(x_ref, y_ref, yb_ref, w_ref, g_ref, gluw_ref, glub_ref, og_ref, permt_ref,
                     x1_ref, h_ref):
    half = D_MODEL // 2
    yb = yb_ref[...]
    y = jax.nn.gelu(y_ref[...])
    x1_lo = x_ref[:, 0:half] + _dot(yb, w_ref[D_SSM:, 0:half])
    gate = jax.nn.sigmoid(_dot(y.astype(BF16), gluw_ref[...]) + glub_ref[...])
    ya = _rms(y * gate, og_ref[...]).astype(BF16)
    x1_hi = x_ref[:, half:] + _dot(yb, w_ref[D_SSM:, half:])
    permt = permt_ref[...]
    ya = jnp.concatenate(
        [_dot(permt, ya[c * T_CHUNK:(c + 1) * T_CHUNK]).astype(BF16)
         for c in range(x_ref.shape[0] // T_CHUNK)], axis=0)
    x1 = jnp.concatenate([x1_lo, x1_hi], axis=1) + _dot(ya, w_ref[0:D_SSM, :])
    x1_ref[...] = x1
    h_ref[...] = _rms(x1, g_ref[...]).astype(BF16)


def _out_proj(x2, y, yb, w_out, g, glu_w, glu_b, og, permt):
    n = x2.shape[0]
    row = lambda i: (i, 0)
    return pl.pallas_call(
        _out_proj_kernel,
        name="out_proj",
        grid=(n // ROWS_IN,),
        in_specs=[
            pl.BlockSpec((ROWS_IN, D_MODEL), row),
            pl.BlockSpec((ROWS_IN, D_SSM), row),
            pl.BlockSpec((ROWS_IN, D_SGU), row),
            _resident(w_out.shape),
            _resident((1, D_MODEL)),
            _resident(glu_w.shape), _resident(glu_b.shape), _resident(og.shape),
            _resident(permt.shape),
        ],
        out_specs=(pl.BlockSpec((ROWS_IN, D_MODEL), row), pl.BlockSpec((ROWS_IN, D_MODEL), row)),
        out_shape=(jax.ShapeDtypeStruct((n, D_MODEL), F32), jax.ShapeDtypeStruct((n, D_MODEL), BF16)),
        compiler_params=pltpu.CompilerParams(
            dimension_semantics=("arbitrary",), vmem_limit_bytes=VMEM_LIMIT),
    )(x2, y, yb, w_out, g, glu_w, glu_b, og, permt)


def _ffn_kernel(h_ref, x1_ref, wg_ref, wu_ref, wo_ref, out_ref):
    @pl.when(pl.program_id(1) == 0)
    def _():
        out_ref[...] = x1_ref[...]

    h = h_ref[...]
    act = jax.nn.silu(_dot(h, wg_ref[...])) * _dot(h, wu_ref[...])
    out_ref[...] += _dot(act.astype(BF16), wo_ref[...])


def _ffn(h, x1, w_in, w_out):
    n = h.shape[0]
    nf = D_FFN // FFN_BLK
    return pl.pallas_call(
        _ffn_kernel,
        name="ffn",
        grid=(n // ROWS_FFN, nf),
        in_specs=[
            pl.BlockSpec((ROWS_FFN, D_MODEL), lambda i, f: (i, 0)),
            pl.BlockSpec((ROWS_FFN, D_MODEL), lambda i, f: (i, 0)),
            pl.BlockSpec((None, D_MODEL, FFN_BLK), lambda i, f: (f, 0, 0)),
            pl.BlockSpec((None, D_MODEL, FFN_BLK), lambda i, f: (f + nf, 0, 0)),
            pl.BlockSpec((FFN_BLK, D_MODEL), lambda i, f: (f, 0)),
        ],
        out_specs=pl.BlockSpec((ROWS_FFN, D_MODEL), lambda i, f: (i, 0)),
        out_shape=jax.ShapeDtypeStruct((n, D_MODEL), F32),
        compiler_params=pltpu.CompilerParams(
            dimension_semantics=("arbitrary", "arbitrary"), vmem_limit_bytes=VMEM_LIMIT_FFN),
    )(h, x1, w_in, w_in, w_out)


def _ple_kernel(x_ref, p_ref, wg_ref, bg_ref, wp_ref, g_ref, gf_ref, out_ref, *, final):
    x = x_ref[...]
    h = _rms(x, g_ref[...]).astype(BF16)
    pe = _dot(p_ref[...].astype(BF16), wp_ref[...])
    gate = jax.nn.sigmoid(_dot(h, wg_ref[...]) + bg_ref[...])
    x3 = x + gate * pe
    out_ref[...] = _rms(x3, gf_ref[...]) if final else x3


def _ple(x2, p2, w_gate, b_gate, w_proj, g, gf, final):
    n = x2.shape[0]
    row = lambda i: (i, 0)
    return pl.pallas_call(
        functools.partial(_ple_kernel, final=final),
        name="ple",
        grid=(n // ROWS_IN,),
        in_specs=[
            pl.BlockSpec((ROWS_IN, D_MODEL), row),
            pl.BlockSpec((ROWS_IN, PLE_DIM), row),
            _resident(w_gate.shape),
            _resident((1, D_MODEL)),
            _resident(w_proj.shape),
            _resident((1, D_MODEL)),
            _resident((1, D_MODEL)),
        ],
        out_specs=pl.BlockSpec((ROWS_IN, D_MODEL), row),
        out_shape=jax.ShapeDtypeStruct((n, D_MODEL), F32),
        compiler_params=pltpu.CompilerParams(
            dimension_semantics=("arbitrary",), vmem_limit_bytes=VMEM_LIMIT),
    )(x2, p2, w_gate, b_gate, w_proj, g, gf)


def _segment_permutation():
    r = np.arange(T_CHUNK)
    src = (r % SUBLANES) * K_STEPS + r // SUBLANES
    perm = np.zeros((T_CHUNK, T_CHUNK), np.float32)
    perm[r, src] = 1.0
    return perm


def _scan_exponents():
    n = [1.0]
    n += [float(K_STEPS * m) for m in (1, 2, 4)]
    n += [float(K_STEPS * (i + 1)) for i in range(SUBLANES)]
    return np.asarray(n, np.float32).reshape(-1, 1)


def _block_diag_b(t):
    t = jnp.broadcast_to(t.reshape(SSM_GROUP, N_BLK, 1, BLK_ST),
                         (SSM_GROUP, N_BLK, SSM_GROUP, BLK_ST)).transpose(1, 2, 0, 3)
    same = (np.arange(SSM_GROUP)[:, None, None] == (np.arange(BLK_ST) // SSM_STATE)[None, None, :])
    return jnp.where(same, t, 0.0).reshape(N_BLK, BLK_CH, BLK_ST)


def _block_diag_c(t):
    t = t.reshape(N_BLK, SSM_GROUP, SSM_GROUP, SSM_STATE).transpose(0, 1, 3, 2)
    t = jnp.broadcast_to(t.reshape(N_BLK, BLK_ST, 1, SSM_GROUP), (N_BLK, BLK_ST, SSM_GROUP, SSM_GROUP))
    same = ((np.arange(BLK_ST) // SSM_STATE)[:, None, None] == np.arange(SSM_GROUP)[None, :, None])
    return jnp.where(same, t, 0.0).reshape(N_BLK, BLK_ST, BLK_CH)


def kernel(x, p, norm_mix_g, w_in, ssm_lambda_re, ssm_lambda_im, ssm_log_step, ssm_b_re, ssm_b_im, ssm_c_re, ssm_c_im, ssm_d, ssm_glu_w, ssm_glu_b, sgu_ln_g, sgu_ln_b, sgu_w, sgu_b, out_norm_ssm_g, out_norm_sgu_g, w_out, norm_ffn_g, w_ffn_in, w_ffn_out, norm_ple_g, w_ple_gate, b_ple_gate, w_ple_proj, final_norm_g):
    bsz, seq, _ = x.shape
    depth = w_in.shape[0]
    perm_np = _segment_permutation()
    perm = jnp.asarray(perm_np, BF16)
    permt = jnp.asarray(perm_np.T, BF16)
    exponents = jnp.asarray(_scan_exponents())
    x2 = x.reshape(bsz * seq, D_MODEL)
    vec = lambda a: a.reshape(1, -1)
    for i in range(depth):
        pw_re, pw_im, bb_re, bb_im = _s5_params(
            ssm_lambda_re[i], ssm_lambda_im[i], ssm_log_step[i], ssm_b_re[i], ssm_b_im[i], exponents)
        tab = _table_layout(pw_re, pw_im)
        bd = _pieces(_block_diag_b(bb_re), _block_diag_b(bb_im), axis=2).astype(BF16)
        cd = _pieces(_block_diag_c(ssm_c_re[i]), _block_diag_c(-ssm_c_im[i]), axis=1).astype(BF16)

        zs, yb, w_out_bf, w_gate_bf, glu_bf, w_ffn_out_bf = _in_proj(
            x2, vec(norm_mix_g[i]), w_in[i].astype(BF16), perm, vec(sgu_ln_g[i]), vec(sgu_ln_b[i]),
            sgu_w[i].astype(BF16), sgu_b[i].T, vec(out_norm_sgu_g[i]),
            to_cast=(w_out[i], w_ple_gate[i], ssm_glu_w[i], w_ffn_out[i]))
        y, w_ffn_in_bf = _s5(zs, bsz, seq, bd, cd, tab, vec(ssm_d[i]), to_cast=w_ffn_in[i])
        x1, h = _out_proj(x2, y, yb, w_out_bf, vec(norm_ffn_g[i]), glu_bf, vec(ssm_glu_b[i]),
                          vec(out_norm_ssm_g[i]), permt)
        x2 = _ffn(h, x1, w_ffn_in_bf, w_ffn_out_bf)
        x2 = _ple(x2, p[i].reshape(bsz * seq, PLE_DIM), w_gate_bf, vec(b_ple_gate[i]),
                  w_ple_proj[i].astype(BF16), vec(norm_ple_g[i]), vec(final_norm_g),
                  final=(i == depth - 1))
    return x2.reshape(bsz, seq, D_MODEL)
```

```python
import functools

import numpy as np
import jax
import jax.numpy as jnp
from jax import lax
from jax.experimental import pallas as pl
from jax.experimental.pallas import tpu as pltpu

D_MODEL = 2048
D_SSM = 1024
D_SGU = 1024
SSM_GROUP = 16
SSM_GROUPS = 64
SSM_STATE = 64
SGU_CHUNK = 128
SGU_HEADS = 8
SGU_HEAD_DIM = D_SGU // SGU_HEADS
D_FFN = 5632
PLE_DIM = 256
EPS = 1e-6
LAMBDA_RE_MAX = -1e-4

SUBLANES = 8
N_STATE = SSM_GROUPS * SSM_STATE
N_BLK = 4
BLK_CH = D_SSM // N_BLK
BLK_ST = N_STATE // N_BLK
BLK_W = 2 * BLK_ST
T_CHUNK = 256
K_STEPS = T_CHUNK // SUBLANES
S5_ROWS = 512
SCAN_W = 512
N_GRP = BLK_ST // SCAN_W
N_PIECE = 2 * N_GRP

ROWS_IN = 512
ROWS_FFN = 1024
FFN_BLK = 512
VMEM_LIMIT = 56 * 1024 * 1024
VMEM_LIMIT_FFN = 60 * 1024 * 1024

F32 = jnp.float32
BF16 = jnp.bfloat16


def _rms(x, g):
    r = lax.rsqrt(jnp.mean(x * x, axis=-1, keepdims=True) + EPS)
    return (x * r) * g


def _dot(a, b):
    return jnp.dot(a, b, preferred_element_type=F32)


def _resident(shape):
    nd = len(shape)
    return pl.BlockSpec(shape, lambda *_: (0,) * nd, pipeline_mode=pl.Buffered(1))


def _cast_block_spec(w, steps, index):
    return pl.BlockSpec((w.shape[0] // steps, w.shape[1]), index)


def _cast_out_shape(w):
    return jax.ShapeDtypeStruct(w.shape, BF16)


def _s5_params_kernel(lr_ref, li_ref, ls_ref, n_ref, bre_ref, bim_ref,
                      pwr_ref, pwi_ref, bbr_ref, bbi_ref):
    lr = jnp.minimum(lr_ref[...], LAMBDA_RE_MAX)
    li = li_ref[...]
    dt = jnp.exp(ls_ref[...])
    n = n_ref[...]
    mag = jnp.exp(n * (lr * dt))
    ang = n * (li * dt)
    pwr_ref[...] = mag * jnp.cos(ang)
    pwi_ref[...] = mag * jnp.sin(ang)
    mag1 = jnp.exp(lr * dt)
    ang1 = li * dt
    nr = mag1 * jnp.cos(ang1) - 1.0
    ni = mag1 * jnp.sin(ang1)
    den = lr * lr + li * li
    q_re = (nr * lr + ni * li) / den
    q_im = (ni * lr - nr * li) / den
    bre = bre_ref[...]
    bim = bim_ref[...]
    bbr_ref[...] = q_re * bre - q_im * bim
    bbi_ref[...] = q_re * bim + q_im * bre


def _s5_params(lam_re, lam_im, log_step, b_re, b_im, exponents):
    n_rows = exponents.shape[0]
    lr = lam_re.reshape(1, N_STATE)
    li = lam_im.reshape(1, N_STATE)
    ls = jnp.repeat(log_step, SSM_STATE).reshape(1, N_STATE)
    bre = b_re.transpose(2, 0, 1).reshape(SSM_GROUP, N_STATE)
    bim = b_im.transpose(2, 0, 1).reshape(SSM_GROUP, N_STATE)
    return pl.pallas_call(
        _s5_params_kernel,
        name="s5_params",
        out_shape=(jax.ShapeDtypeStruct((n_rows, N_STATE), F32),
                   jax.ShapeDtypeStruct((n_rows, N_STATE), F32),
                   jax.ShapeDtypeStruct((SSM_GROUP, N_STATE), F32),
                   jax.ShapeDtypeStruct((SSM_GROUP, N_STATE), F32)),
    )(lr, li, ls, exponents, bre, bim)


def _table_layout(re, im):
    r = re.shape[0]
    cat = jnp.concatenate([re.reshape(r, N_BLK, N_GRP, SCAN_W), im.reshape(r, N_BLK, N_GRP, SCAN_W)],
                          axis=-1)
    return cat.transpose(1, 2, 0, 3)


def _pieces(re, im, axis):
    parts = []
    for g in range(N_GRP):
        idx = [slice(None)] * re.ndim
        idx[axis] = slice(g * SCAN_W, (g + 1) * SCAN_W)
        parts += [re[tuple(idx)], im[tuple(idx)]]
    return jnp.stack(parts, axis=1)


def _in_proj_kernel(x_ref, g_ref, w_ref, perm_ref, lng_ref, lnb_ref, sw_ref, sbt_ref, og_ref,
                    c0_ref, c1_ref, c2_ref, c3_ref,
                    zs_ref, yb_ref, c0_out, c1_out, c2_out, c3_out, s_scr):
    c0_out[...] = c0_ref[...].astype(BF16)
    c1_out[...] = c1_ref[...].astype(BF16)
    c2_out[...] = c2_ref[...].astype(BF16)
    c3_out[...] = c3_ref[...].astype(BF16)
    rows = x_ref.shape[0]
    h = _rms(x_ref[...], g_ref[...]).astype(BF16)
    gv = jax.nn.gelu(_dot(h, w_ref[:, D_SSM + D_SGU:]))
    mu = jnp.mean(gv, axis=-1, keepdims=True)
    xc = gv - mu
    r = lax.rsqrt(jnp.mean(xc * xc, axis=-1, keepdims=True) + EPS)
    v = ((xc * r) * lng_ref[...] + lnb_ref[...]).astype(BF16)
    u = jax.nn.gelu(_dot(h, w_ref[:, D_SSM:D_SSM + D_SGU]))
    ti = lax.broadcasted_iota(jnp.int32, (SGU_CHUNK, SGU_CHUNK), 0)
    si = lax.broadcasted_iota(jnp.int32, (SGU_CHUNK, SGU_CHUNK), 1)
    causal = si <= ti
    sbt = sbt_ref[...]
    for hd in range(SGU_HEADS):
        wm = jnp.where(causal, sw_ref[hd], jnp.zeros((), BF16))
        cols = slice(hd * SGU_HEAD_DIM, (hd + 1) * SGU_HEAD_DIM)
        bias = sbt[:, hd:hd + 1]
        for c in range(rows // SGU_CHUNK):
            rs = slice(c * SGU_CHUNK, (c + 1) * SGU_CHUNK)
            s_scr[rs, cols] = _dot(wm, v[rs, cols]) + bias
    yb = u * s_scr[...]
    yb_ref[...] = _rms(yb, og_ref[...]).astype(BF16)
    perm = perm_ref[...]
    hp = jnp.concatenate(
        [_dot(perm, h[c * T_CHUNK:(c + 1) * T_CHUNK]).astype(BF16) for c in range(rows // T_CHUNK)],
        axis=0)
    zs_ref[...] = _dot(hp, w_ref[:, 0:D_SSM])


def _in_proj(x2, g, w_in, perm, ln_g, ln_b, sgu_w, sgu_bt, og, to_cast):
    n = x2.shape[0]
    steps = n // ROWS_IN
    row = lambda i: (i, 0)
    return pl.pallas_call(
        _in_proj_kernel,
        name="in_proj",
        grid=(n // ROWS_IN,),
        in_specs=[
            pl.BlockSpec((ROWS_IN, D_MODEL), row),
            _resident((1, D_MODEL)),
            _resident(w_in.shape),
            _resident(perm.shape),
            _resident((1, D_SGU)),
            _resident((1, D_SGU)),
            _resident(sgu_w.shape),
            _resident(sgu_bt.shape),
            _resident((1, D_SGU)),
        ] + [_cast_block_spec(w, steps, row) for w in to_cast],
        out_specs=(pl.BlockSpec((ROWS_IN, D_SSM), row), pl.BlockSpec((ROWS_IN, D_SGU), row))
        + tuple(_cast_block_spec(w, steps, row) for w in to_cast),
        out_shape=(jax.ShapeDtypeStruct((n, D_SSM), F32), jax.ShapeDtypeStruct((n, D_SGU), BF16))
        + tuple(_cast_out_shape(w) for w in to_cast),
        scratch_shapes=[pltpu.VMEM((ROWS_IN, D_SGU), F32)],
        compiler_params=pltpu.CompilerParams(
            dimension_semantics=("arbitrary",), vmem_limit_bytes=VMEM_LIMIT),
    )(x2, g, w_in, perm, ln_g, ln_b, sgu_w, sgu_bt, og, *to_cast)


def _s5_kernel(zp_ref, zn_ref, bd_ref, cd_ref, tab_ref, d_ref, c0_ref,
               out_ref, c0_out, bu_a, bu_b, sb_a, sb_b, carry_ref):
    c0_out[...] = c0_ref[...].astype(BF16)

    @pl.when(pl.program_id(1) == 0)
    def _():
        carry_ref[...] = jnp.zeros_like(carry_ref)

    zp = zp_ref[...]
    zpb = zp.astype(BF16)
    znb = zn_ref[...].astype(BF16)
    bu = (bu_a, bu_b)
    sb = (sb_a, sb_b)
    seg = lax.broadcasted_iota(jnp.int32, (SUBLANES, SCAN_W), 0)
    zero = jnp.zeros((SUBLANES, SCAN_W), F32)
    blk = lambda j: slice(j * BLK_CH, (j + 1) * BLK_CH)

    @pl.when((pl.program_id(0) == 0) & (pl.program_id(1) == 0))
    def _():
        for p in range(N_PIECE):
            bu[0][p] = _dot(zpb[:, blk(0)], bd_ref[0, p])

    for j in range(N_BLK):
        out_ref[:, blk(j)] = d_ref[:, blk(j)] * zp[:, blk(j)]

    for j in range(N_BLK):
        cur, oth = j % 2, (j + 1) % 2

        def body(g, _, j=j, cur=cur, oth=oth):
            for part in range(2):
                p = 2 * g + part
                if j + 1 < N_BLK:
                    bu[oth][p] = _dot(zpb[:, blk(j + 1)], bd_ref[j + 1, p])
                else:
                    bu[oth][p] = _dot(znb, bd_ref[0, p])
                if j >= 1:
                    out_ref[:, blk(j - 1)] += _dot(sb[oth][p], cd_ref[j - 1, p])

            re_p, im_p = 2 * g, 2 * g + 1
            tre, tim = slice(0, SCAN_W), slice(SCAN_W, 2 * SCAN_W)
            ar = jnp.broadcast_to(tab_ref[j, g, 0:1, tre], (SUBLANES, SCAN_W))
            ai = jnp.broadcast_to(tab_ref[j, g, 0:1, tim], (SUBLANES, SCAN_W))

            def step(row0, sr, si):
                rows = pl.ds(row0, SUBLANES)
                return (ar * sr - ai * si + bu[cur][re_p, rows, :],
                        ar * si + ai * sr + bu[cur][im_p, rows, :])

            def start_states(xr, xi):
                for lvl, m in enumerate((1, 2, 4)):
                    mr = tab_ref[j, g, 1 + lvl:2 + lvl, tre]
                    mi = tab_ref[j, g, 1 + lvl:2 + lvl, tim]
                    rr = pltpu.roll(xr, m, 0)
                    ri = pltpu.roll(xi, m, 0)
                    keep = seg >= m
                    xr, xi = (xr + jnp.where(keep, mr * rr - mi * ri, 0.0),
                              xi + jnp.where(keep, mr * ri + mi * rr, 0.0))
                cr = carry_ref[j, g, :, tre]
                ci = carry_ref[j, g, :, tim]
                pr = tab_ref[j, g, 4:4 + SUBLANES, tre]
                pi = tab_ref[j, g, 4:4 + SUBLANES, tim]
                xr, xi = xr + pr * cr - pi * ci, xi + pr * ci + pi * cr
                carry_ref[j, g, :, tre] = jnp.broadcast_to(xr[SUBLANES - 1:SUBLANES], (SUBLANES, SCAN_W))
                carry_ref[j, g, :, tim] = jnp.broadcast_to(xi[SUBLANES - 1:SUBLANES], (SUBLANES, SCAN_W))
                return (jnp.where(seg >= 1, pltpu.roll(xr, 1, 0), cr),
                        jnp.where(seg >= 1, pltpu.roll(xi, 1, 0), ci))

            def emit_pair(base, m, sr, si):
                r1, i1 = step(base + 2 * m * SUBLANES, sr, si)
                sr, si = step(base + (2 * m + 1) * SUBLANES, r1, i1)
                rows = pl.ds(base + m * 2 * SUBLANES, 2 * SUBLANES)
                sb[cur][re_p, rows, :] = jnp.concatenate([r1, sr], axis=0).astype(BF16)
                sb[cur][im_p, rows, :] = jnp.concatenate([i1, si], axis=0).astype(BF16)
                return sr, si

            n_chunk = S5_ROWS // T_CHUNK
            state = None
            for c in range(n_chunk + 1):
                xr, xi = zero, zero
                for m in range(K_STEPS // 2):
                    if c < n_chunk:
                        xr, xi = step(c * T_CHUNK + 2 * m * SUBLANES, xr, xi)
                        xr, xi = step(c * T_CHUNK + (2 * m + 1) * SUBLANES, xr, xi)
                    if c >= 1:
                        state = emit_pair((c - 1) * T_CHUNK, m, *state)
                if c < n_chunk:
                    state = start_states(xr, xi)
            return 0

        lax.fori_loop(0, N_GRP, body, 0)

    last = N_BLK - 1
    for p in range(N_PIECE):
        out_ref[:, blk(last)] += _dot(sb[last % 2][p], cd_ref[last, p])


def _s5(zs, bsz, seq, bd, cd, tab, dvec, to_cast):
    steps = seq // S5_ROWS
    row = lambda b, c: (b * steps + c, 0)
    nxt = lambda b, c: (jnp.minimum(b * steps + c + 1, bsz * steps - 1), 0)
    return pl.pallas_call(
        _s5_kernel,
        name="s5",
        grid=(bsz, steps),
        in_specs=[
            pl.BlockSpec((S5_ROWS, D_SSM), row),
            pl.BlockSpec((S5_ROWS, BLK_CH), nxt),
            _resident(bd.shape), _resident(cd.shape), _resident(tab.shape),
            _resident(dvec.shape),
            _cast_block_spec(to_cast, bsz * steps, row),
        ],
        out_specs=(pl.BlockSpec((S5_ROWS, D_SSM), row), _cast_block_spec(to_cast, bsz * steps, row)),
        out_shape=(jax.ShapeDtypeStruct((bsz * seq, D_SSM), F32), _cast_out_shape(to_cast)),
        scratch_shapes=[
            pltpu.VMEM((N_PIECE, S5_ROWS, SCAN_W), F32),
            pltpu.VMEM((N_PIECE, S5_ROWS, SCAN_W), F32),
            pltpu.VMEM((N_PIECE, S5_ROWS, SCAN_W), BF16),
            pltpu.VMEM((N_PIECE, S5_ROWS, SCAN_W), BF16),
            pltpu.VMEM((N_BLK, N_GRP, SUBLANES, 2 * SCAN_W), F32),
        ],
        compiler_params=pltpu.CompilerParams(
            dimension_semantics=("arbitrary", "arbitrary"), vmem_limit_bytes=VMEM_LIMIT),
    )(zs, zs, bd, cd, tab, dvec, to_cast)


def _out_proj_kernel(x_ref, y_ref, yb_ref, w_ref, g_ref, gluw_ref, glub_ref, og_ref, permt_ref,
                     x1_ref, h_ref):
    half = D_MODEL // 2
    yb = yb_ref[...]
    y = jax.nn.gelu(y_ref[...])
    x1_lo = x_ref[:, 0:half] + _dot(yb, w_ref[D_SSM:, 0:half])
    gate = jax.nn.sigmoid(_dot(y.astype(BF16), gluw_ref[...]) + glub_ref[...])
    ya = _rms(y * gate, og_ref[...]).astype(BF16)
    x1_hi = x_ref[:, half:] + _dot(yb, w_ref[D_SSM:, half:])
    permt = permt_ref[...]
    ya = jnp.concatenate(
        [_dot(permt, ya[c * T_CHUNK:(c + 1) * T_CHUNK]).astype(BF16)
         for c in range(x_ref.shape[0] // T_CHUNK)], axis=0)
    x1 = jnp.concatenate([x1_lo, x1_hi], axis=1) + _dot(ya, w_ref[0:D_SSM, :])
    x1_ref[...] = x1
    h_ref[...] = _rms(x1, g_ref[...]).astype(BF16)


def _out_proj(x2, y, yb, w_out, g, glu_w, glu_b, og, permt):
    n = x2.shape[0]
    row = lambda i: (i, 0)
    return pl.pallas_call(
        _out_proj_kernel,
        name="out_proj",
        grid=(n // ROWS_IN,),
        in_specs=[
            pl.BlockSpec((ROWS_IN, D_MODEL), row),
            pl.BlockSpec((ROWS_IN, D_SSM), row),
            pl.BlockSpec((ROWS_IN, D_SGU), row),
            _resident(w_out.shape),
            _resident((1, D_MODEL)),
            _resident(glu_w.shape), _resident(glu_b.shape), _resident(og.shape),
            _resident(permt.shape),
        ],
        out_specs=(pl.BlockSpec((ROWS_IN, D_MODEL), row), pl.BlockSpec((ROWS_IN, D_MODEL), row)),
        out_shape=(jax.ShapeDtypeStruct((n, D_MODEL), F32), jax.ShapeDtypeStruct((n, D_MODEL), BF16)),
        compiler_params=pltpu.CompilerParams(
            dimension_semantics=("arbitrary",), vmem_limit_bytes=VMEM_LIMIT),
    )(x2, y, yb, w_out, g, glu_w, glu_b, og, permt)


def _ffn_kernel(h_ref, x1_ref, wg_ref, wu_ref, wo_ref, out_ref):
    @pl.when(pl.program_id(1) == 0)
    def _():
        out_ref[...] = x1_ref[...]

    h = h_ref[...]
    act = jax.nn.silu(_dot(h, wg_ref[...])) * _dot(h, wu_ref[...])
    out_ref[...] += _dot(act.astype(BF16), wo_ref[...])


def _ffn(h, x1, w_in, w_out):
    n = h.shape[0]
    nf = D_FFN // FFN_BLK
    return pl.pallas_call(
        _ffn_kernel,
        name="ffn",
        grid=(n // ROWS_FFN, nf),
        in_specs=[
            pl.BlockSpec((ROWS_FFN, D_MODEL), lambda i, f: (i, 0)),
            pl.BlockSpec((ROWS_FFN, D_MODEL), lambda i, f: (i, 0)),
            pl.BlockSpec((D_MODEL, FFN_BLK), lambda i, f: (0, f)),
            pl.BlockSpec((D_MODEL, FFN_BLK), lambda i, f: (0, f + nf)),
            pl.BlockSpec((FFN_BLK, D_MODEL), lambda i, f: (f, 0)),
        ],
        out_specs=pl.BlockSpec((ROWS_FFN, D_MODEL), lambda i, f: (i, 0)),
        out_shape=jax.ShapeDtypeStruct((n, D_MODEL), F32),
        compiler_params=pltpu.CompilerParams(
            dimension_semantics=("arbitrary", "arbitrary"), vmem_limit_bytes=VMEM_LIMIT_FFN),
    )(h, x1, w_in, w_in, w_out)


def _ple_kernel(x_ref, p_ref, wg_ref, bg_ref, wp_ref, g_ref, gf_ref, out_ref, *, final):
    x = x_ref[...]
    h = _rms(x, g_ref[...]).astype(BF16)
    pe = _dot(p_ref[...].astype(BF16), wp_ref[...])
    gate = jax.nn.sigmoid(_dot(h, wg_ref[...]) + bg_ref[...])
    x3 = x + gate * pe
    out_ref[...] = _rms(x3, gf_ref[...]) if final else x3


def _ple(x2, p2, w_gate, b_gate, w_proj, g, gf, final):
    n = x2.shape[0]
    row = lambda i: (i, 0)
    return pl.pallas_call(
        functools.partial(_ple_kernel, final=final),
        name="ple",
        grid=(n // ROWS_IN,),
        in_specs=[
            pl.BlockSpec((ROWS_IN, D_MODEL), row),
            pl.BlockSpec((ROWS_IN, PLE_DIM), row),
            _resident(w_gate.shape),
            _resident((1, D_MODEL)),
            _resident(w_proj.shape),
            _resident((1, D_MODEL)),
            _resident((1, D_MODEL)),
        ],
        out_specs=pl.BlockSpec((ROWS_IN, D_MODEL), row),
        out_shape=jax.ShapeDtypeStruct((n, D_MODEL), F32),
        compiler_params=pltpu.CompilerParams(
            dimension_semantics=("arbitrary",), vmem_limit_bytes=VMEM_LIMIT),
    )(x2, p2, w_gate, b_gate, w_proj, g, gf)


def _segment_permutation():
    r = np.arange(T_CHUNK)
    src = (r % SUBLANES) * K_STEPS + r // SUBLANES
    perm = np.zeros((T_CHUNK, T_CHUNK), np.float32)
    perm[r, src] = 1.0
    return perm


def _scan_exponents():
    n = [1.0]
    n += [float(K_STEPS * m) for m in (1, 2, 4)]
    n += [float(K_STEPS * (i + 1)) for i in range(SUBLANES)]
    return np.asarray(n, np.float32).reshape(-1, 1)


def _block_diag_b(t):
    t = jnp.broadcast_to(t.reshape(SSM_GROUP, N_BLK, 1, BLK_ST),
                         (SSM_GROUP, N_BLK, SSM_GROUP, BLK_ST)).transpose(1, 2, 0, 3)
    same = (np.arange(SSM_GROUP)[:, None, None] == (np.arange(BLK_ST) // SSM_STATE)[None, None, :])
    return jnp.where(same, t, 0.0).reshape(N_BLK, BLK_CH, BLK_ST)


def _block_diag_c(t):
    t = t.reshape(N_BLK, SSM_GROUP, SSM_GROUP, SSM_STATE).transpose(0, 1, 3, 2)
    t = jnp.broadcast_to(t.reshape(N_BLK, BLK_ST, 1, SSM_GROUP), (N_BLK, BLK_ST, SSM_GROUP, SSM_GROUP))
    same = ((np.arange(BLK_ST) // SSM_STATE)[:, None, None] == np.arange(SSM_GROUP)[None, :, None])
    return jnp.where(same, t, 0.0).reshape(N_BLK, BLK_ST, BLK_CH)


def kernel(x, p, norm_mix_g, w_in, ssm_lambda_re, ssm_lambda_im, ssm_log_step, ssm_b_re, ssm_b_im, ssm_c_re, ssm_c_im, ssm_d, ssm_glu_w, ssm_glu_b, sgu_ln_g, sgu_ln_b, sgu_w, sgu_b, out_norm_ssm_g, out_norm_sgu_g, w_out, norm_ffn_g, w_ffn_in, w_ffn_out, norm_ple_g, w_ple_gate, b_ple_gate, w_ple_proj, final_norm_g):
    bsz, seq, _ = x.shape
    depth = w_in.shape[0]
    perm_np = _segment_permutation()
    perm = jnp.asarray(perm_np, BF16)
    permt = jnp.asarray(perm_np.T, BF16)
    exponents = jnp.asarray(_scan_exponents())
    x2 = x.reshape(bsz * seq, D_MODEL)
    vec = lambda a: a.reshape(1, -1)
    for i in range(depth):
        pw_re, pw_im, bb_re, bb_im = _s5_params(
            ssm_lambda_re[i], ssm_lambda_im[i], ssm_log_step[i], ssm_b_re[i], ssm_b_im[i], exponents)
        tab = _table_layout(pw_re, pw_im)
        bd = _pieces(_block_diag_b(bb_re), _block_diag_b(bb_im), axis=2).astype(BF16)
        cd = _pieces(_block_diag_c(ssm_c_re[i]), _block_diag_c(-ssm_c_im[i]), axis=1).astype(BF16)

        zs, yb, w_out_bf, w_gate_bf, glu_bf, w_ffn_out_bf = _in_proj(
            x2, vec(norm_mix_g[i]), w_in[i].astype(BF16), perm, vec(sgu_ln_g[i]), vec(sgu_ln_b[i]),
            sgu_w[i].astype(BF16), sgu_b[i].T, vec(out_norm_sgu_g[i]),
            to_cast=(w_out[i], w_ple_gate[i], ssm_glu_w[i], w_ffn_out[i]))
        y, w_ffn_in_bf = _s5(zs, bsz, seq, bd, cd, tab, vec(ssm_d[i]), to_cast=w_ffn_in[i])
        x1, h = _out_proj(x2, y, yb, w_out_bf, vec(norm_ffn_g[i]), glu_bf, vec(ssm_glu_b[i]),
                          vec(out_norm_ssm_g[i]), permt)
        x2 = _ffn(h, x1, w_ffn_in_bf, w_ffn_out_bf)
        x2 = _ple(x2, p[i].reshape(bsz * seq, PLE_DIM), w_gate_bf, vec(b_ple_gate[i]),
                  w_ple_proj[i].astype(BF16), vec(norm_ple_g[i]), vec(final_norm_g),
                  final=(i == depth - 1))
    return x2.reshape(bsz, seq, D_MODEL)
```

```python
import functools

import numpy as np
import jax
import jax.numpy as jnp
from jax import lax
from jax.experimental import pallas as pl
from jax.experimental.pallas import tpu as pltpu

D_MODEL = 2048
D_SSM = 1024
D_SGU = 1024
SSM_GROUP = 16
SSM_GROUPS = 64
SSM_STATE = 64
SGU_CHUNK = 128
SGU_HEADS = 8
SGU_HEAD_DIM = D_SGU // SGU_HEADS
D_FFN = 5632
PLE_DIM = 256
EPS = 1e-6
LAMBDA_RE_MAX = -1e-4

SUBLANES = 8
N_STATE = SSM_GROUPS * SSM_STATE
N_BLK = 4
BLK_CH = D_SSM // N_BLK
BLK_ST = N_STATE // N_BLK
T_CHUNK = 256
K_STEPS = T_CHUNK // SUBLANES
S5_ROWS = 512
SCAN_W = 512
N_GRP = BLK_ST // SCAN_W
N_PIECE = 2 * N_GRP

ROWS_IN = 512
ROWS_FFN = 1024
FFN_BLK = 512
VMEM_LIMIT = 56 * 1024 * 1024
VMEM_LIMIT_FFN = 60 * 1024 * 1024

F32 = jnp.float32
BF16 = jnp.bfloat16


def _rms(x, g):
    r = lax.rsqrt(jnp.mean(x * x, axis=-1, keepdims=True) + EPS)
    return (x * r) * g


def _dot(a, b):
    return jnp.dot(a, b, preferred_element_type=F32)


def _resident(shape):
    nd = len(shape)
    return pl.BlockSpec(shape, lambda *_: (0,) * nd, pipeline_mode=pl.Buffered(1))


def _cast_block_spec(w, steps, index):
    return pl.BlockSpec((w.shape[0] // steps, w.shape[1]), index)


def _cast_out_shape(w):
    return jax.ShapeDtypeStruct(w.shape, BF16)


def _s5_params_kernel(lr_ref, li_ref, ls_ref, n_ref, bre_ref, bim_ref, cre_ref, cim_ref,
                      tab_ref, bd_ref, cd_ref):
    lr = jnp.minimum(lr_ref[...], LAMBDA_RE_MAX)
    li = li_ref[...]
    dt = jnp.exp(ls_ref[...])
    n = n_ref[...]
    mag = jnp.exp(n * (lr * dt))
    ang = n * (li * dt)
    pwr = mag * jnp.cos(ang)
    pwi = mag * jnp.sin(ang)
    mag1 = jnp.exp(lr * dt)
    ang1 = li * dt
    nr = mag1 * jnp.cos(ang1) - 1.0
    ni = mag1 * jnp.sin(ang1)
    den = lr * lr + li * li
    q_re = (nr * lr + ni * li) / den
    q_im = (ni * lr - nr * li) / den
    bre = bre_ref[...]
    bim = bim_ref[...]
    bbar = (q_re * bre - q_im * bim, q_re * bim + q_im * bre)

    iota = lambda shape, dim: lax.broadcasted_iota(jnp.int32, shape, dim)
    grp_of_ch = lambda i: lax.shift_right_logical(i, 4)
    grp_of_st = lambda i: lax.shift_right_logical(i, 6)
    b_row_grp = grp_of_ch(iota((BLK_CH, SCAN_W), 0))
    b_col_grp = grp_of_st(iota((BLK_CH, SCAN_W), 1))
    c_row_grp = grp_of_st(iota((SCAN_W, BLK_CH), 0))
    c_col_grp = grp_of_ch(iota((SCAN_W, BLK_CH), 1))
    spread = (iota((SSM_GROUP, BLK_CH), 0)
              == lax.bitwise_and(iota((SSM_GROUP, BLK_CH), 1), SSM_GROUP - 1)).astype(BF16)
    grp_per_half = SCAN_W // SSM_STATE
    for j in range(N_BLK):
        for g in range(N_GRP):
            st = slice((j * N_GRP + g) * SCAN_W, (j * N_GRP + g + 1) * SCAN_W)
            tab_ref[j, g] = jnp.concatenate([pwr[:, st], pwi[:, st]], axis=1)
            same_b = b_row_grp == b_col_grp + g * grp_per_half
            same_c = c_row_grp + g * grp_per_half == c_col_grp
            for part in range(2):
                tiled = jnp.concatenate([bbar[part][:, st]] * SSM_GROUP, axis=0)
                bd_ref[j, 2 * g + part] = jnp.where(same_b, tiled, 0.0).astype(BF16)
                c_rows = (cre_ref, cim_ref)[part][st, :].astype(BF16)
                wide = _dot(c_rows, spread)
                wide = wide if part == 0 else -wide
                cd_ref[j, 2 * g + part] = jnp.where(same_c, wide, 0.0).astype(BF16)


def _s5_params(lam_re, lam_im, log_step, b_re, b_im, c_re, c_im, exponents):
    n_rows = exponents.shape[0]
    lr = lam_re.reshape(1, N_STATE)
    li = lam_im.reshape(1, N_STATE)
    ls = jnp.repeat(log_step, SSM_STATE).reshape(1, N_STATE)
    bre = b_re.transpose(2, 0, 1).reshape(SSM_GROUP, N_STATE)
    bim = b_im.transpose(2, 0, 1).reshape(SSM_GROUP, N_STATE)
    cre = c_re.transpose(0, 2, 1).reshape(N_STATE, SSM_GROUP)
    cim = c_im.transpose(0, 2, 1).reshape(N_STATE, SSM_GROUP)
    return pl.pallas_call(
        _s5_params_kernel,
        name="s5_params",
        out_shape=(jax.ShapeDtypeStruct((N_BLK, N_GRP, n_rows, 2 * SCAN_W), F32),
                   jax.ShapeDtypeStruct((N_BLK, N_PIECE, BLK_CH, SCAN_W), BF16),
                   jax.ShapeDtypeStruct((N_BLK, N_PIECE, SCAN_W, BLK_CH), BF16)),
        compiler_params=pltpu.CompilerParams(vmem_limit_bytes=VMEM_LIMIT),
    )(lr, li, ls, exponents, bre, bim, cre, cim)


def _in_proj_kernel(x_ref, g_ref, w_ref, perm_ref, lng_ref, lnb_ref, sw_ref, sbt_ref, og_ref,
                    c0_ref, c1_ref, c2_ref, c3_ref, c4_ref,
                    zs_ref, yb_ref, c0_out, c1_out, c2_out, c3_out, c4_out, s_scr):
    c0_out[...] = c0_ref[...].astype(BF16)
    c1_out[...] = c1_ref[...].astype(BF16)
    c2_out[...] = c2_ref[...].astype(BF16)
    c3_out[...] = c3_ref[...].astype(BF16)
    c4_out[...] = c4_ref[...].astype(BF16)
    rows = x_ref.shape[0]
    h = _rms(x_ref[...], g_ref[...]).astype(BF16)
    gv = jax.nn.gelu(_dot(h, w_ref[:, D_SSM + D_SGU:]))
    mu = jnp.mean(gv, axis=-1, keepdims=True)
    xc = gv - mu
    r = lax.rsqrt(jnp.mean(xc * xc, axis=-1, keepdims=True) + EPS)
    v = ((xc * r) * lng_ref[...] + lnb_ref[...]).astype(BF16)
    u = jax.nn.gelu(_dot(h, w_ref[:, D_SSM:D_SSM + D_SGU]))
    ti = lax.broadcasted_iota(jnp.int32, (SGU_CHUNK, SGU_CHUNK), 0)
    si = lax.broadcasted_iota(jnp.int32, (SGU_CHUNK, SGU_CHUNK), 1)
    causal = si <= ti
    sbt = sbt_ref[...]
    for hd in range(SGU_HEADS):
        wm = jnp.where(causal, sw_ref[hd], 0.0).astype(BF16)
        cols = slice(hd * SGU_HEAD_DIM, (hd + 1) * SGU_HEAD_DIM)
        bias = sbt[:, hd:hd + 1]
        for c in range(rows // SGU_CHUNK):
            rs = slice(c * SGU_CHUNK, (c + 1) * SGU_CHUNK)
            s_scr[rs, cols] = _dot(wm, v[rs, cols]) + bias
    yb = u * s_scr[...]
    yb_ref[...] = _rms(yb, og_ref[...]).astype(BF16)
    perm = perm_ref[...]
    hp = jnp.concatenate(
        [_dot(perm, h[c * T_CHUNK:(c + 1) * T_CHUNK]).astype(BF16) for c in range(rows // T_CHUNK)],
        axis=0)
    zs_ref[...] = _dot(hp, w_ref[:, 0:D_SSM])


def _in_proj(x2, g, w_in, perm, ln_g, ln_b, sgu_w, sgu_bt, og, to_cast):
    n = x2.shape[0]
    steps = n // ROWS_IN
    row = lambda i: (i, 0)
    return pl.pallas_call(
        _in_proj_kernel,
        name="in_proj",
        grid=(n // ROWS_IN,),
        in_specs=[
            pl.BlockSpec((ROWS_IN, D_MODEL), row),
            _resident((1, D_MODEL)),
            _resident(w_in.shape),
            _resident(perm.shape),
            _resident((1, D_SGU)),
            _resident((1, D_SGU)),
            _resident(sgu_w.shape),
            _resident(sgu_bt.shape),
            _resident((1, D_SGU)),
        ] + [_cast_block_spec(w, steps, row) for w in to_cast],
        out_specs=(pl.BlockSpec((ROWS_IN, D_SSM), row), pl.BlockSpec((ROWS_IN, D_SGU), row))
        + tuple(_cast_block_spec(w, steps, row) for w in to_cast),
        out_shape=(jax.ShapeDtypeStruct((n, D_SSM), F32), jax.ShapeDtypeStruct((n, D_SGU), BF16))
        + tuple(_cast_out_shape(w) for w in to_cast),
        scratch_shapes=[pltpu.VMEM((ROWS_IN, D_SGU), F32)],
        compiler_params=pltpu.CompilerParams(
            dimension_semantics=("arbitrary",), vmem_limit_bytes=VMEM_LIMIT),
    )(x2, g, w_in, perm, ln_g, ln_b, sgu_w, sgu_bt, og, *to_cast)


def _s5_kernel(zp_ref, zn_ref, bd_ref, cd_ref, tab_ref, d_ref, c0_ref,
               out_ref, c0_out, bu_a, bu_b, sb_a, sb_b, carry_ref):
    c0_out[...] = c0_ref[...].astype(BF16)

    @pl.when(pl.program_id(1) == 0)
    def _():
        carry_ref[...] = jnp.zeros_like(carry_ref)

    zp = zp_ref[...]
    zpb = zp.astype(BF16)
    znb = zn_ref[...].astype(BF16)
    bu = (bu_a, bu_b)
    sb = (sb_a, sb_b)
    seg = lax.broadcasted_iota(jnp.int32, (SUBLANES, SCAN_W), 0)
    zero = jnp.zeros((SUBLANES, SCAN_W), F32)
    blk = lambda j: slice(j * BLK_CH, (j + 1) * BLK_CH)

    @pl.when((pl.program_id(0) == 0) & (pl.program_id(1) == 0))
    def _():
        for p in range(N_PIECE):
            bu[0][p] = _dot(zpb[:, blk(0)], bd_ref[0, p])

    for j in range(N_BLK):
        out_ref[:, blk(j)] = d_ref[:, blk(j)] * zp[:, blk(j)]

    for j in range(N_BLK):
        cur, oth = j % 2, (j + 1) % 2

        def body(g, _, j=j, cur=cur, oth=oth):
            for part in range(2):
                p = 2 * g + part
                if j + 1 < N_BLK:
                    bu[oth][p] = _dot(zpb[:, blk(j + 1)], bd_ref[j + 1, p])
                else:
                    bu[oth][p] = _dot(znb, bd_ref[0, p])
                if j >= 1:
                    out_ref[:, blk(j - 1)] += _dot(sb[oth][p], cd_ref[j - 1, p])

            re_p, im_p = 2 * g, 2 * g + 1
            tre, tim = slice(0, SCAN_W), slice(SCAN_W, 2 * SCAN_W)
            ar = jnp.broadcast_to(tab_ref[j, g, 0:1, tre], (SUBLANES, SCAN_W))
            ai = jnp.broadcast_to(tab_ref[j, g, 0:1, tim], (SUBLANES, SCAN_W))

            def step(row0, sr, si):
                rows = pl.ds(row0, SUBLANES)
                return (ar * sr - ai * si + bu[cur][re_p, rows, :],
                        ar * si + ai * sr + bu[cur][im_p, rows, :])

            def start_states(xr, xi):
                for lvl, m in enumerate((1, 2, 4)):
                    mr = tab_ref[j, g, 1 + lvl:2 + lvl, tre]
                    mi = tab_ref[j, g, 1 + lvl:2 + lvl, tim]
                    rr = pltpu.roll(xr, m, 0)
                    ri = pltpu.roll(xi, m, 0)
                    keep = seg >= m
                    xr, xi = (xr + jnp.where(keep, mr * rr - mi * ri, 0.0),
                              xi + jnp.where(keep, mr * ri + mi * rr, 0.0))
                cr = carry_ref[j, g, :, tre]
                ci = carry_ref[j, g, :, tim]
                pr = tab_ref[j, g, 4:4 + SUBLANES, tre]
                pi = tab_ref[j, g, 4:4 + SUBLANES, tim]
                xr, xi = xr + pr * cr - pi * ci, xi + pr * ci + pi * cr
                carry_ref[j, g, :, tre] = jnp.broadcast_to(xr[SUBLANES - 1:SUBLANES], (SUBLANES, SCAN_W))
                carry_ref[j, g, :, tim] = jnp.broadcast_to(xi[SUBLANES - 1:SUBLANES], (SUBLANES, SCAN_W))
                return (jnp.where(seg >= 1, pltpu.roll(xr, 1, 0), cr),
                        jnp.where(seg >= 1, pltpu.roll(xi, 1, 0), ci))

            def emit_pair(base, m, sr, si):
                r1, i1 = step(base + 2 * m * SUBLANES, sr, si)
                sr, si = step(base + (2 * m + 1) * SUBLANES, r1, i1)
                rows = pl.ds(base + m * 2 * SUBLANES, 2 * SUBLANES)
                sb[cur][re_p, rows, :] = jnp.concatenate([r1, sr], axis=0).astype(BF16)
                sb[cur][im_p, rows, :] = jnp.concatenate([i1, si], axis=0).astype(BF16)
                return sr, si

            n_chunk = S5_ROWS // T_CHUNK
            state = None
            for c in range(n_chunk + 1):
                xr, xi = zero, zero
                for m in range(K_STEPS // 2):
                    if c < n_chunk:
                        xr, xi = step(c * T_CHUNK + 2 * m * SUBLANES, xr, xi)
                        xr, xi = step(c * T_CHUNK + (2 * m + 1) * SUBLANES, xr, xi)
                    if c >= 1:
                        state = emit_pair((c - 1) * T_CHUNK, m, *state)
                if c < n_chunk:
                    state = start_states(xr, xi)
            return 0

        lax.fori_loop(0, N_GRP, body, 0)

    last = N_BLK - 1
    for p in range(N_PIECE):
        out_ref[:, blk(last)] += _dot(sb[last % 2][p], cd_ref[last, p])


def _s5(zs, bsz, seq, bd, cd, tab, dvec, to_cast):
    steps = seq // S5_ROWS
    row = lambda b, c: (b * steps + c, 0)
    nxt = lambda b, c: (jnp.minimum(b * steps + c + 1, bsz * steps - 1), 0)
    return pl.pallas_call(
        _s5_kernel,
        name="s5",
        grid=(bsz, steps),
        in_specs=[
            pl.BlockSpec((S5_ROWS, D_SSM), row),
            pl.BlockSpec((S5_ROWS, BLK_CH), nxt),
            _resident(bd.shape), _resident(cd.shape), _resident(tab.shape),
            _resident(dvec.shape),
            _cast_block_spec(to_cast, bsz * steps, row),
        ],
        out_specs=(pl.BlockSpec((S5_ROWS, D_SSM), row), _cast_block_spec(to_cast, bsz * steps, row)),
        out_shape=(jax.ShapeDtypeStruct((bsz * seq, D_SSM), F32), _cast_out_shape(to_cast)),
        scratch_shapes=[
            pltpu.VMEM((N_PIECE, S5_ROWS, SCAN_W), F32),
            pltpu.VMEM((N_PIECE, S5_ROWS, SCAN_W), F32),
            pltpu.VMEM((N_PIECE, S5_ROWS, SCAN_W), BF16),
            pltpu.VMEM((N_PIECE, S5_ROWS, SCAN_W), BF16),
            pltpu.VMEM((N_BLK, N_GRP, SUBLANES, 2 * SCAN_W), F32),
        ],
        compiler_params=pltpu.CompilerParams(
            dimension_semantics=("arbitrary", "arbitrary"), vmem_limit_bytes=VMEM_LIMIT),
    )(zs, zs, bd, cd, tab, dvec, to_cast)


def _out_proj_kernel(x_ref, y_ref, yb_ref, w_ref, g_ref, gluw_ref, glub_ref, og_ref, permt_ref,
                     x1_ref, h_ref):
    half = D_MODEL // 2
    yb = yb_ref[...]
    y = jax.nn.gelu(y_ref[...])
    x1_lo = x_ref[:, 0:half] + _dot(yb, w_ref[D_SSM:, 0:half])
    gate = jax.nn.sigmoid(_dot(y.astype(BF16), gluw_ref[...]) + glub_ref[...])
    ya = _rms(y * gate, og_ref[...]).astype(BF16)
    x1_hi = x_ref[:, half:] + _dot(yb, w_ref[D_SSM:, half:])
    permt = permt_ref[...]
    ya = jnp.concatenate(
        [_dot(permt, ya[c * T_CHUNK:(c + 1) * T_CHUNK]).astype(BF16)
         for c in range(x_ref.shape[0] // T_CHUNK)], axis=0)
    x1 = jnp.concatenate([x1_lo, x1_hi], axis=1) + _dot(ya, w_ref[0:D_SSM, :])
    x1_ref[...] = x1
    h_ref[...] = _rms(x1, g_ref[...]).astype(BF16)


def _out_proj(x2, y, yb, w_out, g, glu_w, glu_b, og, permt):
    n = x2.shape[0]
    row = lambda i: (i, 0)
    return pl.pallas_call(
        _out_proj_kernel,
        name="out_proj",
        grid=(n // ROWS_IN,),
        in_specs=[
            pl.BlockSpec((ROWS_IN, D_MODEL), row),
            pl.BlockSpec((ROWS_IN, D_SSM), row),
            pl.BlockSpec((ROWS_IN, D_SGU), row),
            _resident(w_out.shape),
            _resident((1, D_MODEL)),
            _resident(glu_w.shape), _resident(glu_b.shape), _resident(og.shape),
            _resident(permt.shape),
        ],
        out_specs=(pl.BlockSpec((ROWS_IN, D_MODEL), row), pl.BlockSpec((ROWS_IN, D_MODEL), row)),
        out_shape=(jax.ShapeDtypeStruct((n, D_MODEL), F32), jax.ShapeDtypeStruct((n, D_MODEL), BF16)),
        compiler_params=pltpu.CompilerParams(
            dimension_semantics=("arbitrary",), vmem_limit_bytes=VMEM_LIMIT),
    )(x2, y, yb, w_out, g, glu_w, glu_b, og, permt)


def _ffn_kernel(h_ref, x1_ref, wg_ref, wu_ref, wo_ref, out_ref):
    @pl.when(pl.program_id(1) == 0)
    def _():
        out_ref[...] = x1_ref[...]

    h = h_ref[...]
    act = jax.nn.silu(_dot(h, wg_ref[...])) * _dot(h, wu_ref[...])
    out_ref[...] += _dot(act.astype(BF16), wo_ref[...])


def _ffn(h, x1, w_in, w_out):
    n = h.shape[0]
    nf = D_FFN // FFN_BLK
    return pl.pallas_call(
        _ffn_kernel,
        name="ffn",
        grid=(n // ROWS_FFN, nf),
        in_specs=[
            pl.BlockSpec((ROWS_FFN, D_MODEL), lambda i, f: (i, 0)),
            pl.BlockSpec((ROWS_FFN, D_MODEL), lambda i, f: (i, 0)),
            pl.BlockSpec((D_MODEL, FFN_BLK), lambda i, f: (0, f)),
            pl.BlockSpec((D_MODEL, FFN_BLK), lambda i, f: (0, f + nf)),
            pl.BlockSpec((FFN_BLK, D_MODEL), lambda i, f: (f, 0)),
        ],
        out_specs=pl.BlockSpec((ROWS_FFN, D_MODEL), lambda i, f: (i, 0)),
        out_shape=jax.ShapeDtypeStruct((n, D_MODEL), F32),
        compiler_params=pltpu.CompilerParams(
            dimension_semantics=("arbitrary", "arbitrary"), vmem_limit_bytes=VMEM_LIMIT_FFN),
    )(h, x1, w_in, w_in, w_out)


def _ple_kernel(x_ref, p_ref, wg_ref, bg_ref, wp_ref, g_ref, gf_ref, out_ref, *, final):
    x = x_ref[...]
    h = _rms(x, g_ref[...]).astype(BF16)
    pe = _dot(p_ref[...].astype(BF16), wp_ref[...])
    gate = jax.nn.sigmoid(_dot(h, wg_ref[...]) + bg_ref[...])
    x3 = x + gate * pe
    out_ref[...] = _rms(x3, gf_ref[...]) if final else x3


def _ple(x2, p2, w_gate, b_gate, w_proj, g, gf, final):
    n = x2.shape[0]
    row = lambda i: (i, 0)
    return pl.pallas_call(
        functools.partial(_ple_kernel, final=final),
        name="ple",
        grid=(n // ROWS_IN,),
        in_specs=[
            pl.BlockSpec((ROWS_IN, D_MODEL), row),
            pl.BlockSpec((ROWS_IN, PLE_DIM), row),
            _resident(w_gate.shape),
            _resident((1, D_MODEL)),
            _resident(w_proj.shape),
            _resident((1, D_MODEL)),
            _resident((1, D_MODEL)),
        ],
        out_specs=pl.BlockSpec((ROWS_IN, D_MODEL), row),
        out_shape=jax.ShapeDtypeStruct((n, D_MODEL), F32),
        compiler_params=pltpu.CompilerParams(
            dimension_semantics=("arbitrary",), vmem_limit_bytes=VMEM_LIMIT),
    )(x2, p2, w_gate, b_gate, w_proj, g, gf)


def _segment_permutation():
    r = np.arange(T_CHUNK)
    src = (r % SUBLANES) * K_STEPS + r // SUBLANES
    perm = np.zeros((T_CHUNK, T_CHUNK), np.float32)
    perm[r, src] = 1.0
    return perm


def _scan_exponents():
    n = [1.0]
    n += [float(K_STEPS * m) for m in (1, 2, 4)]
    n += [float(K_STEPS * (i + 1)) for i in range(SUBLANES)]
    return np.asarray(n, np.float32).reshape(-1, 1)


def kernel(x, p, norm_mix_g, w_in, ssm_lambda_re, ssm_lambda_im, ssm_log_step, ssm_b_re, ssm_b_im, ssm_c_re, ssm_c_im, ssm_d, ssm_glu_w, ssm_glu_b, sgu_ln_g, sgu_ln_b, sgu_w, sgu_b, out_norm_ssm_g, out_norm_sgu_g, w_out, norm_ffn_g, w_ffn_in, w_ffn_out, norm_ple_g, w_ple_gate, b_ple_gate, w_ple_proj, final_norm_g):
    bsz, seq, d_model = x.shape
    depth = w_in.shape[0]
    assert d_model == D_MODEL and x.dtype == F32
    assert seq % S5_ROWS == 0 and (bsz * seq) % ROWS_FFN == 0 and (bsz * seq) % ROWS_IN == 0
    perm_np = _segment_permutation()
    perm = jnp.asarray(perm_np, BF16)
    permt = jnp.asarray(perm_np.T, BF16)
    exponents = jnp.asarray(_scan_exponents())
    x2 = x.reshape(bsz * seq, D_MODEL)
    vec = lambda a: a.reshape(1, -1)
    for i in range(depth):
        tab, bd, cd = _s5_params(ssm_lambda_re[i], ssm_lambda_im[i], ssm_log_step[i], ssm_b_re[i],
                                 ssm_b_im[i], ssm_c_re[i], ssm_c_im[i], exponents)

        zs, yb, w_out_bf, w_gate_bf, glu_bf, w_ffn_out_bf, w_proj_bf = _in_proj(
            x2, vec(norm_mix_g[i]), w_in[i].astype(BF16), perm, vec(sgu_ln_g[i]), vec(sgu_ln_b[i]),
            sgu_w[i], sgu_b[i].T, vec(out_norm_sgu_g[i]),
            to_cast=(w_out[i], w_ple_gate[i], ssm_glu_w[i], w_ffn_out[i], w_ple_proj[i]))
        y, w_ffn_in_bf = _s5(zs, bsz, seq, bd, cd, tab, vec(ssm_d[i]), to_cast=w_ffn_in[i])
        x1, h = _out_proj(x2, y, yb, w_out_bf, vec(norm_ffn_g[i]), glu_bf, vec(ssm_glu_b[i]),
                          vec(out_norm_ssm_g[i]), permt)
        x2 = _ffn(h, x1, w_ffn_in_bf, w_ffn_out_bf)
        x2 = _ple(x2, p[i].reshape(bsz * seq, PLE_DIM), w_gate_bf, vec(b_ple_gate[i]),
                  w_proj_bf, vec(norm_ple_g[i]), vec(final_norm_g),
                  final=(i == depth - 1))
    return x2.reshape(bsz, seq, D_MODEL)
```

```python
import functools

import numpy as np
import jax
import jax.numpy as jnp
from jax import lax
from jax.experimental import pallas as pl
from jax.experimental.pallas import tpu as pltpu

D_MODEL = 2048
D_SSM = 1024
D_SGU = 1024
SSM_GROUP = 16
SSM_GROUPS = 64
SSM_STATE = 64
SGU_CHUNK = 128
SGU_HEADS = 8
SGU_HEAD_DIM = D_SGU // SGU_HEADS
D_FFN = 5632
PLE_DIM = 256
EPS = 1e-6
LAMBDA_RE_MAX = -1e-4

SUBLANES = 8
N_STATE = SSM_GROUPS * SSM_STATE
N_BLK = 4
BLK_CH = D_SSM // N_BLK
BLK_ST = N_STATE // N_BLK
BLK_W = 2 * BLK_ST
T_CHUNK = 256
K_STEPS = T_CHUNK // SUBLANES
S5_ROWS = 512
SCAN_W = 512
N_GRP = BLK_ST // SCAN_W
N_PIECE = 2 * N_GRP
PIECE_ROWS = S5_ROWS + 16

ROWS_IN = 512
ROWS_FFN = 1024
FFN_BLK = 512
VMEM_LIMIT = 56 * 1024 * 1024
VMEM_LIMIT_FFN = 60 * 1024 * 1024

F32 = jnp.float32
BF16 = jnp.bfloat16


def _rms(x, g):
    r = lax.rsqrt(jnp.mean(x * x, axis=-1, keepdims=True) + EPS)
    return (x * r) * g


def _dot(a, b):
    return jnp.dot(a, b, preferred_element_type=F32)


def _resident(shape):
    nd = len(shape)
    return pl.BlockSpec(shape, lambda *_: (0,) * nd, pipeline_mode=pl.Buffered(1))


def _cast_block_spec(w, steps, index):
    return pl.BlockSpec((w.shape[0] // steps, w.shape[1]), index)


def _cast_out_shape(w):
    return jax.ShapeDtypeStruct(w.shape, BF16)


def _s5_params_kernel(lr_ref, li_ref, ls_ref, n_ref, bre_ref, bim_ref, cre_ref, cim_ref, w_ref,
                      tab_ref, bd_ref, cd_ref, w_out):
    w_out[...] = w_ref[...].astype(BF16)

    g = lax.rem(pl.program_id(0), N_GRP)
    lr = jnp.minimum(lr_ref[...], LAMBDA_RE_MAX)
    li = li_ref[...]
    dt = jnp.exp(ls_ref[...])
    n = n_ref[...]
    mag = jnp.exp(n * (lr * dt))
    ang = n * (li * dt)
    tab_ref[0, 0] = jnp.concatenate([mag * jnp.cos(ang), mag * jnp.sin(ang)], axis=1)
    mag1 = jnp.exp(lr * dt)
    ang1 = li * dt
    nr = mag1 * jnp.cos(ang1) - 1.0
    ni = mag1 * jnp.sin(ang1)
    den = lr * lr + li * li
    q_re = (nr * lr + ni * li) / den
    q_im = (ni * lr - nr * li) / den
    bre = bre_ref[...]
    bim = bim_ref[...]
    bbar = (q_re * bre - q_im * bim, q_re * bim + q_im * bre)

    iota = lambda shape, dim: lax.broadcasted_iota(jnp.int32, shape, dim)
    grp_of_ch = lambda i: lax.shift_right_logical(i, 4)
    grp_of_st = lambda i: lax.shift_right_logical(i, 6)
    first_grp = g * (SCAN_W // SSM_STATE)
    same_b = grp_of_ch(iota((BLK_CH, SCAN_W), 0)) == grp_of_st(iota((BLK_CH, SCAN_W), 1)) + first_grp
    same_c = grp_of_st(iota((SCAN_W, BLK_CH), 0)) + first_grp == grp_of_ch(iota((SCAN_W, BLK_CH), 1))
    spread = (iota((SSM_GROUP, BLK_CH), 0)
              == lax.bitwise_and(iota((SSM_GROUP, BLK_CH), 1), SSM_GROUP - 1)).astype(BF16)
    for part in range(2):
        tiled = jnp.concatenate([bbar[part]] * SSM_GROUP, axis=0)
        bd_ref[0, part] = jnp.where(same_b, tiled, 0.0).astype(BF16)
        wide = _dot((cre_ref, cim_ref)[part][...].astype(BF16), spread)
        wide = wide if part == 0 else -wide
        cd_ref[0, part] = jnp.where(same_c, wide, 0.0).astype(BF16)


def _s5_params(lam_re, lam_im, log_step, b_re, b_im, c_re, c_im, exponents, w_in):
    n_rows = exponents.shape[0]
    steps = N_BLK * N_GRP
    lr = lam_re.reshape(1, N_STATE)
    li = lam_im.reshape(1, N_STATE)
    ls = jnp.repeat(log_step, SSM_STATE).reshape(1, N_STATE)
    bre = b_re.transpose(2, 0, 1).reshape(SSM_GROUP, N_STATE)
    bim = b_im.transpose(2, 0, 1).reshape(SSM_GROUP, N_STATE)
    cre = c_re.transpose(0, 2, 1).reshape(N_STATE, SSM_GROUP)
    cim = c_im.transpose(0, 2, 1).reshape(N_STATE, SSM_GROUP)
    lanes = lambda rows: pl.BlockSpec((rows, SCAN_W), lambda s: (0, s))
    slab = lambda shape: pl.BlockSpec((1,) + shape, lambda s: (s // N_GRP, s % N_GRP, 0, 0))
    w_block = pl.BlockSpec((w_in.shape[0] // steps, w_in.shape[1]), lambda s: (s, 0))
    return pl.pallas_call(
        _s5_params_kernel,
        name="s5_params",
        grid=(steps,),
        in_specs=[lanes(1), lanes(1), lanes(1), _resident(exponents.shape),
                  lanes(SSM_GROUP), lanes(SSM_GROUP),
                  pl.BlockSpec((SCAN_W, SSM_GROUP), lambda s: (s, 0)),
                  pl.BlockSpec((SCAN_W, SSM_GROUP), lambda s: (s, 0)),
                  w_block],
        out_specs=(slab((1, n_rows, 2 * SCAN_W)), slab((2, BLK_CH, SCAN_W)), slab((2, SCAN_W, BLK_CH)),
                   w_block),
        out_shape=(jax.ShapeDtypeStruct((N_BLK, N_GRP, n_rows, 2 * SCAN_W), F32),
                   jax.ShapeDtypeStruct((N_BLK, N_PIECE, BLK_CH, SCAN_W), BF16),
                   jax.ShapeDtypeStruct((N_BLK, N_PIECE, SCAN_W, BLK_CH), BF16),
                   jax.ShapeDtypeStruct(w_in.shape, BF16)),
        compiler_params=pltpu.CompilerParams(
            dimension_semantics=("arbitrary",), vmem_limit_bytes=VMEM_LIMIT),
    )(lr, li, ls, exponents, bre, bim, cre, cim, w_in)


def _in_proj_kernel(x_ref, g_ref, w_ref, perm_ref, lng_ref, lnb_ref, sw_ref, sbt_ref, og_ref,
                    c0_ref, c1_ref, c2_ref, c3_ref,
                    zs_ref, yb_ref, c0_out, c1_out, c2_out, c3_out, s_scr):
    c0_out[...] = c0_ref[...].astype(BF16)
    c1_out[...] = c1_ref[...].astype(BF16)
    c2_out[...] = c2_ref[...].astype(BF16)
    c3_out[...] = c3_ref[...].astype(BF16)
    rows = x_ref.shape[0]
    h = _rms(x_ref[...], g_ref[...]).astype(BF16)
    gv = jax.nn.gelu(_dot(h, w_ref[:, D_SSM + D_SGU:]))
    mu = jnp.mean(gv, axis=-1, keepdims=True)
    xc = gv - mu
    r = lax.rsqrt(jnp.mean(xc * xc, axis=-1, keepdims=True) + EPS)
    v = ((xc * r) * lng_ref[...] + lnb_ref[...]).astype(BF16)
    u = jax.nn.gelu(_dot(h, w_ref[:, D_SSM:D_SSM + D_SGU]))
    ti = lax.broadcasted_iota(jnp.int32, (SGU_CHUNK, SGU_CHUNK), 0)
    si = lax.broadcasted_iota(jnp.int32, (SGU_CHUNK, SGU_CHUNK), 1)
    causal = si <= ti
    sbt = sbt_ref[...]
    for hd in range(SGU_HEADS):
        wm = jnp.where(causal, sw_ref[hd], jnp.zeros((), BF16))
        cols = slice(hd * SGU_HEAD_DIM, (hd + 1) * SGU_HEAD_DIM)
        bias = sbt[:, hd:hd + 1]
        for c in range(rows // SGU_CHUNK):
            rs = slice(c * SGU_CHUNK, (c + 1) * SGU_CHUNK)
            s_scr[rs, cols] = _dot(wm, v[rs, cols]) + bias
    yb = u * s_scr[...]
    yb_ref[...] = _rms(yb, og_ref[...]).astype(BF16)
    perm = perm_ref[...]
    hp = jnp.concatenate(
        [_dot(perm, h[c * T_CHUNK:(c + 1) * T_CHUNK]).astype(BF16) for c in range(rows // T_CHUNK)],
        axis=0)
    zs_ref[...] = _dot(hp, w_ref[:, 0:D_SSM])


def _in_proj(x2, g, w_in, perm, ln_g, ln_b, sgu_w, sgu_bt, og, to_cast):
    n = x2.shape[0]
    steps = n // ROWS_IN
    row = lambda i: (i, 0)
    return pl.pallas_call(
        _in_proj_kernel,
        name="in_proj",
        grid=(n // ROWS_IN,),
        in_specs=[
            pl.BlockSpec((ROWS_IN, D_MODEL), row),
            _resident((1, D_MODEL)),
            _resident(w_in.shape),
            _resident(perm.shape),
            _resident((1, D_SGU)),
            _resident((1, D_SGU)),
            _resident(sgu_w.shape),
            _resident(sgu_bt.shape),
            _resident((1, D_SGU)),
        ] + [_cast_block_spec(w, steps, row) for w in to_cast],
        out_specs=(pl.BlockSpec((ROWS_IN, D_SSM), row), pl.BlockSpec((ROWS_IN, D_SGU), row))
        + tuple(_cast_block_spec(w, steps, row) for w in to_cast),
        out_shape=(jax.ShapeDtypeStruct((n, D_SSM), F32), jax.ShapeDtypeStruct((n, D_SGU), BF16))
        + tuple(_cast_out_shape(w) for w in to_cast),
        scratch_shapes=[pltpu.VMEM((ROWS_IN, D_SGU), F32)],
        compiler_params=pltpu.CompilerParams(
            dimension_semantics=("arbitrary",), vmem_limit_bytes=VMEM_LIMIT),
    )(x2, g, w_in, perm, ln_g, ln_b, sgu_w, sgu_bt, og, *to_cast)


def _s5_kernel(zp_ref, zn_ref, bd_ref, cd_ref, tab_ref, d_ref, c0_ref,
               out_ref, c0_out, bu_a, bu_b, sb_a, sb_b, carry_ref):
    c0_out[...] = c0_ref[...].astype(BF16)

    @pl.when(pl.program_id(1) == 0)
    def _():
        carry_ref[...] = jnp.zeros_like(carry_ref)

    zp = zp_ref[...]
    zpb = zp.astype(BF16)
    znb = zn_ref[...].astype(BF16)
    bu = (bu_a, bu_b)
    sb = (sb_a, sb_b)
    seg = lax.broadcasted_iota(jnp.int32, (SUBLANES, SCAN_W), 0)
    zero = jnp.zeros((SUBLANES, SCAN_W), F32)
    blk = lambda j: slice(j * BLK_CH, (j + 1) * BLK_CH)

    @pl.when((pl.program_id(0) == 0) & (pl.program_id(1) == 0))
    def _():
        for p in range(N_PIECE):
            bu[0][p, 0:S5_ROWS, :] = _dot(zpb[:, blk(0)], bd_ref[0, p])

    for j in range(N_BLK):
        out_ref[:, blk(j)] = d_ref[:, blk(j)] * zp[:, blk(j)]

    for j in range(N_BLK):
        cur, oth = j % 2, (j + 1) % 2

        def body(g, _, j=j, cur=cur, oth=oth):
            for part in range(2):
                p = 2 * g + part
                if j + 1 < N_BLK:
                    bu[oth][p, 0:S5_ROWS, :] = _dot(zpb[:, blk(j + 1)], bd_ref[j + 1, p])
                else:
                    bu[oth][p, 0:S5_ROWS, :] = _dot(znb, bd_ref[0, p])
                if j >= 1:
                    out_ref[:, blk(j - 1)] += _dot(sb[oth][p, 0:S5_ROWS, :], cd_ref[j - 1, p])

            re_p, im_p = 2 * g, 2 * g + 1
            tre, tim = slice(0, SCAN_W), slice(SCAN_W, 2 * SCAN_W)
            ar = jnp.broadcast_to(tab_ref[j, g, 0:1, tre], (SUBLANES, SCAN_W))
            ai = jnp.broadcast_to(tab_ref[j, g, 0:1, tim], (SUBLANES, SCAN_W))

            def step(row0, sr, si):
                rows = pl.ds(row0, SUBLANES)
                return (ar * sr - ai * si + bu[cur][re_p, rows, :],
                        ar * si + ai * sr + bu[cur][im_p, rows, :])

            def start_states(xr, xi):
                for lvl, m in enumerate((1, 2, 4)):
                    mr = tab_ref[j, g, 1 + lvl:2 + lvl, tre]
                    mi = tab_ref[j, g, 1 + lvl:2 + lvl, tim]
                    rr = pltpu.roll(xr, m, 0)
                    ri = pltpu.roll(xi, m, 0)
                    keep = seg >= m
                    xr, xi = (xr + jnp.where(keep, mr * rr - mi * ri, 0.0),
                              xi + jnp.where(keep, mr * ri + mi * rr, 0.0))
                cr = carry_ref[j, g, :, tre]
                ci = carry_ref[j, g, :, tim]
                pr = tab_ref[j, g, 4:4 + SUBLANES, tre]
                pi = tab_ref[j, g, 4:4 + SUBLANES, tim]
                xr, xi = xr + pr * cr - pi * ci, xi + pr * ci + pi * cr
                carry_ref[j, g, :, tre] = jnp.broadcast_to(xr[SUBLANES - 1:SUBLANES], (SUBLANES, SCAN_W))
                carry_ref[j, g, :, tim] = jnp.broadcast_to(xi[SUBLANES - 1:SUBLANES], (SUBLANES, SCAN_W))
                return (jnp.where(seg >= 1, pltpu.roll(xr, 1, 0), cr),
                        jnp.where(seg >= 1, pltpu.roll(xi, 1, 0), ci))

            def emit_pair(base, m, sr, si):
                r1, i1 = step(base + 2 * m * SUBLANES, sr, si)
                sr, si = step(base + (2 * m + 1) * SUBLANES, r1, i1)
                rows = pl.ds(base + m * 2 * SUBLANES, 2 * SUBLANES)
                sb[cur][re_p, rows, :] = jnp.concatenate([r1, sr], axis=0).astype(BF16)
                sb[cur][im_p, rows, :] = jnp.concatenate([i1, si], axis=0).astype(BF16)
                return sr, si

            n_chunk = S5_ROWS // T_CHUNK
            state = None
            for c in range(n_chunk + 1):
                xr, xi = zero, zero
                for m in range(K_STEPS // 2):
                    if c < n_chunk:
                        xr, xi = step(c * T_CHUNK + 2 * m * SUBLANES, xr, xi)
                        xr, xi = step(c * T_CHUNK + (2 * m + 1) * SUBLANES, xr, xi)
                    if c >= 1:
                        state = emit_pair((c - 1) * T_CHUNK, m, *state)
                if c < n_chunk:
                    state = start_states(xr, xi)
            return 0

        lax.fori_loop(0, N_GRP, body, 0)

    last = N_BLK - 1
    for p in range(N_PIECE):
        out_ref[:, blk(last)] += _dot(sb[last % 2][p, 0:S5_ROWS, :], cd_ref[last, p])


def _s5(zs, bsz, seq, bd, cd, tab, dvec, to_cast):
    steps = seq // S5_ROWS
    row = lambda b, c: (b * steps + c, 0)
    nxt = lambda b, c: (jnp.minimum(b * steps + c + 1, bsz * steps - 1), 0)
    return pl.pallas_call(
        _s5_kernel,
        name="s5",
        grid=(bsz, steps),
        in_specs=[
            pl.BlockSpec((S5_ROWS, D_SSM), row),
            pl.BlockSpec((S5_ROWS, BLK_CH), nxt),
            _resident(bd.shape), _resident(cd.shape), _resident(tab.shape),
            _resident(dvec.shape),
            _cast_block_spec(to_cast, bsz * steps, row),
        ],
        out_specs=(pl.BlockSpec((S5_ROWS, D_SSM), row), _cast_block_spec(to_cast, bsz * steps, row)),
        out_shape=(jax.ShapeDtypeStruct((bsz * seq, D_SSM), F32), _cast_out_shape(to_cast)),
        scratch_shapes=[
            pltpu.VMEM((N_PIECE, PIECE_ROWS, SCAN_W), F32),
            pltpu.VMEM((N_PIECE, PIECE_ROWS, SCAN_W), F32),
            pltpu.VMEM((N_PIECE, PIECE_ROWS, SCAN_W), BF16),
            pltpu.VMEM((N_PIECE, PIECE_ROWS, SCAN_W), BF16),
            pltpu.VMEM((N_BLK, N_GRP, SUBLANES, 2 * SCAN_W), F32),
        ],
        compiler_params=pltpu.CompilerParams(
            dimension_semantics=("arbitrary", "arbitrary"), vmem_limit_bytes=VMEM_LIMIT),
    )(zs, zs, bd, cd, tab, dvec, to_cast)


def _out_proj_kernel(x_ref, y_ref, yb_ref, w_ref, g_ref, gluw_ref, glub_ref, og_ref, permt_ref,
                     x1_ref, h_ref):
    half = D_MODEL // 2
    yb = yb_ref[...]
    y = jax.nn.gelu(y_ref[...])
    x1_lo = x_ref[:, 0:half] + _dot(yb, w_ref[D_SSM:, 0:half])
    gate = jax.nn.sigmoid(_dot(y.astype(BF16), gluw_ref[...]) + glub_ref[...])
    ya = _rms(y * gate, og_ref[...]).astype(BF16)
    x1_hi = x_ref[:, half:] + _dot(yb, w_ref[D_SSM:, half:])
    permt = permt_ref[...]
    ya = jnp.concatenate(
        [_dot(permt, ya[c * T_CHUNK:(c + 1) * T_CHUNK]).astype(BF16)
         for c in range(x_ref.shape[0] // T_CHUNK)], axis=0)
    x1 = jnp.concatenate([x1_lo, x1_hi], axis=1) + _dot(ya, w_ref[0:D_SSM, :])
    x1_ref[...] = x1
    h_ref[...] = _rms(x1, g_ref[...]).astype(BF16)


def _out_proj(x2, y, yb, w_out, g, glu_w, glu_b, og, permt):
    n = x2.shape[0]
    row = lambda i: (i, 0)
    return pl.pallas_call(
        _out_proj_kernel,
        name="out_proj",
        grid=(n // ROWS_IN,),
        in_specs=[
            pl.BlockSpec((ROWS_IN, D_MODEL), row),
            pl.BlockSpec((ROWS_IN, D_SSM), row),
            pl.BlockSpec((ROWS_IN, D_SGU), row),
            _resident(w_out.shape),
            _resident((1, D_MODEL)),
            _resident(glu_w.shape), _resident(glu_b.shape), _resident(og.shape),
            _resident(permt.shape),
        ],
        out_specs=(pl.BlockSpec((ROWS_IN, D_MODEL), row), pl.BlockSpec((ROWS_IN, D_MODEL), row)),
        out_shape=(jax.ShapeDtypeStruct((n, D_MODEL), F32), jax.ShapeDtypeStruct((n, D_MODEL), BF16)),
        compiler_params=pltpu.CompilerParams(
            dimension_semantics=("arbitrary",), vmem_limit_bytes=VMEM_LIMIT),
    )(x2, y, yb, w_out, g, glu_w, glu_b, og, permt)


def _ffn_kernel(h_ref, x1_ref, wg_ref, wu_ref, wo_ref, out_ref):
    @pl.when(pl.program_id(1) == 0)
    def _():
        out_ref[...] = x1_ref[...]

    h = h_ref[...]
    act = jax.nn.silu(_dot(h, wg_ref[...])) * _dot(h, wu_ref[...])
    out_ref[...] += _dot(act.astype(BF16), wo_ref[...])


def _ffn(h, x1, w_in, w_out):
    n = h.shape[0]
    nf = D_FFN // FFN_BLK
    return pl.pallas_call(
        _ffn_kernel,
        name="ffn",
        grid=(n // ROWS_FFN, nf),
        in_specs=[
            pl.BlockSpec((ROWS_FFN, D_MODEL), lambda i, f: (i, 0)),
            pl.BlockSpec((ROWS_FFN, D_MODEL), lambda i, f: (i, 0)),
            pl.BlockSpec((D_MODEL, FFN_BLK), lambda i, f: (0, f)),
            pl.BlockSpec((D_MODEL, FFN_BLK), lambda i, f: (0, f + nf)),
            pl.BlockSpec((FFN_BLK, D_MODEL), lambda i, f: (f, 0)),
        ],
        out_specs=pl.BlockSpec((ROWS_FFN, D_MODEL), lambda i, f: (i, 0)),
        out_shape=jax.ShapeDtypeStruct((n, D_MODEL), F32),
        compiler_params=pltpu.CompilerParams(
            dimension_semantics=("arbitrary", "arbitrary"), vmem_limit_bytes=VMEM_LIMIT_FFN),
    )(h, x1, w_in, w_in, w_out)


def _ple_kernel(x_ref, p_ref, wg_ref, bg_ref, wp_ref, g_ref, gf_ref, out_ref, *, final):
    x = x_ref[...]
    h = _rms(x, g_ref[...]).astype(BF16)
    pe = _dot(p_ref[...].astype(BF16), wp_ref[...])
    gate = jax.nn.sigmoid(_dot(h, wg_ref[...]) + bg_ref[...])
    x3 = x + gate * pe
    out_ref[...] = _rms(x3, gf_ref[...]) if final else x3


def _ple(x2, p2, w_gate, b_gate, w_proj, g, gf, final):
    n = x2.shape[0]
    row = lambda i: (i, 0)
    return pl.pallas_call(
        functools.partial(_ple_kernel, final=final),
        name="ple",
        grid=(n // ROWS_IN,),
        in_specs=[
            pl.BlockSpec((ROWS_IN, D_MODEL), row),
            pl.BlockSpec((ROWS_IN, PLE_DIM), row),
            _resident(w_gate.shape),
            _resident((1, D_MODEL)),
            _resident(w_proj.shape),
            _resident((1, D_MODEL)),
            _resident((1, D_MODEL)),
        ],
        out_specs=pl.BlockSpec((ROWS_IN, D_MODEL), row),
        out_shape=jax.ShapeDtypeStruct((n, D_MODEL), F32),
        compiler_params=pltpu.CompilerParams(
            dimension_semantics=("arbitrary",), vmem_limit_bytes=VMEM_LIMIT),
    )(x2, p2, w_gate, b_gate, w_proj, g, gf)


def _segment_permutation():
    r = np.arange(T_CHUNK)
    src = (r % SUBLANES) * K_STEPS + r // SUBLANES
    perm = np.zeros((T_CHUNK, T_CHUNK), np.float32)
    perm[r, src] = 1.0
    return perm


def _scan_exponents():
    n = [1.0]
    n += [float(K_STEPS * m) for m in (1, 2, 4)]
    n += [float(K_STEPS * (i + 1)) for i in range(SUBLANES)]
    return np.asarray(n, np.float32).reshape(-1, 1)


def kernel(x, p, norm_mix_g, w_in, ssm_lambda_re, ssm_lambda_im, ssm_log_step, ssm_b_re, ssm_b_im, ssm_c_re, ssm_c_im, ssm_d, ssm_glu_w, ssm_glu_b, sgu_ln_g, sgu_ln_b, sgu_w, sgu_b, out_norm_ssm_g, out_norm_sgu_g, w_out, norm_ffn_g, w_ffn_in, w_ffn_out, norm_ple_g, w_ple_gate, b_ple_gate, w_ple_proj, final_norm_g):
    bsz, seq, _ = x.shape
    depth = w_in.shape[0]
    perm_np = _segment_permutation()
    perm = jnp.asarray(perm_np, BF16)
    permt = jnp.asarray(perm_np.T, BF16)
    exponents = jnp.asarray(_scan_exponents())
    x2 = x.reshape(bsz * seq, D_MODEL)
    vec = lambda a: a.reshape(1, -1)
    for i in range(depth):
        tab, bd, cd, w_in_bf = _s5_params(
            ssm_lambda_re[i], ssm_lambda_im[i], ssm_log_step[i], ssm_b_re[i], ssm_b_im[i],
            ssm_c_re[i], ssm_c_im[i], exponents, w_in[i])

        zs, yb, w_out_bf, w_gate_bf, glu_bf, w_ffn_out_bf = _in_proj(
            x2, vec(norm_mix_g[i]), w_in_bf, perm, vec(sgu_ln_g[i]), vec(sgu_ln_b[i]),
            sgu_w[i].astype(BF16), sgu_b[i].T, vec(out_norm_sgu_g[i]),
            to_cast=(w_out[i], w_ple_gate[i], ssm_glu_w[i], w_ffn_out[i]))
        y, w_ffn_in_bf = _s5(zs, bsz, seq, bd, cd, tab, vec(ssm_d[i]), to_cast=w_ffn_in[i])
        x1, h = _out_proj(x2, y, yb, w_out_bf, vec(norm_ffn_g[i]), glu_bf, vec(ssm_glu_b[i]),
                          vec(out_norm_ssm_g[i]), permt)
        x2 = _ffn(h, x1, w_ffn_in_bf, w_ffn_out_bf)
        x2 = _ple(x2, p[i].reshape(bsz * seq, PLE_DIM), w_gate_bf, vec(b_ple_gate[i]),
                  w_ple_proj[i].astype(BF16), vec(norm_ple_g[i]), vec(final_norm_g),
                  final=(i == depth - 1))
    return x2.reshape(bsz, seq, D_MODEL)
```

```python
import functools

import numpy as np
import jax
import jax.numpy as jnp
from jax import lax
from jax.experimental import pallas as pl
from jax.experimental.pallas import tpu as pltpu

D_MODEL = 2048
D_SSM = 1024
D_SGU = 1024
SSM_GROUP = 16
SSM_GROUPS = 64
SSM_STATE = 64
SGU_CHUNK = 128
SGU_HEADS = 8
SGU_HEAD_DIM = D_SGU // SGU_HEADS
D_FFN = 5632
PLE_DIM = 256
EPS = 1e-6
LAMBDA_RE_MAX = -1e-4

SUBLANES = 8
N_STATE = SSM_GROUPS * SSM_STATE
N_BLK = 4
BLK_CH = D_SSM // N_BLK
BLK_ST = N_STATE // N_BLK
BLK_W = 2 * BLK_ST
T_CHUNK = 256
K_STEPS = T_CHUNK // SUBLANES
S5_ROWS = 512
SCAN_W = 512
N_GRP = BLK_ST // SCAN_W
N_PIECE = 2 * N_GRP

ROWS_IN = 512
ROWS_FFN = 1024
FFN_BLK = 512
VMEM_LIMIT = 56 * 1024 * 1024
VMEM_LIMIT_FFN = 60 * 1024 * 1024

F32 = jnp.float32
BF16 = jnp.bfloat16


def _rms(x, g):
    r = lax.rsqrt(jnp.mean(x * x, axis=-1, keepdims=True) + EPS)
    return (x * r) * g


def _dot(a, b):
    return jnp.dot(a, b, preferred_element_type=F32)


def _resident(shape):
    nd = len(shape)
    return pl.BlockSpec(shape, lambda *_: (0,) * nd, pipeline_mode=pl.Buffered(1))


def _cast_block_spec(w, steps, index):
    return pl.BlockSpec((w.shape[0] // steps, w.shape[1]), index)


def _cast_out_shape(w):
    return jax.ShapeDtypeStruct(w.shape, BF16)


def _s5_params_kernel(lr_ref, li_ref, ls_ref, n_ref, bre_ref, bim_ref, cre_ref, cim_ref,
                      tab_ref, bd_ref, cd_ref):
    lr = jnp.minimum(lr_ref[...], LAMBDA_RE_MAX)
    li = li_ref[...]
    dt = jnp.exp(ls_ref[...])
    n = n_ref[...]
    mag = jnp.exp(n * (lr * dt))
    ang = n * (li * dt)
    pwr = mag * jnp.cos(ang)
    pwi = mag * jnp.sin(ang)
    mag1 = jnp.exp(lr * dt)
    ang1 = li * dt
    nr = mag1 * jnp.cos(ang1) - 1.0
    ni = mag1 * jnp.sin(ang1)
    den = lr * lr + li * li
    q_re = (nr * lr + ni * li) / den
    q_im = (ni * lr - nr * li) / den
    bre = bre_ref[...]
    bim = bim_ref[...]
    bbar = (q_re * bre - q_im * bim, q_re * bim + q_im * bre)

    iota = lambda shape, dim: lax.broadcasted_iota(jnp.int32, shape, dim)
    grp_of_ch = lambda i: lax.shift_right_logical(i, 4)
    grp_of_st = lambda i: lax.shift_right_logical(i, 6)
    b_row_grp = grp_of_ch(iota((BLK_CH, SCAN_W), 0))
    b_col_grp = grp_of_st(iota((BLK_CH, SCAN_W), 1))
    c_row_grp = grp_of_st(iota((SCAN_W, BLK_CH), 0))
    c_col_grp = grp_of_ch(iota((SCAN_W, BLK_CH), 1))
    spread = (iota((SSM_GROUP, BLK_CH), 0)
              == lax.bitwise_and(iota((SSM_GROUP, BLK_CH), 1), SSM_GROUP - 1)).astype(BF16)
    grp_per_half = SCAN_W // SSM_STATE
    for j in range(N_BLK):
        for g in range(N_GRP):
            st = slice((j * N_GRP + g) * SCAN_W, (j * N_GRP + g + 1) * SCAN_W)
            tab_ref[j, g] = jnp.concatenate([pwr[:, st], pwi[:, st]], axis=1)
            same_b = b_row_grp == b_col_grp + g * grp_per_half
            same_c = c_row_grp + g * grp_per_half == c_col_grp
            for part in range(2):
                tiled = jnp.concatenate([bbar[part][:, st]] * SSM_GROUP, axis=0)
                bd_ref[j, 2 * g + part] = jnp.where(same_b, tiled, 0.0).astype(BF16)
                c_ref = (cre_ref, cim_ref)[part]
                first = (j * N_GRP + g) * grp_per_half
                wide = jnp.concatenate(
                    [lax.dot_general(c_ref[(first + q) * SSM_GROUP:(first + q + 1) * SSM_GROUP, :].astype(BF16),
                                     spread, (((0,), (0,)), ((), ())), preferred_element_type=F32)
                     for q in range(grp_per_half)], axis=0)
                wide = wide if part == 0 else -wide
                cd_ref[j, 2 * g + part] = jnp.where(same_c, wide, 0.0).astype(BF16)


def _s5_params(lam_re, lam_im, log_step, b_re, b_im, c_re, c_im, exponents):
    n_rows = exponents.shape[0]
    lr = lam_re.reshape(1, N_STATE)
    li = lam_im.reshape(1, N_STATE)
    ls = jnp.repeat(log_step, SSM_STATE).reshape(1, N_STATE)
    bre = b_re.transpose(2, 0, 1).reshape(SSM_GROUP, N_STATE)
    bim = b_im.transpose(2, 0, 1).reshape(SSM_GROUP, N_STATE)
    cre = c_re.reshape(SSM_GROUPS * SSM_GROUP, SSM_STATE)
    cim = c_im.reshape(SSM_GROUPS * SSM_GROUP, SSM_STATE)
    return pl.pallas_call(
        _s5_params_kernel,
        name="s5_params",
        out_shape=(jax.ShapeDtypeStruct((N_BLK, N_GRP, n_rows, 2 * SCAN_W), F32),
                   jax.ShapeDtypeStruct((N_BLK, N_PIECE, BLK_CH, SCAN_W), BF16),
                   jax.ShapeDtypeStruct((N_BLK, N_PIECE, SCAN_W, BLK_CH), BF16)),
        compiler_params=pltpu.CompilerParams(vmem_limit_bytes=VMEM_LIMIT),
    )(lr, li, ls, exponents, bre, bim, cre, cim)


def _in_proj_kernel(x_ref, g_ref, w_ref, perm_ref, lng_ref, lnb_ref, sw_ref, sbt_ref, og_ref,
                    c0_ref, c1_ref, c2_ref, c3_ref,
                    zs_ref, yb_ref, c0_out, c1_out, c2_out, c3_out, s_scr):
    c0_out[...] = c0_ref[...].astype(BF16)
    c1_out[...] = c1_ref[...].astype(BF16)
    c2_out[...] = c2_ref[...].astype(BF16)
    c3_out[...] = c3_ref[...].astype(BF16)
    rows = x_ref.shape[0]
    h = _rms(x_ref[...], g_ref[...]).astype(BF16)
    gv = jax.nn.gelu(_dot(h, w_ref[:, D_SSM + D_SGU:]))
    mu = jnp.mean(gv, axis=-1, keepdims=True)
    xc = gv - mu
    r = lax.rsqrt(jnp.mean(xc * xc, axis=-1, keepdims=True) + EPS)
    v = ((xc * r) * lng_ref[...] + lnb_ref[...]).astype(BF16)
    u = jax.nn.gelu(_dot(h, w_ref[:, D_SSM:D_SSM + D_SGU]))
    ti = lax.broadcasted_iota(jnp.int32, (SGU_CHUNK, SGU_CHUNK), 0)
    si = lax.broadcasted_iota(jnp.int32, (SGU_CHUNK, SGU_CHUNK), 1)
    causal = si <= ti
    sbt = sbt_ref[...]
    for hd in range(SGU_HEADS):
        wm = jnp.where(causal, sw_ref[hd], jnp.zeros((), BF16))
        cols = slice(hd * SGU_HEAD_DIM, (hd + 1) * SGU_HEAD_DIM)
        bias = sbt[:, hd:hd + 1]
        for c in range(rows // SGU_CHUNK):
            rs = slice(c * SGU_CHUNK, (c + 1) * SGU_CHUNK)
            s_scr[rs, cols] = _dot(wm, v[rs, cols]) + bias
    yb = u * s_scr[...]
    yb_ref[...] = _rms(yb, og_ref[...]).astype(BF16)
    perm = perm_ref[...]
    hp = jnp.concatenate(
        [_dot(perm, h[c * T_CHUNK:(c + 1) * T_CHUNK]).astype(BF16) for c in range(rows // T_CHUNK)],
        axis=0)
    zs_ref[...] = _dot(hp, w_ref[:, 0:D_SSM])


def _in_proj(x2, g, w_in, perm, ln_g, ln_b, sgu_w, sgu_bt, og, to_cast):
    n = x2.shape[0]
    steps = n // ROWS_IN
    row = lambda i: (i, 0)
    return pl.pallas_call(
        _in_proj_kernel,
        name="in_proj",
        grid=(n // ROWS_IN,),
        in_specs=[
            pl.BlockSpec((ROWS_IN, D_MODEL), row),
            _resident((1, D_MODEL)),
            _resident(w_in.shape),
            _resident(perm.shape),
            _resident((1, D_SGU)),
            _resident((1, D_SGU)),
            _resident(sgu_w.shape),
            _resident(sgu_bt.shape),
            _resident((1, D_SGU)),
        ] + [_cast_block_spec(w, steps, row) for w in to_cast],
        out_specs=(pl.BlockSpec((ROWS_IN, D_SSM), row), pl.BlockSpec((ROWS_IN, D_SGU), row))
        + tuple(_cast_block_spec(w, steps, row) for w in to_cast),
        out_shape=(jax.ShapeDtypeStruct((n, D_SSM), F32), jax.ShapeDtypeStruct((n, D_SGU), BF16))
        + tuple(_cast_out_shape(w) for w in to_cast),
        scratch_shapes=[pltpu.VMEM((ROWS_IN, D_SGU), F32)],
        compiler_params=pltpu.CompilerParams(
            dimension_semantics=("arbitrary",), vmem_limit_bytes=VMEM_LIMIT),
    )(x2, g, w_in, perm, ln_g, ln_b, sgu_w, sgu_bt, og, *to_cast)


def _s5_kernel(zp_ref, zn_ref, bd_ref, cd_ref, tab_ref, d_ref, c0_ref,
               out_ref, c0_out, bu_a, bu_b, sb_a, sb_b, carry_ref):
    c0_out[...] = c0_ref[...].astype(BF16)

    @pl.when(pl.program_id(1) == 0)
    def _():
        carry_ref[...] = jnp.zeros_like(carry_ref)

    zp = zp_ref[...]
    zpb = zp.astype(BF16)
    znb = zn_ref[...].astype(BF16)
    bu = (bu_a, bu_b)
    sb = (sb_a, sb_b)
    seg = lax.broadcasted_iota(jnp.int32, (SUBLANES, SCAN_W), 0)
    zero = jnp.zeros((SUBLANES, SCAN_W), F32)
    blk = lambda j: slice(j * BLK_CH, (j + 1) * BLK_CH)

    @pl.when((pl.program_id(0) == 0) & (pl.program_id(1) == 0))
    def _():
        for p in range(N_PIECE):
            bu[0][p] = _dot(zpb[:, blk(0)], bd_ref[0, p])

    for j in range(N_BLK):
        out_ref[:, blk(j)] = d_ref[:, blk(j)] * zp[:, blk(j)]

    for j in range(N_BLK):
        cur, oth = j % 2, (j + 1) % 2

        def body(g, _, j=j, cur=cur, oth=oth):
            for part in range(2):
                p = 2 * g + part
                if j + 1 < N_BLK:
                    bu[oth][p] = _dot(zpb[:, blk(j + 1)], bd_ref[j + 1, p])
                else:
                    bu[oth][p] = _dot(znb, bd_ref[0, p])
                if j >= 1:
                    out_ref[:, blk(j - 1)] += _dot(sb[oth][p], cd_ref[j - 1, p])

            re_p, im_p = 2 * g, 2 * g + 1
            tre, tim = slice(0, SCAN_W), slice(SCAN_W, 2 * SCAN_W)
            ar = jnp.broadcast_to(tab_ref[j, g, 0:1, tre], (SUBLANES, SCAN_W))
            ai = jnp.broadcast_to(tab_ref[j, g, 0:1, tim], (SUBLANES, SCAN_W))

            def step(row0, sr, si):
                rows = pl.ds(row0, SUBLANES)
                return (ar * sr - ai * si + bu[cur][re_p, rows, :],
                        ar * si + ai * sr + bu[cur][im_p, rows, :])

            def start_states(xr, xi):
                for lvl, m in enumerate((1, 2, 4)):
                    mr = tab_ref[j, g, 1 + lvl:2 + lvl, tre]
                    mi = tab_ref[j, g, 1 + lvl:2 + lvl, tim]
                    rr = pltpu.roll(xr, m, 0)
                    ri = pltpu.roll(xi, m, 0)
                    keep = seg >= m
                    xr, xi = (xr + jnp.where(keep, mr * rr - mi * ri, 0.0),
                              xi + jnp.where(keep, mr * ri + mi * rr, 0.0))
                cr = carry_ref[j, g, :, tre]
                ci = carry_ref[j, g, :, tim]
                pr = tab_ref[j, g, 4:4 + SUBLANES, tre]
                pi = tab_ref[j, g, 4:4 + SUBLANES, tim]
                xr, xi = xr + pr * cr - pi * ci, xi + pr * ci + pi * cr
                carry_ref[j, g, :, tre] = jnp.broadcast_to(xr[SUBLANES - 1:SUBLANES], (SUBLANES, SCAN_W))
                carry_ref[j, g, :, tim] = jnp.broadcast_to(xi[SUBLANES - 1:SUBLANES], (SUBLANES, SCAN_W))
                return (jnp.where(seg >= 1, pltpu.roll(xr, 1, 0), cr),
                        jnp.where(seg >= 1, pltpu.roll(xi, 1, 0), ci))

            def emit_pair(base, m, sr, si):
                r1, i1 = step(base + 2 * m * SUBLANES, sr, si)
                sr, si = step(base + (2 * m + 1) * SUBLANES, r1, i1)
                rows = pl.ds(base + m * 2 * SUBLANES, 2 * SUBLANES)
                sb[cur][re_p, rows, :] = jnp.concatenate([r1, sr], axis=0).astype(BF16)
                sb[cur][im_p, rows, :] = jnp.concatenate([i1, si], axis=0).astype(BF16)
                return sr, si

            n_chunk = S5_ROWS // T_CHUNK
            state = None
            for c in range(n_chunk + 1):
                xr, xi = zero, zero
                for m in range(K_STEPS // 2):
                    if c < n_chunk:
                        xr, xi = step(c * T_CHUNK + 2 * m * SUBLANES, xr, xi)
                        xr, xi = step(c * T_CHUNK + (2 * m + 1) * SUBLANES, xr, xi)
                    if c >= 1:
                        state = emit_pair((c - 1) * T_CHUNK, m, *state)
                if c < n_chunk:
                    state = start_states(xr, xi)
            return 0

        lax.fori_loop(0, N_GRP, body, 0)

    last = N_BLK - 1
    for p in range(N_PIECE):
        out_ref[:, blk(last)] += _dot(sb[last % 2][p], cd_ref[last, p])


def _s5(zs, bsz, seq, bd, cd, tab, dvec, to_cast):
    steps = seq // S5_ROWS
    row = lambda b, c: (b * steps + c, 0)
    nxt = lambda b, c: (jnp.minimum(b * steps + c + 1, bsz * steps - 1), 0)
    return pl.pallas_call(
        _s5_kernel,
        name="s5",
        grid=(bsz, steps),
        in_specs=[
            pl.BlockSpec((S5_ROWS, D_SSM), row),
            pl.BlockSpec((S5_ROWS, BLK_CH), nxt),
            _resident(bd.shape), _resident(cd.shape), _resident(tab.shape),
            _resident(dvec.shape),
            _cast_block_spec(to_cast, bsz * steps, row),
        ],
        out_specs=(pl.BlockSpec((S5_ROWS, D_SSM), row), _cast_block_spec(to_cast, bsz * steps, row)),
        out_shape=(jax.ShapeDtypeStruct((bsz * seq, D_SSM), F32), _cast_out_shape(to_cast)),
        scratch_shapes=[
            pltpu.VMEM((N_PIECE, S5_ROWS, SCAN_W), F32),
            pltpu.VMEM((N_PIECE, S5_ROWS, SCAN_W), F32),
            pltpu.VMEM((N_PIECE, S5_ROWS, SCAN_W), BF16),
            pltpu.VMEM((N_PIECE, S5_ROWS, SCAN_W), BF16),
            pltpu.VMEM((N_BLK, N_GRP, SUBLANES, 2 * SCAN_W), F32),
        ],
        compiler_params=pltpu.CompilerParams(
            dimension_semantics=("arbitrary", "arbitrary"), vmem_limit_bytes=VMEM_LIMIT),
    )(zs, zs, bd, cd, tab, dvec, to_cast)


def _out_proj_kernel(x_ref, y_ref, yb_ref, w_ref, g_ref, gluw_ref, glub_ref, og_ref, permt_ref,
                     x1_ref, h_ref):
    half = D_MODEL // 2
    yb = yb_ref[...]
    y = jax.nn.gelu(y_ref[...])
    x1_lo = x_ref[:, 0:half] + _dot(yb, w_ref[D_SSM:, 0:half])
    gate = jax.nn.sigmoid(_dot(y.astype(BF16), gluw_ref[...]) + glub_ref[...])
    ya = _rms(y * gate, og_ref[...]).astype(BF16)
    x1_hi = x_ref[:, half:] + _dot(yb, w_ref[D_SSM:, half:])
    permt = permt_ref[...]
    ya = jnp.concatenate(
        [_dot(permt, ya[c * T_CHUNK:(c + 1) * T_CHUNK]).astype(BF16)
         for c in range(x_ref.shape[0] // T_CHUNK)], axis=0)
    x1 = jnp.concatenate([x1_lo, x1_hi], axis=1) + _dot(ya, w_ref[0:D_SSM, :])
    x1_ref[...] = x1
    h_ref[...] = _rms(x1, g_ref[...]).astype(BF16)


def _out_proj(x2, y, yb, w_out, g, glu_w, glu_b, og, permt):
    n = x2.shape[0]
    row = lambda i: (i, 0)
    return pl.pallas_call(
        _out_proj_kernel,
        name="out_proj",
        grid=(n // ROWS_IN,),
        in_specs=[
            pl.BlockSpec((ROWS_IN, D_MODEL), row),
            pl.BlockSpec((ROWS_IN, D_SSM), row),
            pl.BlockSpec((ROWS_IN, D_SGU), row),
            _resident(w_out.shape),
            _resident((1, D_MODEL)),
            _resident(glu_w.shape), _resident(glu_b.shape), _resident(og.shape),
            _resident(permt.shape),
        ],
        out_specs=(pl.BlockSpec((ROWS_IN, D_MODEL), row), pl.BlockSpec((ROWS_IN, D_MODEL), row)),
        out_shape=(jax.ShapeDtypeStruct((n, D_MODEL), F32), jax.ShapeDtypeStruct((n, D_MODEL), BF16)),
        compiler_params=pltpu.CompilerParams(
            dimension_semantics=("arbitrary",), vmem_limit_bytes=VMEM_LIMIT),
    )(x2, y, yb, w_out, g, glu_w, glu_b, og, permt)


def _ffn_kernel(h_ref, x1_ref, wg_ref, wu_ref, wo_ref, out_ref):
    @pl.when(pl.program_id(1) == 0)
    def _():
        out_ref[...] = x1_ref[...]

    h = h_ref[...]
    act = jax.nn.silu(_dot(h, wg_ref[...])) * _dot(h, wu_ref[...])
    out_ref[...] += _dot(act.astype(BF16), wo_ref[...])


def _ffn(h, x1, w_in, w_out):
    n = h.shape[0]
    nf = D_FFN // FFN_BLK
    return pl.pallas_call(
        _ffn_kernel,
        name="ffn",
        grid=(n // ROWS_FFN, nf),
        in_specs=[
            pl.BlockSpec((ROWS_FFN, D_MODEL), lambda i, f: (i, 0)),
            pl.BlockSpec((ROWS_FFN, D_MODEL), lambda i, f: (i, 0)),
            pl.BlockSpec((D_MODEL, FFN_BLK), lambda i, f: (0, f)),
            pl.BlockSpec((D_MODEL, FFN_BLK), lambda i, f: (0, f + nf)),
            pl.BlockSpec((FFN_BLK, D_MODEL), lambda i, f: (f, 0)),
        ],
        out_specs=pl.BlockSpec((ROWS_FFN, D_MODEL), lambda i, f: (i, 0)),
        out_shape=jax.ShapeDtypeStruct((n, D_MODEL), F32),
        compiler_params=pltpu.CompilerParams(
            dimension_semantics=("arbitrary", "arbitrary"), vmem_limit_bytes=VMEM_LIMIT_FFN),
    )(h, x1, w_in, w_in, w_out)


def _ple_kernel(x_ref, p_ref, wg_ref, bg_ref, wp_ref, g_ref, gf_ref, out_ref, *, final):
    x = x_ref[...]
    h = _rms(x, g_ref[...]).astype(BF16)
    pe = _dot(p_ref[...].astype(BF16), wp_ref[...])
    gate = jax.nn.sigmoid(_dot(h, wg_ref[...]) + bg_ref[...])
    x3 = x + gate * pe
    out_ref[...] = _rms(x3, gf_ref[...]) if final else x3


def _ple(x2, p2, w_gate, b_gate, w_proj, g, gf, final):
    n = x2.shape[0]
    row = lambda i: (i, 0)
    return pl.pallas_call(
        functools.partial(_ple_kernel, final=final),
        name="ple",
        grid=(n // ROWS_IN,),
        in_specs=[
            pl.BlockSpec((ROWS_IN, D_MODEL), row),
            pl.BlockSpec((ROWS_IN, PLE_DIM), row),
            _resident(w_gate.shape),
            _resident((1, D_MODEL)),
            _resident(w_proj.shape),
            _resident((1, D_MODEL)),
            _resident((1, D_MODEL)),
        ],
        out_specs=pl.BlockSpec((ROWS_IN, D_MODEL), row),
        out_shape=jax.ShapeDtypeStruct((n, D_MODEL), F32),
        compiler_params=pltpu.CompilerParams(
            dimension_semantics=("arbitrary",), vmem_limit_bytes=VMEM_LIMIT),
    )(x2, p2, w_gate, b_gate, w_proj, g, gf)


def _segment_permutation():
    r = np.arange(T_CHUNK)
    src = (r % SUBLANES) * K_STEPS + r // SUBLANES
    perm = np.zeros((T_CHUNK, T_CHUNK), np.float32)
    perm[r, src] = 1.0
    return perm


def _scan_exponents():
    n = [1.0]
    n += [float(K_STEPS * m) for m in (1, 2, 4)]
    n += [float(K_STEPS * (i + 1)) for i in range(SUBLANES)]
    return np.asarray(n, np.float32).reshape(-1, 1)


def kernel(x, p, norm_mix_g, w_in, ssm_lambda_re, ssm_lambda_im, ssm_log_step, ssm_b_re, ssm_b_im, ssm_c_re, ssm_c_im, ssm_d, ssm_glu_w, ssm_glu_b, sgu_ln_g, sgu_ln_b, sgu_w, sgu_b, out_norm_ssm_g, out_norm_sgu_g, w_out, norm_ffn_g, w_ffn_in, w_ffn_out, norm_ple_g, w_ple_gate, b_ple_gate, w_ple_proj, final_norm_g):
    bsz, seq, _ = x.shape
    depth = w_in.shape[0]
    perm_np = _segment_permutation()
    perm = jnp.asarray(perm_np, BF16)
    permt = jnp.asarray(perm_np.T, BF16)
    exponents = jnp.asarray(_scan_exponents())
    x2 = x.reshape(bsz * seq, D_MODEL)
    vec = lambda a: a.reshape(1, -1)
    for i in range(depth):
        tab, bd, cd = _s5_params(ssm_lambda_re[i], ssm_lambda_im[i], ssm_log_step[i], ssm_b_re[i],
                                 ssm_b_im[i], ssm_c_re[i], ssm_c_im[i], exponents)

        zs, yb, w_out_bf, w_gate_bf, glu_bf, w_ffn_out_bf = _in_proj(
            x2, vec(norm_mix_g[i]), w_in[i].astype(BF16), perm, vec(sgu_ln_g[i]), vec(sgu_ln_b[i]),
            sgu_w[i].astype(BF16), sgu_b[i].T, vec(out_norm_sgu_g[i]),
            to_cast=(w_out[i], w_ple_gate[i], ssm_glu_w[i], w_ffn_out[i]))
        y, w_ffn_in_bf = _s5(zs, bsz, seq, bd, cd, tab, vec(ssm_d[i]), to_cast=w_ffn_in[i])
        x1, h = _out_proj(x2, y, yb, w_out_bf, vec(norm_ffn_g[i]), glu_bf, vec(ssm_glu_b[i]),
                          vec(out_norm_ssm_g[i]), permt)
        x2 = _ffn(h, x1, w_ffn_in_bf, w_ffn_out_bf)
        x2 = _ple(x2, p[i].reshape(bsz * seq, PLE_DIM), w_gate_bf, vec(b_ple_gate[i]),
                  w_ple_proj[i].astype(BF16), vec(norm_ple_g[i]), vec(final_norm_g),
                  final=(i == depth - 1))
    return x2.reshape(bsz, seq, D_MODEL)
```

```python
import functools

import numpy as np
import jax
import jax.numpy as jnp
from jax import lax
from jax.experimental import pallas as pl
from jax.experimental.pallas import tpu as pltpu

D_MODEL = 2048
D_SSM = 1024
D_SGU = 1024
SSM_GROUP = 16
SSM_GROUPS = 64
SSM_STATE = 64
SGU_CHUNK = 128
SGU_HEADS = 8
SGU_HEAD_DIM = D_SGU // SGU_HEADS
D_FFN = 5632
PLE_DIM = 256
EPS = 1e-6
LAMBDA_RE_MAX = -1e-4

SUBLANES = 8
N_STATE = SSM_GROUPS * SSM_STATE
N_BLK = 4
BLK_CH = D_SSM // N_BLK
BLK_ST = N_STATE // N_BLK
BLK_W = 2 * BLK_ST
T_CHUNK = 256
K_STEPS = T_CHUNK // SUBLANES
S5_ROWS = 512
SCAN_W = 512
N_GRP = BLK_ST // SCAN_W
N_PIECE = 2 * N_GRP

ROWS_IN = 512
ROWS_FFN = 1024
FFN_BLK = 512
VMEM_LIMIT = 56 * 1024 * 1024
VMEM_LIMIT_FFN = 60 * 1024 * 1024

F32 = jnp.float32
BF16 = jnp.bfloat16


def _rms(x, g):
    r = lax.rsqrt(jnp.mean(x * x, axis=-1, keepdims=True) + EPS)
    return (x * r) * g


def _dot(a, b):
    return jnp.dot(a, b, preferred_element_type=F32)


def _resident(shape):
    nd = len(shape)
    return pl.BlockSpec(shape, lambda *_: (0,) * nd, pipeline_mode=pl.Buffered(1))


def _cast_block_spec(w, steps, index):
    return pl.BlockSpec((w.shape[0] // steps, w.shape[1]), index)


def _cast_out_shape(w):
    return jax.ShapeDtypeStruct(w.shape, BF16)


def _s5_params_kernel(lr_ref, li_ref, ls_ref, n_ref, bre_ref, bim_ref, cre_ref, cim_ref, w_ref,
                      tab_ref, bd_ref, cd_ref, w_out):
    w_out[...] = w_ref[...].astype(BF16)

    g = lax.rem(pl.program_id(0), N_GRP)
    lr = jnp.minimum(lr_ref[...], LAMBDA_RE_MAX)
    li = li_ref[...]
    dt = jnp.exp(ls_ref[...])
    n = n_ref[...]
    mag = jnp.exp(n * (lr * dt))
    ang = n * (li * dt)
    tab_ref[0, 0] = jnp.concatenate([mag * jnp.cos(ang), mag * jnp.sin(ang)], axis=1)
    mag1 = jnp.exp(lr * dt)
    ang1 = li * dt
    nr = mag1 * jnp.cos(ang1) - 1.0
    ni = mag1 * jnp.sin(ang1)
    den = lr * lr + li * li
    q_re = (nr * lr + ni * li) / den
    q_im = (ni * lr - nr * li) / den
    bre = bre_ref[...]
    bim = bim_ref[...]
    bbar = (q_re * bre - q_im * bim, q_re * bim + q_im * bre)

    iota = lambda shape, dim: lax.broadcasted_iota(jnp.int32, shape, dim)
    grp_of_ch = lambda i: lax.shift_right_logical(i, 4)
    grp_of_st = lambda i: lax.shift_right_logical(i, 6)
    grp_per_slice = SCAN_W // SSM_STATE
    first_grp = g * grp_per_slice
    same_b = grp_of_ch(iota((BLK_CH, SCAN_W), 0)) == grp_of_st(iota((BLK_CH, SCAN_W), 1)) + first_grp
    same_c = grp_of_st(iota((SCAN_W, BLK_CH), 0)) + first_grp == grp_of_ch(iota((SCAN_W, BLK_CH), 1))
    spread = (iota((SSM_GROUP, BLK_CH), 0)
              == lax.bitwise_and(iota((SSM_GROUP, BLK_CH), 1), SSM_GROUP - 1)).astype(BF16)
    for part in range(2):
        tiled = jnp.concatenate([bbar[part]] * SSM_GROUP, axis=0)
        bd_ref[0, part] = jnp.where(same_b, tiled, 0.0).astype(BF16)
        c_ref = (cre_ref, cim_ref)[part]
        wide = jnp.concatenate(
            [lax.dot_general(c_ref[q * SSM_GROUP:(q + 1) * SSM_GROUP, :].astype(BF16), spread,
                             (((0,), (0,)), ((), ())), preferred_element_type=F32)
             for q in range(grp_per_slice)], axis=0)
        wide = wide if part == 0 else -wide
        cd_ref[0, part] = jnp.where(same_c, wide, 0.0).astype(BF16)


def _s5_params(lam_re, lam_im, log_step, b_re, b_im, c_re, c_im, exponents, w_in):
    n_rows = exponents.shape[0]
    steps = N_BLK * N_GRP
    lr = lam_re.reshape(1, N_STATE)
    li = lam_im.reshape(1, N_STATE)
    ls = jnp.repeat(log_step, SSM_STATE).reshape(1, N_STATE)
    bre = b_re.transpose(2, 0, 1).reshape(SSM_GROUP, N_STATE)
    bim = b_im.transpose(2, 0, 1).reshape(SSM_GROUP, N_STATE)
    cre = c_re.reshape(SSM_GROUPS * SSM_GROUP, SSM_STATE)
    cim = c_im.reshape(SSM_GROUPS * SSM_GROUP, SSM_STATE)
    lanes = lambda rows: pl.BlockSpec((rows, SCAN_W), lambda s: (0, s))
    c_rows = pl.BlockSpec((SCAN_W // SSM_STATE * SSM_GROUP, SSM_STATE), lambda s: (s, 0))
    slab = lambda shape: pl.BlockSpec((1,) + shape, lambda s: (s // N_GRP, s % N_GRP, 0, 0))
    w_block = pl.BlockSpec((w_in.shape[0] // steps, w_in.shape[1]), lambda s: (s, 0))
    return pl.pallas_call(
        _s5_params_kernel,
        name="s5_params",
        grid=(steps,),
        in_specs=[lanes(1), lanes(1), lanes(1), _resident(exponents.shape),
                  lanes(SSM_GROUP), lanes(SSM_GROUP), c_rows, c_rows, w_block],
        out_specs=(slab((1, n_rows, 2 * SCAN_W)), slab((2, BLK_CH, SCAN_W)), slab((2, SCAN_W, BLK_CH)),
                   w_block),
        out_shape=(jax.ShapeDtypeStruct((N_BLK, N_GRP, n_rows, 2 * SCAN_W), F32),
                   jax.ShapeDtypeStruct((N_BLK, N_PIECE, BLK_CH, SCAN_W), BF16),
                   jax.ShapeDtypeStruct((N_BLK, N_PIECE, SCAN_W, BLK_CH), BF16),
                   jax.ShapeDtypeStruct(w_in.shape, BF16)),
        compiler_params=pltpu.CompilerParams(
            dimension_semantics=("arbitrary",), vmem_limit_bytes=VMEM_LIMIT),
    )(lr, li, ls, exponents, bre, bim, cre, cim, w_in)


def _in_proj_kernel(x_ref, g_ref, w_ref, perm_ref, lng_ref, lnb_ref, sw_ref, sbt_ref, og_ref,
                    c0_ref, c1_ref, c2_ref, c3_ref,
                    zs_ref, yb_ref, c0_out, c1_out, c2_out, c3_out, s_scr):
    c0_out[...] = c0_ref[...].astype(BF16)
    c1_out[...] = c1_ref[...].astype(BF16)
    c2_out[...] = c2_ref[...].astype(BF16)
    c3_out[...] = c3_ref[...].astype(BF16)
    rows = x_ref.shape[0]
    h = _rms(x_ref[...], g_ref[...]).astype(BF16)
    gv = jax.nn.gelu(_dot(h, w_ref[:, D_SSM + D_SGU:]))
    mu = jnp.mean(gv, axis=-1, keepdims=True)
    xc = gv - mu
    r = lax.rsqrt(jnp.mean(xc * xc, axis=-1, keepdims=True) + EPS)
    v = ((xc * r) * lng_ref[...] + lnb_ref[...]).astype(BF16)
    u = jax.nn.gelu(_dot(h, w_ref[:, D_SSM:D_SSM + D_SGU]))
    ti = lax.broadcasted_iota(jnp.int32, (SGU_CHUNK, SGU_CHUNK), 0)
    si = lax.broadcasted_iota(jnp.int32, (SGU_CHUNK, SGU_CHUNK), 1)
    causal = si <= ti
    sbt = sbt_ref[...]
    for hd in range(SGU_HEADS):
        wm = jnp.where(causal, sw_ref[hd], jnp.zeros((), BF16))
        cols = slice(hd * SGU_HEAD_DIM, (hd + 1) * SGU_HEAD_DIM)
        bias = sbt[:, hd:hd + 1]
        for c in range(rows // SGU_CHUNK):
            rs = slice(c * SGU_CHUNK, (c + 1) * SGU_CHUNK)
            s_scr[rs, cols] = _dot(wm, v[rs, cols]) + bias
    yb = u * s_scr[...]
    yb_ref[...] = _rms(yb, og_ref[...]).astype(BF16)
    perm = perm_ref[...]
    hp = jnp.concatenate(
        [_dot(perm, h[c * T_CHUNK:(c + 1) * T_CHUNK]).astype(BF16) for c in range(rows // T_CHUNK)],
        axis=0)
    zs_ref[...] = _dot(hp, w_ref[:, 0:D_SSM])


def _in_proj(x2, g, w_in, perm, ln_g, ln_b, sgu_w, sgu_bt, og, to_cast):
    n = x2.shape[0]
    steps = n // ROWS_IN
    row = lambda i: (i, 0)
    return pl.pallas_call(
        _in_proj_kernel,
        name="in_proj",
        grid=(n // ROWS_IN,),
        in_specs=[
            pl.BlockSpec((ROWS_IN, D_MODEL), row),
            _resident((1, D_MODEL)),
            _resident(w_in.shape),
            _resident(perm.shape),
            _resident((1, D_SGU)),
            _resident((1, D_SGU)),
            _resident(sgu_w.shape),
            _resident(sgu_bt.shape),
            _resident((1, D_SGU)),
        ] + [_cast_block_spec(w, steps, row) for w in to_cast],
        out_specs=(pl.BlockSpec((ROWS_IN, D_SSM), row), pl.BlockSpec((ROWS_IN, D_SGU), row))
        + tuple(_cast_block_spec(w, steps, row) for w in to_cast),
        out_shape=(jax.ShapeDtypeStruct((n, D_SSM), F32), jax.ShapeDtypeStruct((n, D_SGU), BF16))
        + tuple(_cast_out_shape(w) for w in to_cast),
        scratch_shapes=[pltpu.VMEM((ROWS_IN, D_SGU), F32)],
        compiler_params=pltpu.CompilerParams(
            dimension_semantics=("arbitrary",), vmem_limit_bytes=VMEM_LIMIT),
    )(x2, g, w_in, perm, ln_g, ln_b, sgu_w, sgu_bt, og, *to_cast)


def _s5_kernel(zp_ref, zn_ref, bd_ref, cd_ref, tab_ref, d_ref, c0_ref,
               out_ref, c0_out, bu_a, bu_b, sb_a, sb_b, carry_ref):
    c0_out[...] = c0_ref[...].astype(BF16)

    @pl.when(pl.program_id(1) == 0)
    def _():
        carry_ref[...] = jnp.zeros_like(carry_ref)

    zp = zp_ref[...]
    zpb = zp.astype(BF16)
    znb = zn_ref[...].astype(BF16)
    bu = (bu_a, bu_b)
    sb = (sb_a, sb_b)
    seg = lax.broadcasted_iota(jnp.int32, (SUBLANES, SCAN_W), 0)
    zero = jnp.zeros((SUBLANES, SCAN_W), F32)
    blk = lambda j: slice(j * BLK_CH, (j + 1) * BLK_CH)

    @pl.when((pl.program_id(0) == 0) & (pl.program_id(1) == 0))
    def _():
        for p in range(N_PIECE):
            bu[0][p] = _dot(zpb[:, blk(0)], bd_ref[0, p])

    for j in range(N_BLK):
        out_ref[:, blk(j)] = d_ref[:, blk(j)] * zp[:, blk(j)]

    for j in range(N_BLK):
        cur, oth = j % 2, (j + 1) % 2

        def body(g, _, j=j, cur=cur, oth=oth):
            for part in range(2):
                p = 2 * g + part
                if j + 1 < N_BLK:
                    bu[oth][p] = _dot(zpb[:, blk(j + 1)], bd_ref[j + 1, p])
                else:
                    bu[oth][p] = _dot(znb, bd_ref[0, p])
                if j >= 1:
                    out_ref[:, blk(j - 1)] += _dot(sb[oth][p], cd_ref[j - 1, p])

            re_p, im_p = 2 * g, 2 * g + 1
            tre, tim = slice(0, SCAN_W), slice(SCAN_W, 2 * SCAN_W)
            ar = jnp.broadcast_to(tab_ref[j, g, 0:1, tre], (SUBLANES, SCAN_W))
            ai = jnp.broadcast_to(tab_ref[j, g, 0:1, tim], (SUBLANES, SCAN_W))

            def step(row0, sr, si):
                rows = pl.ds(row0, SUBLANES)
                return (ar * sr - ai * si + bu[cur][re_p, rows, :],
                        ar * si + ai * sr + bu[cur][im_p, rows, :])

            def start_states(xr, xi):
                for lvl, m in enumerate((1, 2, 4)):
                    mr = tab_ref[j, g, 1 + lvl:2 + lvl, tre]
                    mi = tab_ref[j, g, 1 + lvl:2 + lvl, tim]
                    rr = pltpu.roll(xr, m, 0)
                    ri = pltpu.roll(xi, m, 0)
                    keep = seg >= m
                    xr, xi = (xr + jnp.where(keep, mr * rr - mi * ri, 0.0),
                              xi + jnp.where(keep, mr * ri + mi * rr, 0.0))
                cr = carry_ref[j, g, :, tre]
                ci = carry_ref[j, g, :, tim]
                pr = tab_ref[j, g, 4:4 + SUBLANES, tre]
                pi = tab_ref[j, g, 4:4 + SUBLANES, tim]
                xr, xi = xr + pr * cr - pi * ci, xi + pr * ci + pi * cr
                carry_ref[j, g, :, tre] = jnp.broadcast_to(xr[SUBLANES - 1:SUBLANES], (SUBLANES, SCAN_W))
                carry_ref[j, g, :, tim] = jnp.broadcast_to(xi[SUBLANES - 1:SUBLANES], (SUBLANES, SCAN_W))
                return (jnp.where(seg >= 1, pltpu.roll(xr, 1, 0), cr),
                        jnp.where(seg >= 1, pltpu.roll(xi, 1, 0), ci))

            def emit_pair(base, m, sr, si):
                r1, i1 = step(base + 2 * m * SUBLANES, sr, si)
                sr, si = step(base + (2 * m + 1) * SUBLANES, r1, i1)
                rows = pl.ds(base + m * 2 * SUBLANES, 2 * SUBLANES)
                sb[cur][re_p, rows, :] = jnp.concatenate([r1, sr], axis=0).astype(BF16)
                sb[cur][im_p, rows, :] = jnp.concatenate([i1, si], axis=0).astype(BF16)
                return sr, si

            n_chunk = S5_ROWS // T_CHUNK
            state = None
            for c in range(n_chunk + 1):
                xr, xi = zero, zero
                for m in range(K_STEPS // 2):
                    if c < n_chunk:
                        xr, xi = step(c * T_CHUNK + 2 * m * SUBLANES, xr, xi)
                        xr, xi = step(c * T_CHUNK + (2 * m + 1) * SUBLANES, xr, xi)
                    if c >= 1:
                        state = emit_pair((c - 1) * T_CHUNK, m, *state)
                if c < n_chunk:
                    state = start_states(xr, xi)
            return 0

        lax.fori_loop(0, N_GRP, body, 0)

    last = N_BLK - 1
    for p in range(N_PIECE):
        out_ref[:, blk(last)] += _dot(sb[last % 2][p], cd_ref[last, p])


def _s5(zs, bsz, seq, bd, cd, tab, dvec, to_cast):
    steps = seq // S5_ROWS
    row = lambda b, c: (b * steps + c, 0)
    nxt = lambda b, c: (jnp.minimum(b * steps + c + 1, bsz * steps - 1), 0)
    return pl.pallas_call(
        _s5_kernel,
        name="s5",
        grid=(bsz, steps),
        in_specs=[
            pl.BlockSpec((S5_ROWS, D_SSM), row),
            pl.BlockSpec((S5_ROWS, BLK_CH), nxt),
            _resident(bd.shape), _resident(cd.shape), _resident(tab.shape),
            _resident(dvec.shape),
            _cast_block_spec(to_cast, bsz * steps, row),
        ],
        out_specs=(pl.BlockSpec((S5_ROWS, D_SSM), row), _cast_block_spec(to_cast, bsz * steps, row)),
        out_shape=(jax.ShapeDtypeStruct((bsz * seq, D_SSM), F32), _cast_out_shape(to_cast)),
        scratch_shapes=[
            pltpu.VMEM((N_PIECE, S5_ROWS, SCAN_W), F32),
            pltpu.VMEM((N_PIECE, S5_ROWS, SCAN_W), F32),
            pltpu.VMEM((N_PIECE, S5_ROWS, SCAN_W), BF16),
            pltpu.VMEM((N_PIECE, S5_ROWS, SCAN_W), BF16),
            pltpu.VMEM((N_BLK, N_GRP, SUBLANES, 2 * SCAN_W), F32),
        ],
        compiler_params=pltpu.CompilerParams(
            dimension_semantics=("arbitrary", "arbitrary"), vmem_limit_bytes=VMEM_LIMIT),
    )(zs, zs, bd, cd, tab, dvec, to_cast)


def _out_proj_kernel(x_ref, y_ref, yb_ref, w_ref, g_ref, gluw_ref, glub_ref, og_ref, permt_ref,
                     x1_ref, h_ref):
    half = D_MODEL // 2
    yb = yb_ref[...]
    y = jax.nn.gelu(y_ref[...])
    x1_lo = x_ref[:, 0:half] + _dot(yb, w_ref[D_SSM:, 0:half])
    gate = jax.nn.sigmoid(_dot(y.astype(BF16), gluw_ref[...]) + glub_ref[...])
    ya = _rms(y * gate, og_ref[...]).astype(BF16)
    x1_hi = x_ref[:, half:] + _dot(yb, w_ref[D_SSM:, half:])
    permt = permt_ref[...]
    ya = jnp.concatenate(
        [_dot(permt, ya[c * T_CHUNK:(c + 1) * T_CHUNK]).astype(BF16)
         for c in range(x_ref.shape[0] // T_CHUNK)], axis=0)
    x1 = jnp.concatenate([x1_lo, x1_hi], axis=1) + _dot(ya, w_ref[0:D_SSM, :])
    x1_ref[...] = x1
    h_ref[...] = _rms(x1, g_ref[...]).astype(BF16)


def _out_proj(x2, y, yb, w_out, g, glu_w, glu_b, og, permt):
    n = x2.shape[0]
    row = lambda i: (i, 0)
    return pl.pallas_call(
        _out_proj_kernel,
        name="out_proj",
        grid=(n // ROWS_IN,),
        in_specs=[
            pl.BlockSpec((ROWS_IN, D_MODEL), row),
            pl.BlockSpec((ROWS_IN, D_SSM), row),
            pl.BlockSpec((ROWS_IN, D_SGU), row),
            _resident(w_out.shape),
            _resident((1, D_MODEL)),
            _resident(glu_w.shape), _resident(glu_b.shape), _resident(og.shape),
            _resident(permt.shape),
        ],
        out_specs=(pl.BlockSpec((ROWS_IN, D_MODEL), row), pl.BlockSpec((ROWS_IN, D_MODEL), row)),
        out_shape=(jax.ShapeDtypeStruct((n, D_MODEL), F32), jax.ShapeDtypeStruct((n, D_MODEL), BF16)),
        compiler_params=pltpu.CompilerParams(
            dimension_semantics=("arbitrary",), vmem_limit_bytes=VMEM_LIMIT),
    )(x2, y, yb, w_out, g, glu_w, glu_b, og, permt)


def _ffn_kernel(h_ref, x1_ref, wg_ref, wu_ref, wo_ref, out_ref):
    @pl.when(pl.program_id(1) == 0)
    def _():
        out_ref[...] = x1_ref[...]

    h = h_ref[...]
    act = jax.nn.silu(_dot(h, wg_ref[...])) * _dot(h, wu_ref[...])
    out_ref[...] += _dot(act.astype(BF16), wo_ref[...])


def _ffn(h, x1, w_in, w_out):
    n = h.shape[0]
    nf = D_FFN // FFN_BLK
    return pl.pallas_call(
        _ffn_kernel,
        name="ffn",
        grid=(n // ROWS_FFN, nf),
        in_specs=[
            pl.BlockSpec((ROWS_FFN, D_MODEL), lambda i, f: (i, 0)),
            pl.BlockSpec((ROWS_FFN, D_MODEL), lambda i, f: (i, 0)),
            pl.BlockSpec((D_MODEL, FFN_BLK), lambda i, f: (0, f)),
            pl.BlockSpec((D_MODEL, FFN_BLK), lambda i, f: (0, f + nf)),
            pl.BlockSpec((FFN_BLK, D_MODEL), lambda i, f: (f, 0)),
        ],
        out_specs=pl.BlockSpec((ROWS_FFN, D_MODEL), lambda i, f: (i, 0)),
        out_shape=jax.ShapeDtypeStruct((n, D_MODEL), F32),
        compiler_params=pltpu.CompilerParams(
            dimension_semantics=("arbitrary", "arbitrary"), vmem_limit_bytes=VMEM_LIMIT_FFN),
    )(h, x1, w_in, w_in, w_out)


def _ple_kernel(x_ref, p_ref, wg_ref, bg_ref, wp_ref, g_ref, gf_ref, out_ref, *, final):
    x = x_ref[...]
    h = _rms(x, g_ref[...]).astype(BF16)
    pe = _dot(p_ref[...].astype(BF16), wp_ref[...])
    gate = jax.nn.sigmoid(_dot(h, wg_ref[...]) + bg_ref[...])
    x3 = x + gate * pe
    out_ref[...] = _rms(x3, gf_ref[...]) if final else x3


def _ple(x2, p2, w_gate, b_gate, w_proj, g, gf, final):
    n = x2.shape[0]
    row = lambda i: (i, 0)
    return pl.pallas_call(
        functools.partial(_ple_kernel, final=final),
        name="ple",
        grid=(n // ROWS_IN,),
        in_specs=[
            pl.BlockSpec((ROWS_IN, D_MODEL), row),
            pl.BlockSpec((ROWS_IN, PLE_DIM), row),
            _resident(w_gate.shape),
            _resident((1, D_MODEL)),
            _resident(w_proj.shape),
            _resident((1, D_MODEL)),
            _resident((1, D_MODEL)),
        ],
        out_specs=pl.BlockSpec((ROWS_IN, D_MODEL), row),
        out_shape=jax.ShapeDtypeStruct((n, D_MODEL), F32),
        compiler_params=pltpu.CompilerParams(
            dimension_semantics=("arbitrary",), vmem_limit_bytes=VMEM_LIMIT),
    )(x2, p2, w_gate, b_gate, w_proj, g, gf)


def _segment_permutation():
    r = np.arange(T_CHUNK)
    src = (r % SUBLANES) * K_STEPS + r // SUBLANES
    perm = np.zeros((T_CHUNK, T_CHUNK), np.float32)
    perm[r, src] = 1.0
    return perm


def _scan_exponents():
    n = [1.0]
    n += [float(K_STEPS * m) for m in (1, 2, 4)]
    n += [float(K_STEPS * (i + 1)) for i in range(SUBLANES)]
    return np.asarray(n, np.float32).reshape(-1, 1)


def kernel(x, p, norm_mix_g, w_in, ssm_lambda_re, ssm_lambda_im, ssm_log_step, ssm_b_re, ssm_b_im, ssm_c_re, ssm_c_im, ssm_d, ssm_glu_w, ssm_glu_b, sgu_ln_g, sgu_ln_b, sgu_w, sgu_b, out_norm_ssm_g, out_norm_sgu_g, w_out, norm_ffn_g, w_ffn_in, w_ffn_out, norm_ple_g, w_ple_gate, b_ple_gate, w_ple_proj, final_norm_g):
    bsz, seq, _ = x.shape
    depth = w_in.shape[0]
    perm_np = _segment_permutation()
    perm = jnp.asarray(perm_np, BF16)
    permt = jnp.asarray(perm_np.T, BF16)
    exponents = jnp.asarray(_scan_exponents())
    x2 = x.reshape(bsz * seq, D_MODEL)
    vec = lambda a: a.reshape(1, -1)
    for i in range(depth):
        tab, bd, cd, w_in_bf = _s5_params(
            ssm_lambda_re[i], ssm_lambda_im[i], ssm_log_step[i], ssm_b_re[i], ssm_b_im[i],
            ssm_c_re[i], ssm_c_im[i], exponents, w_in[i])

        zs, yb, w_out_bf, w_gate_bf, glu_bf, w_ffn_out_bf = _in_proj(
            x2, vec(norm_mix_g[i]), w_in_bf, perm, vec(sgu_ln_g[i]), vec(sgu_ln_b[i]),
            sgu_w[i].astype(BF16), sgu_b[i].T, vec(out_norm_sgu_g[i]),
            to_cast=(w_out[i], w_ple_gate[i], ssm_glu_w[i], w_ffn_out[i]))
        y, w_ffn_in_bf = _s5(zs, bsz, seq, bd, cd, tab, vec(ssm_d[i]), to_cast=w_ffn_in[i])
        x1, h = _out_proj(x2, y, yb, w_out_bf, vec(norm_ffn_g[i]), glu_bf, vec(ssm_glu_b[i]),
                          vec(out_norm_ssm_g[i]), permt)
        x2 = _ffn(h, x1, w_ffn_in_bf, w_ffn_out_bf)
        x2 = _ple(x2, p[i].reshape(bsz * seq, PLE_DIM), w_gate_bf, vec(b_ple_gate[i]),
                  w_ple_proj[i].astype(BF16), vec(norm_ple_g[i]), vec(final_norm_g),
                  final=(i == depth - 1))
    return x2.reshape(bsz, seq, D_MODEL)
```

```python
import functools

import numpy as np
import jax
import jax.numpy as jnp
from jax import lax
from jax.experimental import pallas as pl
from jax.experimental.pallas import tpu as pltpu

D_MODEL = 2048
D_SSM = 1024
D_SGU = 1024
SSM_GROUP = 16
SSM_GROUPS = 64
SSM_STATE = 64
SGU_CHUNK = 128
SGU_HEADS = 8
SGU_HEAD_DIM = D_SGU // SGU_HEADS
D_FFN = 5632
PLE_DIM = 256
EPS = 1e-6
LAMBDA_RE_MAX = -1e-4

SUBLANES = 8
N_STATE = SSM_GROUPS * SSM_STATE
N_BLK = 4
BLK_CH = D_SSM // N_BLK
BLK_ST = N_STATE // N_BLK
BLK_W = 2 * BLK_ST
T_CHUNK = 256
K_STEPS = T_CHUNK // SUBLANES
S5_ROWS = 512
SCAN_W = 512
N_GRP = BLK_ST // SCAN_W
N_PIECE = 2 * N_GRP

ROWS_IN = 512
ROWS_FFN = 1024
ROWS_PLE = 1024
FFN_BLK = 512
VMEM_LIMIT = 56 * 1024 * 1024
VMEM_LIMIT_FFN = 60 * 1024 * 1024

F32 = jnp.float32
BF16 = jnp.bfloat16


def _rms(x, g):
    r = lax.rsqrt(jnp.mean(x * x, axis=-1, keepdims=True) + EPS)
    return (x * r) * g


def _dot(a, b):
    return jnp.dot(a, b, preferred_element_type=F32)


def _resident(shape):
    nd = len(shape)
    return pl.BlockSpec(shape, lambda *_: (0,) * nd, pipeline_mode=pl.Buffered(1))


def _cast_block_spec(w, steps, index):
    return pl.BlockSpec((w.shape[0] // steps, w.shape[1]), index)


def _cast_out_shape(w):
    return jax.ShapeDtypeStruct(w.shape, BF16)


def _s5_params_kernel(lr_ref, li_ref, ls_ref, n_ref, bre_ref, bim_ref, cre_ref, cim_ref,
                      tab_ref, bd_ref, cd_ref):
    lr = jnp.minimum(lr_ref[...], LAMBDA_RE_MAX)
    li = li_ref[...]
    dt = jnp.exp(ls_ref[...])
    n = n_ref[...]
    mag = jnp.exp(n * (lr * dt))
    ang = n * (li * dt)
    pwr = mag * jnp.cos(ang)
    pwi = mag * jnp.sin(ang)
    mag1 = jnp.exp(lr * dt)
    ang1 = li * dt
    nr = mag1 * jnp.cos(ang1) - 1.0
    ni = mag1 * jnp.sin(ang1)
    den = lr * lr + li * li
    q_re = (nr * lr + ni * li) / den
    q_im = (ni * lr - nr * li) / den
    bre = bre_ref[...]
    bim = bim_ref[...]
    bbar = (q_re * bre - q_im * bim, q_re * bim + q_im * bre)

    iota = lambda shape, dim: lax.broadcasted_iota(jnp.int32, shape, dim)
    grp_of_ch = lambda i: lax.shift_right_logical(i, 4)
    grp_of_st = lambda i: lax.shift_right_logical(i, 6)
    b_row_grp = grp_of_ch(iota((BLK_CH, SCAN_W), 0))
    b_col_grp = grp_of_st(iota((BLK_CH, SCAN_W), 1))
    c_row_grp = grp_of_st(iota((SCAN_W, BLK_CH), 0))
    c_col_grp = grp_of_ch(iota((SCAN_W, BLK_CH), 1))
    spread = (iota((SSM_GROUP, BLK_CH), 0)
              == lax.bitwise_and(iota((SSM_GROUP, BLK_CH), 1), SSM_GROUP - 1)).astype(BF16)
    grp_per_half = SCAN_W // SSM_STATE
    for j in range(N_BLK):
        for g in range(N_GRP):
            st = slice((j * N_GRP + g) * SCAN_W, (j * N_GRP + g + 1) * SCAN_W)
            tab_ref[j, g] = jnp.concatenate([pwr[:, st], pwi[:, st]], axis=1)
            same_b = b_row_grp == b_col_grp + g * grp_per_half
            same_c = c_row_grp + g * grp_per_half == c_col_grp
            for part in range(2):
                tiled = jnp.concatenate([bbar[part][:, st]] * SSM_GROUP, axis=0)
                bd_ref[j, 2 * g + part] = jnp.where(same_b, tiled, 0.0).astype(BF16)
                c_ref = (cre_ref, cim_ref)[part]
                first = (j * N_GRP + g) * grp_per_half
                wide = jnp.concatenate(
                    [lax.dot_general(c_ref[(first + q) * SSM_GROUP:(first + q + 1) * SSM_GROUP, :].astype(BF16),
                                     spread, (((0,), (0,)), ((), ())), preferred_element_type=F32)
                     for q in range(grp_per_half)], axis=0)
                wide = wide if part == 0 else -wide
                cd_ref[j, 2 * g + part] = jnp.where(same_c, wide, 0.0).astype(BF16)


def _s5_params(lam_re, lam_im, log_step, b_re, b_im, c_re, c_im, exponents):
    n_rows = exponents.shape[0]
    lr = lam_re.reshape(1, N_STATE)
    li = lam_im.reshape(1, N_STATE)
    ls = jnp.repeat(log_step, SSM_STATE).reshape(1, N_STATE)
    bre = b_re.transpose(2, 0, 1).reshape(SSM_GROUP, N_STATE)
    bim = b_im.transpose(2, 0, 1).reshape(SSM_GROUP, N_STATE)
    cre = c_re.reshape(SSM_GROUPS * SSM_GROUP, SSM_STATE)
    cim = c_im.reshape(SSM_GROUPS * SSM_GROUP, SSM_STATE)
    return pl.pallas_call(
        _s5_params_kernel,
        name="s5_params",
        out_shape=(jax.ShapeDtypeStruct((N_BLK, N_GRP, n_rows, 2 * SCAN_W), F32),
                   jax.ShapeDtypeStruct((N_BLK, N_PIECE, BLK_CH, SCAN_W), BF16),
                   jax.ShapeDtypeStruct((N_BLK, N_PIECE, SCAN_W, BLK_CH), BF16)),
        compiler_params=pltpu.CompilerParams(vmem_limit_bytes=VMEM_LIMIT),
    )(lr, li, ls, exponents, bre, bim, cre, cim)


def _in_proj_kernel(x_ref, g_ref, w_ref, perm_ref, lng_ref, lnb_ref, sw_ref, sbt_ref, og_ref,
                    c0_ref, c1_ref, c2_ref, c3_ref,
                    zs_ref, yb_ref, c0_out, c1_out, c2_out, c3_out, s_scr):
    c0_out[...] = c0_ref[...].astype(BF16)
    c1_out[...] = c1_ref[...].astype(BF16)
    c2_out[...] = c2_ref[...].astype(BF16)
    c3_out[...] = c3_ref[...].astype(BF16)
    rows = x_ref.shape[0]
    h = _rms(x_ref[...], g_ref[...]).astype(BF16)
    gv = jax.nn.gelu(_dot(h, w_ref[:, D_SSM + D_SGU:]))
    mu = jnp.mean(gv, axis=-1, keepdims=True)
    xc = gv - mu
    r = lax.rsqrt(jnp.mean(xc * xc, axis=-1, keepdims=True) + EPS)
    v = ((xc * r) * lng_ref[...] + lnb_ref[...]).astype(BF16)
    u = jax.nn.gelu(_dot(h, w_ref[:, D_SSM:D_SSM + D_SGU]))
    ti = lax.broadcasted_iota(jnp.int32, (SGU_CHUNK, SGU_CHUNK), 0)
    si = lax.broadcasted_iota(jnp.int32, (SGU_CHUNK, SGU_CHUNK), 1)
    causal = si <= ti
    sbt = sbt_ref[...]
    for hd in range(SGU_HEADS):
        wm = jnp.where(causal, sw_ref[hd], jnp.zeros((), BF16))
        cols = slice(hd * SGU_HEAD_DIM, (hd + 1) * SGU_HEAD_DIM)
        bias = sbt[:, hd:hd + 1]
        for c in range(rows // SGU_CHUNK):
            rs = slice(c * SGU_CHUNK, (c + 1) * SGU_CHUNK)
            s_scr[rs, cols] = _dot(wm, v[rs, cols]) + bias
    yb = u * s_scr[...]
    yb_ref[...] = _rms(yb, og_ref[...]).astype(BF16)
    perm = perm_ref[...]
    hp = jnp.concatenate(
        [_dot(perm, h[c * T_CHUNK:(c + 1) * T_CHUNK]).astype(BF16) for c in range(rows // T_CHUNK)],
        axis=0)
    zs_ref[...] = _dot(hp, w_ref[:, 0:D_SSM])


def _in_proj(x2, g, w_in, perm, ln_g, ln_b, sgu_w, sgu_bt, og, to_cast):
    n = x2.shape[0]
    steps = n // ROWS_IN
    row = lambda i: (i, 0)
    return pl.pallas_call(
        _in_proj_kernel,
        name="in_proj",
        grid=(n // ROWS_IN,),
        in_specs=[
            pl.BlockSpec((ROWS_IN, D_MODEL), row),
            _resident((1, D_MODEL)),
            _resident(w_in.shape),
            _resident(perm.shape),
            _resident((1, D_SGU)),
            _resident((1, D_SGU)),
            _resident(sgu_w.shape),
            _resident(sgu_bt.shape),
            _resident((1, D_SGU)),
        ] + [_cast_block_spec(w, steps, row) for w in to_cast],
        out_specs=(pl.BlockSpec((ROWS_IN, D_SSM), row), pl.BlockSpec((ROWS_IN, D_SGU), row))
        + tuple(_cast_block_spec(w, steps, row) for w in to_cast),
        out_shape=(jax.ShapeDtypeStruct((n, D_SSM), F32), jax.ShapeDtypeStruct((n, D_SGU), BF16))
        + tuple(_cast_out_shape(w) for w in to_cast),
        scratch_shapes=[pltpu.VMEM((ROWS_IN, D_SGU), F32)],
        compiler_params=pltpu.CompilerParams(
            dimension_semantics=("arbitrary",), vmem_limit_bytes=VMEM_LIMIT),
    )(x2, g, w_in, perm, ln_g, ln_b, sgu_w, sgu_bt, og, *to_cast)


def _s5_kernel(zp_ref, zn_ref, bd_ref, cd_ref, tab_ref, d_ref, c0_ref,
               out_ref, c0_out, bu_a, bu_b, sb_a, sb_b, carry_ref):
    for f in range(c0_out.shape[0]):
        gate_cols = slice(f * FFN_BLK, (f + 1) * FFN_BLK)
        up_cols = slice(D_FFN + f * FFN_BLK, D_FFN + (f + 1) * FFN_BLK)
        c0_out[f, :, 0:FFN_BLK] = c0_ref[:, gate_cols].astype(BF16)
        c0_out[f, :, FFN_BLK:] = c0_ref[:, up_cols].astype(BF16)

    @pl.when(pl.program_id(1) == 0)
    def _():
        carry_ref[...] = jnp.zeros_like(carry_ref)

    zp = zp_ref[...]
    zpb = zp.astype(BF16)
    znb = zn_ref[...].astype(BF16)
    bu = (bu_a, bu_b)
    sb = (sb_a, sb_b)
    seg = lax.broadcasted_iota(jnp.int32, (SUBLANES, SCAN_W), 0)
    zero = jnp.zeros((SUBLANES, SCAN_W), F32)
    blk = lambda j: slice(j * BLK_CH, (j + 1) * BLK_CH)

    @pl.when((pl.program_id(0) == 0) & (pl.program_id(1) == 0))
    def _():
        for p in range(N_PIECE):
            bu[0][p] = _dot(zpb[:, blk(0)], bd_ref[0, p])

    for j in range(N_BLK):
        out_ref[:, blk(j)] = d_ref[:, blk(j)] * zp[:, blk(j)]

    for j in range(N_BLK):
        cur, oth = j % 2, (j + 1) % 2

        def body(g, _, j=j, cur=cur, oth=oth):
            for part in range(2):
                p = 2 * g + part
                if j + 1 < N_BLK:
                    bu[oth][p] = _dot(zpb[:, blk(j + 1)], bd_ref[j + 1, p])
                else:
                    bu[oth][p] = _dot(znb, bd_ref[0, p])
                if j >= 1:
                    out_ref[:, blk(j - 1)] += _dot(sb[oth][p], cd_ref[j - 1, p])

            re_p, im_p = 2 * g, 2 * g + 1
            tre, tim = slice(0, SCAN_W), slice(SCAN_W, 2 * SCAN_W)
            ar = jnp.broadcast_to(tab_ref[j, g, 0:1, tre], (SUBLANES, SCAN_W))
            ai = jnp.broadcast_to(tab_ref[j, g, 0:1, tim], (SUBLANES, SCAN_W))

            def step(row0, sr, si):
                rows = pl.ds(row0, SUBLANES)
                return (ar * sr - ai * si + bu[cur][re_p, rows, :],
                        ar * si + ai * sr + bu[cur][im_p, rows, :])

            def start_states(xr, xi):
                for lvl, m in enumerate((1, 2, 4)):
                    mr = tab_ref[j, g, 1 + lvl:2 + lvl, tre]
                    mi = tab_ref[j, g, 1 + lvl:2 + lvl, tim]
                    rr = pltpu.roll(xr, m, 0)
                    ri = pltpu.roll(xi, m, 0)
                    keep = seg >= m
                    xr, xi = (xr + jnp.where(keep, mr * rr - mi * ri, 0.0),
                              xi + jnp.where(keep, mr * ri + mi * rr, 0.0))
                cr = carry_ref[j, g, :, tre]
                ci = carry_ref[j, g, :, tim]
                pr = tab_ref[j, g, 4:4 + SUBLANES, tre]
                pi = tab_ref[j, g, 4:4 + SUBLANES, tim]
                xr, xi = xr + pr * cr - pi * ci, xi + pr * ci + pi * cr
                carry_ref[j, g, :, tre] = jnp.broadcast_to(xr[SUBLANES - 1:SUBLANES], (SUBLANES, SCAN_W))
                carry_ref[j, g, :, tim] = jnp.broadcast_to(xi[SUBLANES - 1:SUBLANES], (SUBLANES, SCAN_W))
                return (jnp.where(seg >= 1, pltpu.roll(xr, 1, 0), cr),
                        jnp.where(seg >= 1, pltpu.roll(xi, 1, 0), ci))

            def emit_pair(base, m, sr, si):
                r1, i1 = step(base + 2 * m * SUBLANES, sr, si)
                sr, si = step(base + (2 * m + 1) * SUBLANES, r1, i1)
                rows = pl.ds(base + m * 2 * SUBLANES, 2 * SUBLANES)
                sb[cur][re_p, rows, :] = jnp.concatenate([r1, sr], axis=0).astype(BF16)
                sb[cur][im_p, rows, :] = jnp.concatenate([i1, si], axis=0).astype(BF16)
                return sr, si

            n_chunk = S5_ROWS // T_CHUNK
            state = None
            for c in range(n_chunk + 1):
                xr, xi = zero, zero
                for m in range(K_STEPS // 2):
                    if c < n_chunk:
                        xr, xi = step(c * T_CHUNK + 2 * m * SUBLANES, xr, xi)
                        xr, xi = step(c * T_CHUNK + (2 * m + 1) * SUBLANES, xr, xi)
                    if c >= 1:
                        state = emit_pair((c - 1) * T_CHUNK, m, *state)
                if c < n_chunk:
                    state = start_states(xr, xi)
            return 0

        lax.fori_loop(0, N_GRP, body, 0)

    last = N_BLK - 1
    for p in range(N_PIECE):
        out_ref[:, blk(last)] += _dot(sb[last % 2][p], cd_ref[last, p])


def _s5(zs, bsz, seq, bd, cd, tab, dvec, to_cast):
    steps = seq // S5_ROWS
    row = lambda b, c: (b * steps + c, 0)
    nxt = lambda b, c: (jnp.minimum(b * steps + c + 1, bsz * steps - 1), 0)
    n_fblk = D_FFN // FFN_BLK
    cast_rows = to_cast.shape[0] // (bsz * steps)
    return pl.pallas_call(
        _s5_kernel,
        name="s5",
        grid=(bsz, steps),
        in_specs=[
            pl.BlockSpec((S5_ROWS, D_SSM), row),
            pl.BlockSpec((S5_ROWS, BLK_CH), nxt),
            _resident(bd.shape), _resident(cd.shape), _resident(tab.shape),
            _resident(dvec.shape),
            _cast_block_spec(to_cast, bsz * steps, row),
        ],
        out_specs=(pl.BlockSpec((S5_ROWS, D_SSM), row),
                   pl.BlockSpec((n_fblk, cast_rows, 2 * FFN_BLK), lambda b, c: (0, b * steps + c, 0))),
        out_shape=(jax.ShapeDtypeStruct((bsz * seq, D_SSM), F32),
                   jax.ShapeDtypeStruct((n_fblk, to_cast.shape[0], 2 * FFN_BLK), BF16)),
        scratch_shapes=[
            pltpu.VMEM((N_PIECE, S5_ROWS, SCAN_W), F32),
            pltpu.VMEM((N_PIECE, S5_ROWS, SCAN_W), F32),
            pltpu.VMEM((N_PIECE, S5_ROWS, SCAN_W), BF16),
            pltpu.VMEM((N_PIECE, S5_ROWS, SCAN_W), BF16),
            pltpu.VMEM((N_BLK, N_GRP, SUBLANES, 2 * SCAN_W), F32),
        ],
        compiler_params=pltpu.CompilerParams(
            dimension_semantics=("arbitrary", "arbitrary"), vmem_limit_bytes=VMEM_LIMIT),
    )(zs, zs, bd, cd, tab, dvec, to_cast)


def _out_proj_kernel(x_ref, y_ref, yb_ref, w_ref, g_ref, gluw_ref, glub_ref, og_ref, permt_ref,
                     x1_ref, h_ref):
    half = D_MODEL // 2
    yb = yb_ref[...]
    y = jax.nn.gelu(y_ref[...])
    x1_lo = x_ref[:, 0:half] + _dot(yb, w_ref[D_SSM:, 0:half])
    gate = jax.nn.sigmoid(_dot(y.astype(BF16), gluw_ref[...]) + glub_ref[...])
    ya = _rms(y * gate, og_ref[...]).astype(BF16)
    x1_hi = x_ref[:, half:] + _dot(yb, w_ref[D_SSM:, half:])
    permt = permt_ref[...]
    ya = jnp.concatenate(
        [_dot(permt, ya[c * T_CHUNK:(c + 1) * T_CHUNK]).astype(BF16)
         for c in range(x_ref.shape[0] // T_CHUNK)], axis=0)
    x1 = jnp.concatenate([x1_lo, x1_hi], axis=1) + _dot(ya, w_ref[0:D_SSM, :])
    x1_ref[...] = x1
    h_ref[...] = _rms(x1, g_ref[...]).astype(BF16)


def _out_proj(x2, y, yb, w_out, g, glu_w, glu_b, og, permt):
    n = x2.shape[0]
    row = lambda i: (i, 0)
    return pl.pallas_call(
        _out_proj_kernel,
        name="out_proj",
        grid=(n // ROWS_IN,),
        in_specs=[
            pl.BlockSpec((ROWS_IN, D_MODEL), row),
            pl.BlockSpec((ROWS_IN, D_SSM), row),
            pl.BlockSpec((ROWS_IN, D_SGU), row),
            _resident(w_out.shape),
            _resident((1, D_MODEL)),
            _resident(glu_w.shape), _resident(glu_b.shape), _resident(og.shape),
            _resident(permt.shape),
        ],
        out_specs=(pl.BlockSpec((ROWS_IN, D_MODEL), row), pl.BlockSpec((ROWS_IN, D_MODEL), row)),
        out_shape=(jax.ShapeDtypeStruct((n, D_MODEL), F32), jax.ShapeDtypeStruct((n, D_MODEL), BF16)),
        compiler_params=pltpu.CompilerParams(
            dimension_semantics=("arbitrary",), vmem_limit_bytes=VMEM_LIMIT),
    )(x2, y, yb, w_out, g, glu_w, glu_b, og, permt)


def _ffn_kernel(h_ref, x1_ref, wgu_ref, wo_ref, out_ref):
    @pl.when(pl.program_id(1) == 0)
    def _():
        out_ref[...] = x1_ref[...]

    gu = _dot(h_ref[...], wgu_ref[...])
    act = jax.nn.silu(gu[:, 0:FFN_BLK]) * gu[:, FFN_BLK:]
    out_ref[...] += _dot(act.astype(BF16), wo_ref[...])


def _ffn(h, x1, w_in, w_out):
    n = h.shape[0]
    nf = D_FFN // FFN_BLK
    return pl.pallas_call(
        _ffn_kernel,
        name="ffn",
        grid=(n // ROWS_FFN, nf),
        in_specs=[
            pl.BlockSpec((ROWS_FFN, D_MODEL), lambda i, f: (i, 0)),
            pl.BlockSpec((ROWS_FFN, D_MODEL), lambda i, f: (i, 0)),
            pl.BlockSpec((None, D_MODEL, 2 * FFN_BLK), lambda i, f: (f, 0, 0)),
            pl.BlockSpec((FFN_BLK, D_MODEL), lambda i, f: (f, 0)),
        ],
        out_specs=pl.BlockSpec((ROWS_FFN, D_MODEL), lambda i, f: (i, 0)),
        out_shape=jax.ShapeDtypeStruct((n, D_MODEL), F32),
        compiler_params=pltpu.CompilerParams(
            dimension_semantics=("arbitrary", "arbitrary"), vmem_limit_bytes=VMEM_LIMIT_FFN),
    )(h, x1, w_in, w_out)


def _ple_kernel(x_ref, p_ref, wg_ref, bg_ref, wp_ref, g_ref, gf_ref, out_ref, *, final):
    x = x_ref[...]
    h = _rms(x, g_ref[...]).astype(BF16)
    pe = _dot(p_ref[...].astype(BF16), wp_ref[...])
    gate = jax.nn.sigmoid(_dot(h, wg_ref[...]) + bg_ref[...])
    x3 = x + gate * pe
    out_ref[...] = _rms(x3, gf_ref[...]) if final else x3


def _ple(x2, p2, w_gate, b_gate, w_proj, g, gf, final):
    n = x2.shape[0]
    row = lambda i: (i, 0)
    return pl.pallas_call(
        functools.partial(_ple_kernel, final=final),
        name="ple",
        grid=(n // ROWS_PLE,),
        in_specs=[
            pl.BlockSpec((ROWS_PLE, D_MODEL), row),
            pl.BlockSpec((ROWS_PLE, PLE_DIM), row),
            _resident(w_gate.shape),
            _resident((1, D_MODEL)),
            _resident(w_proj.shape),
            _resident((1, D_MODEL)),
            _resident((1, D_MODEL)),
        ],
        out_specs=pl.BlockSpec((ROWS_PLE, D_MODEL), row),
        out_shape=jax.ShapeDtypeStruct((n, D_MODEL), F32),
        compiler_params=pltpu.CompilerParams(
            dimension_semantics=("arbitrary",), vmem_limit_bytes=VMEM_LIMIT_FFN),
    )(x2, p2, w_gate, b_gate, w_proj, g, gf)


def _segment_permutation():
    r = np.arange(T_CHUNK)
    src = (r % SUBLANES) * K_STEPS + r // SUBLANES
    perm = np.zeros((T_CHUNK, T_CHUNK), np.float32)
    perm[r, src] = 1.0
    return perm


def _scan_exponents():
    n = [1.0]
    n += [float(K_STEPS * m) for m in (1, 2, 4)]
    n += [float(K_STEPS * (i + 1)) for i in range(SUBLANES)]
    return np.asarray(n, np.float32).reshape(-1, 1)


def kernel(x, p, norm_mix_g, w_in, ssm_lambda_re, ssm_lambda_im, ssm_log_step, ssm_b_re, ssm_b_im, ssm_c_re, ssm_c_im, ssm_d, ssm_glu_w, ssm_glu_b, sgu_ln_g, sgu_ln_b, sgu_w, sgu_b, out_norm_ssm_g, out_norm_sgu_g, w_out, norm_ffn_g, w_ffn_in, w_ffn_out, norm_ple_g, w_ple_gate, b_ple_gate, w_ple_proj, final_norm_g):
    bsz, seq, _ = x.shape
    depth = w_in.shape[0]
    perm_np = _segment_permutation()
    perm = jnp.asarray(perm_np, BF16)
    permt = jnp.asarray(perm_np.T, BF16)
    exponents = jnp.asarray(_scan_exponents())
    x2 = x.reshape(bsz * seq, D_MODEL)
    vec = lambda a: a.reshape(1, -1)
    for i in range(depth):
        tab, bd, cd = _s5_params(ssm_lambda_re[i], ssm_lambda_im[i], ssm_log_step[i], ssm_b_re[i],
                                 ssm_b_im[i], ssm_c_re[i], ssm_c_im[i], exponents)

        zs, yb, w_out_bf, w_gate_bf, glu_bf, w_ffn_out_bf = _in_proj(
            x2, vec(norm_mix_g[i]), w_in[i].astype(BF16), perm, vec(sgu_ln_g[i]), vec(sgu_ln_b[i]),
            sgu_w[i].astype(BF16), sgu_b[i].T, vec(out_norm_sgu_g[i]),
            to_cast=(w_out[i], w_ple_gate[i], ssm_glu_w[i], w_ffn_out[i]))
        y, w_ffn_in_bf = _s5(zs, bsz, seq, bd, cd, tab, vec(ssm_d[i]), to_cast=w_ffn_in[i])
        x1, h = _out_proj(x2, y, yb, w_out_bf, vec(norm_ffn_g[i]), glu_bf, vec(ssm_glu_b[i]),
                          vec(out_norm_ssm_g[i]), permt)
        x2 = _ffn(h, x1, w_ffn_in_bf, w_ffn_out_bf)
        x2 = _ple(x2, p[i].reshape(bsz * seq, PLE_DIM), w_gate_bf, vec(b_ple_gate[i]),
                  w_ple_proj[i].astype(BF16), vec(norm_ple_g[i]), vec(final_norm_g),
                  final=(i == depth - 1))
    return x2.reshape(bsz, seq, D_MODEL)
```

```python
import functools

import numpy as np
import jax
import jax.numpy as jnp
from jax import lax
from jax.experimental import pallas as pl
from jax.experimental.pallas import tpu as pltpu

D_MODEL = 2048
D_SSM = 1024
D_SGU = 1024
SSM_GROUP = 16
SSM_GROUPS = 64
SSM_STATE = 64
SGU_CHUNK = 128
SGU_HEADS = 8
SGU_HEAD_DIM = D_SGU // SGU_HEADS
D_FFN = 5632
PLE_DIM = 256
EPS = 1e-6
LAMBDA_RE_MAX = -1e-4

SUBLANES = 8
N_STATE = SSM_GROUPS * SSM_STATE
N_BLK = 4
BLK_CH = D_SSM // N_BLK
BLK_ST = N_STATE // N_BLK
T_CHUNK = 256
K_STEPS = T_CHUNK // SUBLANES
S5_ROWS = 512
SCAN_W = 512
N_GRP = BLK_ST // SCAN_W
N_PIECE = 2 * N_GRP

ROWS_IN = 512
ROWS_FFN = 1024
ROWS_PLE = 1024
FFN_BLK = 512
VMEM_LIMIT = 56 * 1024 * 1024
VMEM_LIMIT_FFN = 60 * 1024 * 1024

F32 = jnp.float32
BF16 = jnp.bfloat16


def _rms(x, g):
    r = lax.rsqrt(jnp.mean(x * x, axis=-1, keepdims=True) + EPS)
    return (x * r) * g


def _dot(a, b):
    return jnp.dot(a, b, preferred_element_type=F32)


def _resident(shape):
    nd = len(shape)
    return pl.BlockSpec(shape, lambda *_: (0,) * nd, pipeline_mode=pl.Buffered(1))


def _cast_block_spec(w, steps, index):
    return pl.BlockSpec((w.shape[0] // steps, w.shape[1]), index)


def _cast_out_shape(w):
    return jax.ShapeDtypeStruct(w.shape, BF16)


def _s5_params_kernel(sp_ref, n_ref, cre_ref, cim_ref, w_ref, tab_ref, bd_ref, cd_ref, w_out):
    w_out[...] = w_ref[...].astype(BF16)

    g = lax.rem(pl.program_id(0), N_GRP)
    h2 = 2 * SSM_GROUP
    lr = jnp.minimum(sp_ref[h2:h2 + 1, :], LAMBDA_RE_MAX)
    li = sp_ref[h2 + 1:h2 + 2, :]
    dt = jnp.exp(sp_ref[h2 + 2:h2 + 3, :])
    n = n_ref[...]
    mag = jnp.exp(n * (lr * dt))
    ang = n * (li * dt)
    tab_ref[0, 0] = jnp.concatenate([mag * jnp.cos(ang), mag * jnp.sin(ang)], axis=1)
    mag1 = jnp.exp(lr * dt)
    ang1 = li * dt
    nr = mag1 * jnp.cos(ang1) - 1.0
    ni = mag1 * jnp.sin(ang1)
    den = lr * lr + li * li
    q_re = (nr * lr + ni * li) / den
    q_im = (ni * lr - nr * li) / den
    bre = sp_ref[0:SSM_GROUP, :]
    bim = sp_ref[SSM_GROUP:h2, :]
    bbar = (q_re * bre - q_im * bim, q_re * bim + q_im * bre)

    iota = lambda shape, dim: lax.broadcasted_iota(jnp.int32, shape, dim)
    grp_of_ch = lambda i: lax.shift_right_logical(i, 4)
    grp_of_st = lambda i: lax.shift_right_logical(i, 6)
    grp_per_slice = SCAN_W // SSM_STATE
    first_grp = g * grp_per_slice
    same_b = grp_of_ch(iota((BLK_CH, SCAN_W), 0)) == grp_of_st(iota((BLK_CH, SCAN_W), 1)) + first_grp
    same_c = grp_of_st(iota((SCAN_W, BLK_CH), 0)) + first_grp == grp_of_ch(iota((SCAN_W, BLK_CH), 1))
    spread = (iota((SSM_GROUP, BLK_CH), 0)
              == lax.bitwise_and(iota((SSM_GROUP, BLK_CH), 1), SSM_GROUP - 1)).astype(BF16)
    for part in range(2):
        tiled = jnp.concatenate([bbar[part]] * SSM_GROUP, axis=0)
        bd_ref[0, part] = jnp.where(same_b, tiled, 0.0).astype(BF16)
        c_ref = (cre_ref, cim_ref)[part]
        wide = jnp.concatenate(
            [lax.dot_general(c_ref[q * SSM_GROUP:(q + 1) * SSM_GROUP, :].astype(BF16), spread,
                             (((0,), (0,)), ((), ())), preferred_element_type=F32)
             for q in range(grp_per_slice)], axis=0)
        wide = wide if part == 0 else -wide
        cd_ref[0, part] = jnp.where(same_c, wide, 0.0).astype(BF16)


def _s5_params(lam_re, lam_im, log_step, b_re, b_im, c_re, c_im, exponents, w_in):
    n_rows = exponents.shape[0]
    steps = N_BLK * N_GRP
    small = jnp.concatenate([
        b_re.transpose(2, 0, 1).reshape(SSM_GROUP, N_STATE),
        b_im.transpose(2, 0, 1).reshape(SSM_GROUP, N_STATE),
        lam_re.reshape(1, N_STATE),
        lam_im.reshape(1, N_STATE),
        jnp.repeat(log_step, SSM_STATE).reshape(1, N_STATE)], axis=0)
    cre = c_re.reshape(SSM_GROUPS * SSM_GROUP, SSM_STATE)
    cim = c_im.reshape(SSM_GROUPS * SSM_GROUP, SSM_STATE)
    lanes = lambda rows: pl.BlockSpec((rows, SCAN_W), lambda s: (0, s))
    c_rows = pl.BlockSpec((SCAN_W // SSM_STATE * SSM_GROUP, SSM_STATE), lambda s: (s, 0))
    slab = lambda shape: pl.BlockSpec((1,) + shape, lambda s: (s // N_GRP, s % N_GRP, 0, 0))
    w_block = pl.BlockSpec((w_in.shape[0] // steps, w_in.shape[1]), lambda s: (s, 0))
    return pl.pallas_call(
        _s5_params_kernel,
        name="s5_params",
        grid=(steps,),
        in_specs=[lanes(small.shape[0]), _resident(exponents.shape), c_rows, c_rows, w_block],
        out_specs=(slab((1, n_rows, 2 * SCAN_W)), slab((2, BLK_CH, SCAN_W)), slab((2, SCAN_W, BLK_CH)),
                   w_block),
        out_shape=(jax.ShapeDtypeStruct((N_BLK, N_GRP, n_rows, 2 * SCAN_W), F32),
                   jax.ShapeDtypeStruct((N_BLK, N_PIECE, BLK_CH, SCAN_W), BF16),
                   jax.ShapeDtypeStruct((N_BLK, N_PIECE, SCAN_W, BLK_CH), BF16),
                   jax.ShapeDtypeStruct(w_in.shape, BF16)),
        compiler_params=pltpu.CompilerParams(
            dimension_semantics=("arbitrary",), vmem_limit_bytes=VMEM_LIMIT),
    )(small, exponents, cre, cim, w_in)


def _in_proj_kernel(x_ref, g_ref, w_ref, perm_ref, lng_ref, lnb_ref, sw_ref, sbt_ref, og_ref,
                    c0_ref, c1_ref, c2_ref, c3_ref, c4_ref,
                    zs_ref, yb_ref, c0_out, c1_out, c2_out, c3_out, c4_out, s_scr):
    c0_out[...] = c0_ref[...].astype(BF16)
    c1_out[...] = c1_ref[...].astype(BF16)
    c2_out[...] = c2_ref[...].astype(BF16)
    c3_out[...] = c3_ref[...].astype(BF16)
    c4_out[...] = c4_ref[...].astype(BF16)
    rows = x_ref.shape[0]
    h = _rms(x_ref[...], g_ref[...]).astype(BF16)
    gv = jax.nn.gelu(_dot(h, w_ref[:, D_SSM + D_SGU:]))
    mu = jnp.mean(gv, axis=-1, keepdims=True)
    xc = gv - mu
    r = lax.rsqrt(jnp.mean(xc * xc, axis=-1, keepdims=True) + EPS)
    v = ((xc * r) * lng_ref[...] + lnb_ref[...]).astype(BF16)
    u = jax.nn.gelu(_dot(h, w_ref[:, D_SSM:D_SSM + D_SGU]))
    ti = lax.broadcasted_iota(jnp.int32, (SGU_CHUNK, SGU_CHUNK), 0)
    si = lax.broadcasted_iota(jnp.int32, (SGU_CHUNK, SGU_CHUNK), 1)
    causal = si <= ti
    sbt = sbt_ref[...]
    for hd in range(SGU_HEADS):
        wm = jnp.where(causal, sw_ref[hd], 0.0).astype(BF16)
        cols = slice(hd * SGU_HEAD_DIM, (hd + 1) * SGU_HEAD_DIM)
        bias = sbt[:, hd:hd + 1]
        for c in range(rows // SGU_CHUNK):
            rs = slice(c * SGU_CHUNK, (c + 1) * SGU_CHUNK)
            s_scr[rs, cols] = _dot(wm, v[rs, cols]) + bias
    yb = u * s_scr[...]
    yb_ref[...] = _rms(yb, og_ref[...]).astype(BF16)
    perm = perm_ref[...]
    hp = jnp.concatenate(
        [_dot(perm, h[c * T_CHUNK:(c + 1) * T_CHUNK]).astype(BF16) for c in range(rows // T_CHUNK)],
        axis=0)
    zs_ref[...] = _dot(hp, w_ref[:, 0:D_SSM])


def _in_proj(x2, g, w_in, perm, ln_g, ln_b, sgu_w, sgu_bt, og, to_cast):
    n = x2.shape[0]
    steps = n // ROWS_IN
    row = lambda i: (i, 0)
    return pl.pallas_call(
        _in_proj_kernel,
        name="in_proj",
        grid=(n // ROWS_IN,),
        in_specs=[
            pl.BlockSpec((ROWS_IN, D_MODEL), row),
            _resident((1, D_MODEL)),
            _resident(w_in.shape),
            _resident(perm.shape),
            _resident((1, D_SGU)),
            _resident((1, D_SGU)),
            _resident(sgu_w.shape),
            _resident(sgu_bt.shape),
            _resident((1, D_SGU)),
        ] + [_cast_block_spec(w, steps, row) for w in to_cast],
        out_specs=(pl.BlockSpec((ROWS_IN, D_SSM), row), pl.BlockSpec((ROWS_IN, D_SGU), row))
        + tuple(_cast_block_spec(w, steps, row) for w in to_cast),
        out_shape=(jax.ShapeDtypeStruct((n, D_SSM), F32), jax.ShapeDtypeStruct((n, D_SGU), BF16))
        + tuple(_cast_out_shape(w) for w in to_cast),
        scratch_shapes=[pltpu.VMEM((ROWS_IN, D_SGU), F32)],
        compiler_params=pltpu.CompilerParams(
            dimension_semantics=("arbitrary",), vmem_limit_bytes=VMEM_LIMIT),
    )(x2, g, w_in, perm, ln_g, ln_b, sgu_w, sgu_bt, og, *to_cast)


def _s5_kernel(zp_ref, zn_ref, bd_ref, cd_ref, tab_ref, d_ref, c0_ref,
               out_ref, c0_out, bu_a, bu_b, sb_a, sb_b, carry_ref):
    for f in range(c0_out.shape[0]):
        gate_cols = slice(f * FFN_BLK, (f + 1) * FFN_BLK)
        up_cols = slice(D_FFN + f * FFN_BLK, D_FFN + (f + 1) * FFN_BLK)
        c0_out[f, :, 0:FFN_BLK] = c0_ref[:, gate_cols].astype(BF16)
        c0_out[f, :, FFN_BLK:] = c0_ref[:, up_cols].astype(BF16)

    @pl.when(pl.program_id(1) == 0)
    def _():
        carry_ref[...] = jnp.zeros_like(carry_ref)

    zp = zp_ref[...]
    zpb = zp.astype(BF16)
    znb = zn_ref[...].astype(BF16)
    bu = (bu_a, bu_b)
    sb = (sb_a, sb_b)
    seg = lax.broadcasted_iota(jnp.int32, (SUBLANES, SCAN_W), 0)
    zero = jnp.zeros((SUBLANES, SCAN_W), F32)
    blk = lambda j: slice(j * BLK_CH, (j + 1) * BLK_CH)

    @pl.when((pl.program_id(0) == 0) & (pl.program_id(1) == 0))
    def _():
        for p in range(N_PIECE):
            bu[0][p] = _dot(zpb[:, blk(0)], bd_ref[0, p])

    for j in range(N_BLK):
        out_ref[:, blk(j)] = d_ref[:, blk(j)] * zp[:, blk(j)]

    for j in range(N_BLK):
        cur, oth = j % 2, (j + 1) % 2

        def body(g, _, j=j, cur=cur, oth=oth):
            for part in range(2):
                p = 2 * g + part
                if j + 1 < N_BLK:
                    bu[oth][p] = _dot(zpb[:, blk(j + 1)], bd_ref[j + 1, p])
                else:
                    bu[oth][p] = _dot(znb, bd_ref[0, p])
                if j >= 1:
                    out_ref[:, blk(j - 1)] += _dot(sb[oth][p], cd_ref[j - 1, p])

            re_p, im_p = 2 * g, 2 * g + 1
            tre, tim = slice(0, SCAN_W), slice(SCAN_W, 2 * SCAN_W)
            ar = jnp.broadcast_to(tab_ref[j, g, 0:1, tre], (SUBLANES, SCAN_W))
            ai = jnp.broadcast_to(tab_ref[j, g, 0:1, tim], (SUBLANES, SCAN_W))

            def step(row0, sr, si):
                rows = pl.ds(row0, SUBLANES)
                return (ar * sr - ai * si + bu[cur][re_p, rows, :],
                        ar * si + ai * sr + bu[cur][im_p, rows, :])

            def start_states(xr, xi):
                for lvl, m in enumerate((1, 2, 4)):
                    mr = tab_ref[j, g, 1 + lvl:2 + lvl, tre]
                    mi = tab_ref[j, g, 1 + lvl:2 + lvl, tim]
                    rr = pltpu.roll(xr, m, 0)
                    ri = pltpu.roll(xi, m, 0)
                    keep = seg >= m
                    xr, xi = (xr + jnp.where(keep, mr * rr - mi * ri, 0.0),
                              xi + jnp.where(keep, mr * ri + mi * rr, 0.0))
                cr = carry_ref[j, g, :, tre]
                ci = carry_ref[j, g, :, tim]
                pr = tab_ref[j, g, 4:4 + SUBLANES, tre]
                pi = tab_ref[j, g, 4:4 + SUBLANES, tim]
                xr, xi = xr + pr * cr - pi * ci, xi + pr * ci + pi * cr
                carry_ref[j, g, :, tre] = jnp.broadcast_to(xr[SUBLANES - 1:SUBLANES], (SUBLANES, SCAN_W))
                carry_ref[j, g, :, tim] = jnp.broadcast_to(xi[SUBLANES - 1:SUBLANES], (SUBLANES, SCAN_W))
                return (jnp.where(seg >= 1, pltpu.roll(xr, 1, 0), cr),
                        jnp.where(seg >= 1, pltpu.roll(xi, 1, 0), ci))

            def emit_pair(base, m, sr, si):
                r1, i1 = step(base + 2 * m * SUBLANES, sr, si)
                sr, si = step(base + (2 * m + 1) * SUBLANES, r1, i1)
                rows = pl.ds(base + m * 2 * SUBLANES, 2 * SUBLANES)
                sb[cur][re_p, rows, :] = jnp.concatenate([r1, sr], axis=0).astype(BF16)
                sb[cur][im_p, rows, :] = jnp.concatenate([i1, si], axis=0).astype(BF16)
                return sr, si

            n_chunk = S5_ROWS // T_CHUNK
            state = None
            for c in range(n_chunk + 1):
                xr, xi = zero, zero
                for m in range(K_STEPS // 2):
                    if c < n_chunk:
                        xr, xi = step(c * T_CHUNK + 2 * m * SUBLANES, xr, xi)
                        xr, xi = step(c * T_CHUNK + (2 * m + 1) * SUBLANES, xr, xi)
                    if c >= 1:
                        state = emit_pair((c - 1) * T_CHUNK, m, *state)
                if c < n_chunk:
                    state = start_states(xr, xi)
            return 0

        lax.fori_loop(0, N_GRP, body, 0)

    last = N_BLK - 1
    for p in range(N_PIECE):
        out_ref[:, blk(last)] += _dot(sb[last % 2][p], cd_ref[last, p])


def _s5(zs, bsz, seq, bd, cd, tab, dvec, to_cast):
    steps = seq // S5_ROWS
    row = lambda b, c: (b * steps + c, 0)
    nxt = lambda b, c: (jnp.minimum(b * steps + c + 1, bsz * steps - 1), 0)
    n_fblk = D_FFN // FFN_BLK
    cast_rows = to_cast.shape[0] // (bsz * steps)
    return pl.pallas_call(
        _s5_kernel,
        name="s5",
        grid=(bsz, steps),
        in_specs=[
            pl.BlockSpec((S5_ROWS, D_SSM), row),
            pl.BlockSpec((S5_ROWS, BLK_CH), nxt),
            _resident(bd.shape), _resident(cd.shape), _resident(tab.shape),
            _resident(dvec.shape),
            _cast_block_spec(to_cast, bsz * steps, row),
        ],
        out_specs=(pl.BlockSpec((S5_ROWS, D_SSM), row),
                   pl.BlockSpec((n_fblk, cast_rows, 2 * FFN_BLK), lambda b, c: (0, b * steps + c, 0))),
        out_shape=(jax.ShapeDtypeStruct((bsz * seq, D_SSM), F32),
                   jax.ShapeDtypeStruct((n_fblk, to_cast.shape[0], 2 * FFN_BLK), BF16)),
        scratch_shapes=[
            pltpu.VMEM((N_PIECE, S5_ROWS, SCAN_W), F32),
            pltpu.VMEM((N_PIECE, S5_ROWS, SCAN_W), F32),
            pltpu.VMEM((N_PIECE, S5_ROWS, SCAN_W), BF16),
            pltpu.VMEM((N_PIECE, S5_ROWS, SCAN_W), BF16),
            pltpu.VMEM((N_BLK, N_GRP, SUBLANES, 2 * SCAN_W), F32),
        ],
        compiler_params=pltpu.CompilerParams(
            dimension_semantics=("arbitrary", "arbitrary"), vmem_limit_bytes=VMEM_LIMIT),
    )(zs, zs, bd, cd, tab, dvec, to_cast)


def _out_proj_kernel(x_ref, y_ref, yb_ref, w_ref, g_ref, gluw_ref, glub_ref, og_ref, permt_ref,
                     x1_ref, h_ref):
    half = D_MODEL // 2
    yb = yb_ref[...]
    y = jax.nn.gelu(y_ref[...])
    x1_lo = x_ref[:, 0:half] + _dot(yb, w_ref[D_SSM:, 0:half])
    gate = jax.nn.sigmoid(_dot(y.astype(BF16), gluw_ref[...]) + glub_ref[...])
    ya = _rms(y * gate, og_ref[...]).astype(BF16)
    x1_hi = x_ref[:, half:] + _dot(yb, w_ref[D_SSM:, half:])
    permt = permt_ref[...]
    ya = jnp.concatenate(
        [_dot(permt, ya[c * T_CHUNK:(c + 1) * T_CHUNK]).astype(BF16)
         for c in range(x_ref.shape[0] // T_CHUNK)], axis=0)
    x1 = jnp.concatenate([x1_lo, x1_hi], axis=1) + _dot(ya, w_ref[0:D_SSM, :])
    x1_ref[...] = x1
    h_ref[...] = _rms(x1, g_ref[...]).astype(BF16)


def _out_proj(x2, y, yb, w_out, g, glu_w, glu_b, og, permt):
    n = x2.shape[0]
    row = lambda i: (i, 0)
    return pl.pallas_call(
        _out_proj_kernel,
        name="out_proj",
        grid=(n // ROWS_IN,),
        in_specs=[
            pl.BlockSpec((ROWS_IN, D_MODEL), row),
            pl.BlockSpec((ROWS_IN, D_SSM), row),
            pl.BlockSpec((ROWS_IN, D_SGU), row),
            _resident(w_out.shape),
            _resident((1, D_MODEL)),
            _resident(glu_w.shape), _resident(glu_b.shape), _resident(og.shape),
            _resident(permt.shape),
        ],
        out_specs=(pl.BlockSpec((ROWS_IN, D_MODEL), row), pl.BlockSpec((ROWS_IN, D_MODEL), row)),
        out_shape=(jax.ShapeDtypeStruct((n, D_MODEL), F32), jax.ShapeDtypeStruct((n, D_MODEL), BF16)),
        compiler_params=pltpu.CompilerParams(
            dimension_semantics=("arbitrary",), vmem_limit_bytes=VMEM_LIMIT),
    )(x2, y, yb, w_out, g, glu_w, glu_b, og, permt)


def _ffn_kernel(h_ref, x1_ref, wgu_ref, wo_ref, out_ref):
    @pl.when(pl.program_id(1) == 0)
    def _():
        out_ref[...] = x1_ref[...]

    gu = _dot(h_ref[...], wgu_ref[...])
    act = jax.nn.silu(gu[:, 0:FFN_BLK]) * gu[:, FFN_BLK:]
    out_ref[...] += _dot(act.astype(BF16), wo_ref[...])


def _ffn(h, x1, w_in, w_out):
    n = h.shape[0]
    nf = D_FFN // FFN_BLK
    return pl.pallas_call(
        _ffn_kernel,
        name="ffn",
        grid=(n // ROWS_FFN, nf),
        in_specs=[
            pl.BlockSpec((ROWS_FFN, D_MODEL), lambda i, f: (i, 0)),
            pl.BlockSpec((ROWS_FFN, D_MODEL), lambda i, f: (i, 0)),
            pl.BlockSpec((None, D_MODEL, 2 * FFN_BLK), lambda i, f: (f, 0, 0)),
            pl.BlockSpec((FFN_BLK, D_MODEL), lambda i, f: (f, 0)),
        ],
        out_specs=pl.BlockSpec((ROWS_FFN, D_MODEL), lambda i, f: (i, 0)),
        out_shape=jax.ShapeDtypeStruct((n, D_MODEL), F32),
        compiler_params=pltpu.CompilerParams(
            dimension_semantics=("arbitrary", "arbitrary"), vmem_limit_bytes=VMEM_LIMIT_FFN),
    )(h, x1, w_in, w_out)


def _ple_kernel(x_ref, p_ref, wg_ref, bg_ref, wp_ref, g_ref, gf_ref, out_ref, *, final):
    x = x_ref[...]
    h = _rms(x, g_ref[...]).astype(BF16)
    pe = _dot(p_ref[...].astype(BF16), wp_ref[...])
    gate = jax.nn.sigmoid(_dot(h, wg_ref[...]) + bg_ref[...])
    x3 = x + gate * pe
    out_ref[...] = _rms(x3, gf_ref[...]) if final else x3


def _ple(x2, p2, w_gate, b_gate, w_proj, g, gf, final):
    n = x2.shape[0]
    row = lambda i: (i, 0)
    return pl.pallas_call(
        functools.partial(_ple_kernel, final=final),
        name="ple",
        grid=(n // ROWS_PLE,),
        in_specs=[
            pl.BlockSpec((ROWS_PLE, D_MODEL), row),
            pl.BlockSpec((ROWS_PLE, PLE_DIM), row),
            _resident(w_gate.shape),
            _resident((1, D_MODEL)),
            _resident(w_proj.shape),
            _resident((1, D_MODEL)),
            _resident((1, D_MODEL)),
        ],
        out_specs=pl.BlockSpec((ROWS_PLE, D_MODEL), row),
        out_shape=jax.ShapeDtypeStruct((n, D_MODEL), F32),
        compiler_params=pltpu.CompilerParams(
            dimension_semantics=("arbitrary",), vmem_limit_bytes=VMEM_LIMIT_FFN),
    )(x2, p2, w_gate, b_gate, w_proj, g, gf)


def _segment_permutation():
    r = np.arange(T_CHUNK)
    src = (r % SUBLANES) * K_STEPS + r // SUBLANES
    perm = np.zeros((T_CHUNK, T_CHUNK), np.float32)
    perm[r, src] = 1.0
    return perm


def _scan_exponents():
    n = [1.0]
    n += [float(K_STEPS * m) for m in (1, 2, 4)]
    n += [float(K_STEPS * (i + 1)) for i in range(SUBLANES)]
    return np.asarray(n, np.float32).reshape(-1, 1)


def kernel(x, p, norm_mix_g, w_in, ssm_lambda_re, ssm_lambda_im, ssm_log_step, ssm_b_re, ssm_b_im, ssm_c_re, ssm_c_im, ssm_d, ssm_glu_w, ssm_glu_b, sgu_ln_g, sgu_ln_b, sgu_w, sgu_b, out_norm_ssm_g, out_norm_sgu_g, w_out, norm_ffn_g, w_ffn_in, w_ffn_out, norm_ple_g, w_ple_gate, b_ple_gate, w_ple_proj, final_norm_g):
    bsz, seq, d_model = x.shape
    depth = w_in.shape[0]
    assert d_model == D_MODEL and x.dtype == F32
    assert seq % S5_ROWS == 0 and (bsz * seq) % ROWS_FFN == 0 and (bsz * seq) % ROWS_PLE == 0
    perm_np = _segment_permutation()
    perm = jnp.asarray(perm_np, BF16)
    permt = jnp.asarray(perm_np.T, BF16)
    exponents = jnp.asarray(_scan_exponents())
    x2 = x.reshape(bsz * seq, D_MODEL)
    vec = lambda a: a.reshape(1, -1)
    for i in range(depth):
        tab, bd, cd, w_in_bf = _s5_params(
            ssm_lambda_re[i], ssm_lambda_im[i], ssm_log_step[i], ssm_b_re[i], ssm_b_im[i],
            ssm_c_re[i], ssm_c_im[i], exponents, w_in[i])

        zs, yb, w_out_bf, w_gate_bf, glu_bf, w_ffn_out_bf, w_proj_bf = _in_proj(
            x2, vec(norm_mix_g[i]), w_in_bf, perm, vec(sgu_ln_g[i]), vec(sgu_ln_b[i]),
            sgu_w[i], sgu_b[i].T, vec(out_norm_sgu_g[i]),
            to_cast=(w_out[i], w_ple_gate[i], ssm_glu_w[i], w_ffn_out[i], w_ple_proj[i]))
        y, w_ffn_in_bf = _s5(zs, bsz, seq, bd, cd, tab, vec(ssm_d[i]), to_cast=w_ffn_in[i])
        x1, h = _out_proj(x2, y, yb, w_out_bf, vec(norm_ffn_g[i]), glu_bf, vec(ssm_glu_b[i]),
                          vec(out_norm_ssm_g[i]), permt)
        x2 = _ffn(h, x1, w_ffn_in_bf, w_ffn_out_bf)
        x2 = _ple(x2, p[i].reshape(bsz * seq, PLE_DIM), w_gate_bf, vec(b_ple_gate[i]),
                  w_proj_bf, vec(norm_ple_g[i]), vec(final_norm_g),
                  final=(i == depth - 1))
    return x2.reshape(bsz, seq, D_MODEL)
```

```python
import functools

import numpy as np
import jax
import jax.numpy as jnp
from jax import lax
from jax.experimental import pallas as pl
from jax.experimental.pallas import tpu as pltpu

D_MODEL = 2048
D_SSM = 1024
D_SGU = 1024
SSM_GROUP = 16
SSM_GROUPS = 64
SSM_STATE = 64
SGU_CHUNK = 128
SGU_HEADS = 8
SGU_HEAD_DIM = D_SGU // SGU_HEADS
D_FFN = 5632
PLE_DIM = 256
EPS = 1e-6
LAMBDA_RE_MAX = -1e-4

SUBLANES = 8
N_STATE = SSM_GROUPS * SSM_STATE
N_BLK = 4
BLK_CH = D_SSM // N_BLK
BLK_ST = N_STATE // N_BLK
T_CHUNK = 256
K_STEPS = T_CHUNK // SUBLANES
S5_ROWS = 512
SCAN_W = 512
N_GRP = BLK_ST // SCAN_W
N_PIECE = 2 * N_GRP

ROWS_IN = 512
ROWS_FFN = 1024
ROWS_PLE = 1024
FFN_BLK = 512
VMEM_LIMIT = 56 * 1024 * 1024
VMEM_LIMIT_FFN = 60 * 1024 * 1024

F32 = jnp.float32
BF16 = jnp.bfloat16


def _rms(x, g):
    r = lax.rsqrt(jnp.mean(x * x, axis=-1, keepdims=True) + EPS)
    return (x * r) * g


def _dot(a, b):
    return jnp.dot(a, b, preferred_element_type=F32)


def _resident(shape):
    nd = len(shape)
    return pl.BlockSpec(shape, lambda *_: (0,) * nd, pipeline_mode=pl.Buffered(1))


def _cast_block_spec(w, steps, index):
    return pl.BlockSpec((w.shape[0] // steps, w.shape[1]), index)


def _cast_out_shape(w):
    return jax.ShapeDtypeStruct(w.shape, BF16)


def _s5_params_kernel(sp_ref, n_ref, cre_ref, cim_ref, w_ref, tab_ref, bd_ref, cd_ref, w_out):
    w_out[...] = w_ref[...].astype(BF16)

    g = lax.rem(pl.program_id(0), N_GRP)
    h2 = 2 * SSM_GROUP
    lr = jnp.minimum(sp_ref[h2:h2 + 1, :], LAMBDA_RE_MAX)
    li = sp_ref[h2 + 1:h2 + 2, :]
    dt = jnp.exp(sp_ref[h2 + 2:h2 + 3, :])
    n = n_ref[...]
    mag = jnp.exp(n * (lr * dt))
    ang = n * (li * dt)
    tab_ref[0, 0] = jnp.concatenate([mag * jnp.cos(ang), mag * jnp.sin(ang)], axis=1)
    mag1 = jnp.exp(lr * dt)
    ang1 = li * dt
    nr = mag1 * jnp.cos(ang1) - 1.0
    ni = mag1 * jnp.sin(ang1)
    den = lr * lr + li * li
    q_re = (nr * lr + ni * li) / den
    q_im = (ni * lr - nr * li) / den
    bre = sp_ref[0:SSM_GROUP, :]
    bim = sp_ref[SSM_GROUP:h2, :]
    bbar = (q_re * bre - q_im * bim, q_re * bim + q_im * bre)

    iota = lambda shape, dim: lax.broadcasted_iota(jnp.int32, shape, dim)
    grp_of_ch = lambda i: lax.shift_right_logical(i, 4)
    grp_of_st = lambda i: lax.shift_right_logical(i, 6)
    grp_per_slice = SCAN_W // SSM_STATE
    first_grp = g * grp_per_slice
    same_b = grp_of_ch(iota((BLK_CH, SCAN_W), 0)) == grp_of_st(iota((BLK_CH, SCAN_W), 1)) + first_grp
    same_c = grp_of_st(iota((SCAN_W, BLK_CH), 0)) + first_grp == grp_of_ch(iota((SCAN_W, BLK_CH), 1))
    spread = (iota((SSM_GROUP, BLK_CH), 0)
              == lax.bitwise_and(iota((SSM_GROUP, BLK_CH), 1), SSM_GROUP - 1)).astype(BF16)
    for part in range(2):
        tiled = jnp.concatenate([bbar[part]] * SSM_GROUP, axis=0)
        bd_ref[0, part] = jnp.where(same_b, tiled, 0.0).astype(BF16)
        c_ref = (cre_ref, cim_ref)[part]
        wide = jnp.concatenate(
            [lax.dot_general(c_ref[q * SSM_GROUP:(q + 1) * SSM_GROUP, :].astype(BF16), spread,
                             (((0,), (0,)), ((), ())), preferred_element_type=F32)
             for q in range(grp_per_slice)], axis=0)
        wide = wide if part == 0 else -wide
        cd_ref[0, part] = jnp.where(same_c, wide, 0.0).astype(BF16)


def _s5_params(lam_re, lam_im, log_step, b_re, b_im, c_re, c_im, exponents, w_in):
    n_rows = exponents.shape[0]
    steps = N_BLK * N_GRP
    small = jnp.concatenate([
        b_re.transpose(2, 0, 1).reshape(SSM_GROUP, N_STATE),
        b_im.transpose(2, 0, 1).reshape(SSM_GROUP, N_STATE),
        lam_re.reshape(1, N_STATE),
        lam_im.reshape(1, N_STATE),
        jnp.repeat(log_step, SSM_STATE).reshape(1, N_STATE)], axis=0)
    cre = c_re.reshape(SSM_GROUPS * SSM_GROUP, SSM_STATE)
    cim = c_im.reshape(SSM_GROUPS * SSM_GROUP, SSM_STATE)
    lanes = lambda rows: pl.BlockSpec((rows, SCAN_W), lambda s: (0, s))
    c_rows = pl.BlockSpec((SCAN_W // SSM_STATE * SSM_GROUP, SSM_STATE), lambda s: (s, 0))
    slab = lambda shape: pl.BlockSpec((1,) + shape, lambda s: (s // N_GRP, s % N_GRP, 0, 0))
    w_block = pl.BlockSpec((w_in.shape[0] // steps, w_in.shape[1]), lambda s: (s, 0))
    return pl.pallas_call(
        _s5_params_kernel,
        name="s5_params",
        grid=(steps,),
        in_specs=[lanes(small.shape[0]), _resident(exponents.shape), c_rows, c_rows, w_block],
        out_specs=(slab((1, n_rows, 2 * SCAN_W)), slab((2, BLK_CH, SCAN_W)), slab((2, SCAN_W, BLK_CH)),
                   w_block),
        out_shape=(jax.ShapeDtypeStruct((N_BLK, N_GRP, n_rows, 2 * SCAN_W), F32),
                   jax.ShapeDtypeStruct((N_BLK, N_PIECE, BLK_CH, SCAN_W), BF16),
                   jax.ShapeDtypeStruct((N_BLK, N_PIECE, SCAN_W, BLK_CH), BF16),
                   jax.ShapeDtypeStruct(w_in.shape, BF16)),
        compiler_params=pltpu.CompilerParams(
            dimension_semantics=("arbitrary",), vmem_limit_bytes=VMEM_LIMIT),
    )(small, exponents, cre, cim, w_in)


def _in_proj_kernel(x_ref, g_ref, w_ref, perm_ref, lng_ref, lnb_ref, sw_ref, sbt_ref, og_ref,
                    c0_ref, c1_ref, c2_ref, c3_ref, c4_ref,
                    zs_ref, yb_ref, c0_out, c1_out, c2_out, c3_out, c4_out, s_scr):
    c0_out[...] = c0_ref[...].astype(BF16)
    c1_out[...] = c1_ref[...].astype(BF16)
    c2_out[...] = c2_ref[...].astype(BF16)
    c3_out[...] = c3_ref[...].astype(BF16)
    c4_out[...] = c4_ref[...].astype(BF16)
    rows = x_ref.shape[0]
    h = _rms(x_ref[...], g_ref[...]).astype(BF16)
    gv = jax.nn.gelu(_dot(h, w_ref[:, D_SSM + D_SGU:]))
    mu = jnp.mean(gv, axis=-1, keepdims=True)
    xc = gv - mu
    r = lax.rsqrt(jnp.mean(xc * xc, axis=-1, keepdims=True) + EPS)
    v = ((xc * r) * lng_ref[...] + lnb_ref[...]).astype(BF16)
    u = jax.nn.gelu(_dot(h, w_ref[:, D_SSM:D_SSM + D_SGU]))
    ti = lax.broadcasted_iota(jnp.int32, (SGU_CHUNK, SGU_CHUNK), 0)
    si = lax.broadcasted_iota(jnp.int32, (SGU_CHUNK, SGU_CHUNK), 1)
    causal = si <= ti
    sbt = sbt_ref[...]
    for hd in range(SGU_HEADS):
        wm = jnp.where(causal, sw_ref[hd], 0.0).astype(BF16)
        cols = slice(hd * SGU_HEAD_DIM, (hd + 1) * SGU_HEAD_DIM)
        bias = sbt[:, hd:hd + 1]
        for c in range(rows // SGU_CHUNK):
            rs = slice(c * SGU_CHUNK, (c + 1) * SGU_CHUNK)
            s_scr[rs, cols] = _dot(wm, v[rs, cols]) + bias
    yb = u * s_scr[...]
    yb_ref[...] = _rms(yb, og_ref[...]).astype(BF16)
    perm = perm_ref[...]
    hp = jnp.concatenate(
        [_dot(perm, h[c * T_CHUNK:(c + 1) * T_CHUNK]).astype(BF16) for c in range(rows // T_CHUNK)],
        axis=0)
    zs_ref[...] = _dot(hp, w_ref[:, 0:D_SSM])


def _in_proj(x2, g, w_in, perm, ln_g, ln_b, sgu_w, sgu_bt, og, to_cast):
    n = x2.shape[0]
    steps = n // ROWS_IN
    row = lambda i: (i, 0)
    return pl.pallas_call(
        _in_proj_kernel,
        name="in_proj",
        grid=(n // ROWS_IN,),
        in_specs=[
            pl.BlockSpec((ROWS_IN, D_MODEL), row),
            _resident((1, D_MODEL)),
            _resident(w_in.shape),
            _resident(perm.shape),
            _resident((1, D_SGU)),
            _resident((1, D_SGU)),
            _resident(sgu_w.shape),
            _resident(sgu_bt.shape),
            _resident((1, D_SGU)),
        ] + [_cast_block_spec(w, steps, row) for w in to_cast],
        out_specs=(pl.BlockSpec((ROWS_IN, D_SSM), row), pl.BlockSpec((ROWS_IN, D_SGU), row))
        + tuple(_cast_block_spec(w, steps, row) for w in to_cast),
        out_shape=(jax.ShapeDtypeStruct((n, D_SSM), F32), jax.ShapeDtypeStruct((n, D_SGU), BF16))
        + tuple(_cast_out_shape(w) for w in to_cast),
        scratch_shapes=[pltpu.VMEM((ROWS_IN, D_SGU), F32)],
        compiler_params=pltpu.CompilerParams(
            dimension_semantics=("arbitrary",), vmem_limit_bytes=VMEM_LIMIT),
    )(x2, g, w_in, perm, ln_g, ln_b, sgu_w, sgu_bt, og, *to_cast)


def _s5_kernel(zp_ref, zn_ref, bd_ref, cd_ref, tab_ref, d_ref, c0_ref,
               out_ref, c0_out, bu_a, bu_b, sb_a, sb_b, carry_ref):
    for f in range(c0_out.shape[0]):
        gate_cols = slice(f * FFN_BLK, (f + 1) * FFN_BLK)
        up_cols = slice(D_FFN + f * FFN_BLK, D_FFN + (f + 1) * FFN_BLK)
        c0_out[f, :, 0:FFN_BLK] = c0_ref[:, gate_cols].astype(BF16)
        c0_out[f, :, FFN_BLK:] = c0_ref[:, up_cols].astype(BF16)

    @pl.when(pl.program_id(1) == 0)
    def _():
        carry_ref[...] = jnp.zeros_like(carry_ref)

    zp = zp_ref[...]
    zpb = zp.astype(BF16)
    znb = zn_ref[...].astype(BF16)
    bu = (bu_a, bu_b)
    sb = (sb_a, sb_b)
    seg = lax.broadcasted_iota(jnp.int32, (SUBLANES, SCAN_W), 0)
    zero = jnp.zeros((SUBLANES, SCAN_W), F32)
    blk = lambda j: slice(j * BLK_CH, (j + 1) * BLK_CH)

    @pl.when((pl.program_id(0) == 0) & (pl.program_id(1) == 0))
    def _():
        for p in range(N_PIECE):
            bu[0][p] = _dot(zpb[:, blk(0)], bd_ref[0, p])

    for j in range(N_BLK):
        out_ref[:, blk(j)] = d_ref[:, blk(j)] * zp[:, blk(j)]

    for j in range(N_BLK):
        cur, oth = j % 2, (j + 1) % 2

        def body(g, _, j=j, cur=cur, oth=oth):
            for part in range(2):
                p = 2 * g + part
                if j + 1 < N_BLK:
                    bu[oth][p] = _dot(zpb[:, blk(j + 1)], bd_ref[j + 1, p])
                else:
                    bu[oth][p] = _dot(znb, bd_ref[0, p])
                if j >= 1:
                    out_ref[:, blk(j - 1)] += _dot(sb[oth][p], cd_ref[j - 1, p])

            re_p, im_p = 2 * g, 2 * g + 1
            tre, tim = slice(0, SCAN_W), slice(SCAN_W, 2 * SCAN_W)
            ar = jnp.broadcast_to(tab_ref[j, g, 0:1, tre], (SUBLANES, SCAN_W))
            ai = jnp.broadcast_to(tab_ref[j, g, 0:1, tim], (SUBLANES, SCAN_W))

            def step(row0, sr, si):
                rows = pl.ds(row0, SUBLANES)
                return (ar * sr - ai * si + bu[cur][re_p, rows, :],
                        ar * si + ai * sr + bu[cur][im_p, rows, :])

            def start_states(xr, xi):
                for lvl, m in enumerate((1, 2, 4)):
                    mr = tab_ref[j, g, 1 + lvl:2 + lvl, tre]
                    mi = tab_ref[j, g, 1 + lvl:2 + lvl, tim]
                    rr = pltpu.roll(xr, m, 0)
                    ri = pltpu.roll(xi, m, 0)
                    keep = seg >= m
                    xr, xi = (xr + jnp.where(keep, mr * rr - mi * ri, 0.0),
                              xi + jnp.where(keep, mr * ri + mi * rr, 0.0))
                cr = carry_ref[j, g, :, tre]
                ci = carry_ref[j, g, :, tim]
                pr = tab_ref[j, g, 4:4 + SUBLANES, tre]
                pi = tab_ref[j, g, 4:4 + SUBLANES, tim]
                xr, xi = xr + pr * cr - pi * ci, xi + pr * ci + pi * cr
                carry_ref[j, g, :, tre] = jnp.broadcast_to(xr[SUBLANES - 1:SUBLANES], (SUBLANES, SCAN_W))
                carry_ref[j, g, :, tim] = jnp.broadcast_to(xi[SUBLANES - 1:SUBLANES], (SUBLANES, SCAN_W))
                return (jnp.where(seg >= 1, pltpu.roll(xr, 1, 0), cr),
                        jnp.where(seg >= 1, pltpu.roll(xi, 1, 0), ci))

            def emit_pair(base, m, sr, si):
                r1, i1 = step(base + 2 * m * SUBLANES, sr, si)
                sr, si = step(base + (2 * m + 1) * SUBLANES, r1, i1)
                rows = pl.ds(base + m * 2 * SUBLANES, 2 * SUBLANES)
                sb[cur][re_p, rows, :] = jnp.concatenate([r1, sr], axis=0).astype(BF16)
                sb[cur][im_p, rows, :] = jnp.concatenate([i1, si], axis=0).astype(BF16)
                return sr, si

            n_chunk = S5_ROWS // T_CHUNK
            state = None
            for c in range(n_chunk + 1):
                xr, xi = zero, zero
                for m in range(K_STEPS // 2):
                    if c < n_chunk:
                        xr, xi = step(c * T_CHUNK + 2 * m * SUBLANES, xr, xi)
                        xr, xi = step(c * T_CHUNK + (2 * m + 1) * SUBLANES, xr, xi)
                    if c >= 1:
                        state = emit_pair((c - 1) * T_CHUNK, m, *state)
                if c < n_chunk:
                    state = start_states(xr, xi)
            return 0

        lax.fori_loop(0, N_GRP, body, 0)

    last = N_BLK - 1
    for p in range(N_PIECE):
        out_ref[:, blk(last)] += _dot(sb[last % 2][p], cd_ref[last, p])


def _s5(zs, bsz, seq, bd, cd, tab, dvec, to_cast):
    steps = seq // S5_ROWS
    row = lambda b, c: (b * steps + c, 0)
    nxt = lambda b, c: (jnp.minimum(b * steps + c + 1, bsz * steps - 1), 0)
    n_fblk = D_FFN // FFN_BLK
    cast_rows = to_cast.shape[0] // (bsz * steps)
    return pl.pallas_call(
        _s5_kernel,
        name="s5",
        grid=(bsz, steps),
        in_specs=[
            pl.BlockSpec((S5_ROWS, D_SSM), row),
            pl.BlockSpec((S5_ROWS, BLK_CH), nxt),
            _resident(bd.shape), _resident(cd.shape), _resident(tab.shape),
            _resident(dvec.shape),
            _cast_block_spec(to_cast, bsz * steps, row),
        ],
        out_specs=(pl.BlockSpec((S5_ROWS, D_SSM), row),
                   pl.BlockSpec((n_fblk, cast_rows, 2 * FFN_BLK), lambda b, c: (0, b * steps + c, 0))),
        out_shape=(jax.ShapeDtypeStruct((bsz * seq, D_SSM), F32),
                   jax.ShapeDtypeStruct((n_fblk, to_cast.shape[0], 2 * FFN_BLK), BF16)),
        scratch_shapes=[
            pltpu.VMEM((N_PIECE, S5_ROWS, SCAN_W), F32),
            pltpu.VMEM((N_PIECE, S5_ROWS, SCAN_W), F32),
            pltpu.VMEM((N_PIECE, S5_ROWS, SCAN_W), BF16),
            pltpu.VMEM((N_PIECE, S5_ROWS, SCAN_W), BF16),
            pltpu.VMEM((N_BLK, N_GRP, SUBLANES, 2 * SCAN_W), F32),
        ],
        compiler_params=pltpu.CompilerParams(
            dimension_semantics=("arbitrary", "arbitrary"), vmem_limit_bytes=VMEM_LIMIT),
    )(zs, zs, bd, cd, tab, dvec, to_cast)


def _out_proj_kernel(x_ref, y_ref, yb_ref, w_ref, g_ref, gluw_ref, glub_ref, og_ref, permt_ref,
                     x1_ref, h_ref):
    half = D_MODEL // 2
    yb = yb_ref[...]
    y = jax.nn.gelu(y_ref[...])
    x1_lo = x_ref[:, 0:half] + _dot(yb, w_ref[D_SSM:, 0:half])
    gate = jax.nn.sigmoid(_dot(y.astype(BF16), gluw_ref[...]) + glub_ref[...])
    ya = _rms(y * gate, og_ref[...]).astype(BF16)
    x1_hi = x_ref[:, half:] + _dot(yb, w_ref[D_SSM:, half:])
    permt = permt_ref[...]
    ya = jnp.concatenate(
        [_dot(permt, ya[c * T_CHUNK:(c + 1) * T_CHUNK]).astype(BF16)
         for c in range(x_ref.shape[0] // T_CHUNK)], axis=0)
    x1 = jnp.concatenate([x1_lo, x1_hi], axis=1) + _dot(ya, w_ref[0:D_SSM, :])
    x1_ref[...] = x1
    h_ref[...] = _rms(x1, g_ref[...]).astype(BF16)


def _out_proj(x2, y, yb, w_out, g, glu_w, glu_b, og, permt):
    n = x2.shape[0]
    row = lambda i: (i, 0)
    return pl.pallas_call(
        _out_proj_kernel,
        name="out_proj",
        grid=(n // ROWS_IN,),
        in_specs=[
            pl.BlockSpec((ROWS_IN, D_MODEL), row),
            pl.BlockSpec((ROWS_IN, D_SSM), row),
            pl.BlockSpec((ROWS_IN, D_SGU), row),
            _resident(w_out.shape),
            _resident((1, D_MODEL)),
            _resident(glu_w.shape), _resident(glu_b.shape), _resident(og.shape),
            _resident(permt.shape),
        ],
        out_specs=(pl.BlockSpec((ROWS_IN, D_MODEL), row), pl.BlockSpec((ROWS_IN, D_MODEL), row)),
        out_shape=(jax.ShapeDtypeStruct((n, D_MODEL), F32), jax.ShapeDtypeStruct((n, D_MODEL), BF16)),
        compiler_params=pltpu.CompilerParams(
            dimension_semantics=("arbitrary",), vmem_limit_bytes=VMEM_LIMIT),
    )(x2, y, yb, w_out, g, glu_w, glu_b, og, permt)


def _ffn_kernel(h_ref, x1_hbm, wgu_ref, wo_ref, out_ref, x1_buf, x1_sem):
    i = pl.program_id(0)
    f = pl.program_id(1)

    def x1_copy(blk):
        return pltpu.make_async_copy(x1_hbm.at[pl.ds(blk * ROWS_FFN, ROWS_FFN), :], x1_buf, x1_sem)

    @pl.when((i == 0) & (f == 0))
    def _():
        x1_copy(0).start()

    @pl.when(f == 0)
    def _():
        x1_copy(i).wait()
        out_ref[...] = x1_buf[...]

    @pl.when((f == 1) & (i + 1 < pl.num_programs(0)))
    def _():
        x1_copy(i + 1).start()

    gu = _dot(h_ref[...], wgu_ref[...])
    act = jax.nn.silu(gu[:, 0:FFN_BLK]) * gu[:, FFN_BLK:]
    out_ref[...] += _dot(act.astype(BF16), wo_ref[...])


def _ffn(h, x1, w_in, w_out):
    n = h.shape[0]
    nf = D_FFN // FFN_BLK
    return pl.pallas_call(
        _ffn_kernel,
        name="ffn",
        grid=(n // ROWS_FFN, nf),
        in_specs=[
            pl.BlockSpec((ROWS_FFN, D_MODEL), lambda i, f: (i, 0)),
            pl.BlockSpec(memory_space=pl.ANY),
            pl.BlockSpec((None, D_MODEL, 2 * FFN_BLK), lambda i, f: (f, 0, 0)),
            pl.BlockSpec((FFN_BLK, D_MODEL), lambda i, f: (f, 0)),
        ],
        out_specs=pl.BlockSpec((ROWS_FFN, D_MODEL), lambda i, f: (i, 0)),
        out_shape=jax.ShapeDtypeStruct((n, D_MODEL), F32),
        scratch_shapes=[pltpu.VMEM((ROWS_FFN, D_MODEL), F32), pltpu.SemaphoreType.DMA(())],
        compiler_params=pltpu.CompilerParams(
            dimension_semantics=("arbitrary", "arbitrary"), vmem_limit_bytes=VMEM_LIMIT_FFN),
    )(h, x1, w_in, w_out)


def _ple_kernel(x_ref, p_ref, wg_ref, bg_ref, wp_ref, g_ref, gf_ref, out_ref, *, final):
    x = x_ref[...]
    h = _rms(x, g_ref[...]).astype(BF16)
    pe = _dot(p_ref[...].astype(BF16), wp_ref[...])
    gate = jax.nn.sigmoid(_dot(h, wg_ref[...]) + bg_ref[...])
    x3 = x + gate * pe
    out_ref[...] = _rms(x3, gf_ref[...]) if final else x3


def _ple(x2, p2, w_gate, b_gate, w_proj, g, gf, final):
    n = x2.shape[0]
    row = lambda i: (i, 0)
    return pl.pallas_call(
        functools.partial(_ple_kernel, final=final),
        name="ple",
        grid=(n // ROWS_PLE,),
        in_specs=[
            pl.BlockSpec((ROWS_PLE, D_MODEL), row),
            pl.BlockSpec((ROWS_PLE, PLE_DIM), row),
            _resident(w_gate.shape),
            _resident((1, D_MODEL)),
            _resident(w_proj.shape),
            _resident((1, D_MODEL)),
            _resident((1, D_MODEL)),
        ],
        out_specs=pl.BlockSpec((ROWS_PLE, D_MODEL), row),
        out_shape=jax.ShapeDtypeStruct((n, D_MODEL), F32),
        compiler_params=pltpu.CompilerParams(
            dimension_semantics=("arbitrary",), vmem_limit_bytes=VMEM_LIMIT_FFN),
    )(x2, p2, w_gate, b_gate, w_proj, g, gf)


def _segment_permutation():
    r = np.arange(T_CHUNK)
    src = (r % SUBLANES) * K_STEPS + r // SUBLANES
    perm = np.zeros((T_CHUNK, T_CHUNK), np.float32)
    perm[r, src] = 1.0
    return perm


def _scan_exponents():
    n = [1.0]
    n += [float(K_STEPS * m) for m in (1, 2, 4)]
    n += [float(K_STEPS * (i + 1)) for i in range(SUBLANES)]
    return np.asarray(n, np.float32).reshape(-1, 1)


def kernel(x, p, norm_mix_g, w_in, ssm_lambda_re, ssm_lambda_im, ssm_log_step, ssm_b_re, ssm_b_im, ssm_c_re, ssm_c_im, ssm_d, ssm_glu_w, ssm_glu_b, sgu_ln_g, sgu_ln_b, sgu_w, sgu_b, out_norm_ssm_g, out_norm_sgu_g, w_out, norm_ffn_g, w_ffn_in, w_ffn_out, norm_ple_g, w_ple_gate, b_ple_gate, w_ple_proj, final_norm_g):
    bsz, seq, d_model = x.shape
    depth = w_in.shape[0]
    assert d_model == D_MODEL and x.dtype == F32
    assert seq % S5_ROWS == 0 and (bsz * seq) % ROWS_FFN == 0 and (bsz * seq) % ROWS_PLE == 0
    perm_np = _segment_permutation()
    perm = jnp.asarray(perm_np, BF16)
    permt = jnp.asarray(perm_np.T, BF16)
    exponents = jnp.asarray(_scan_exponents())
    x2 = x.reshape(bsz * seq, D_MODEL)
    vec = lambda a: a.reshape(1, -1)
    for i in range(depth):
        tab, bd, cd, w_in_bf = _s5_params(
            ssm_lambda_re[i], ssm_lambda_im[i], ssm_log_step[i], ssm_b_re[i], ssm_b_im[i],
            ssm_c_re[i], ssm_c_im[i], exponents, w_in[i])

        zs, yb, w_out_bf, w_gate_bf, glu_bf, w_ffn_out_bf, w_proj_bf = _in_proj(
            x2, vec(norm_mix_g[i]), w_in_bf, perm, vec(sgu_ln_g[i]), vec(sgu_ln_b[i]),
            sgu_w[i], sgu_b[i].T, vec(out_norm_sgu_g[i]),
            to_cast=(w_out[i], w_ple_gate[i], ssm_glu_w[i], w_ffn_out[i], w_ple_proj[i]))
        y, w_ffn_in_bf = _s5(zs, bsz, seq, bd, cd, tab, vec(ssm_d[i]), to_cast=w_ffn_in[i])
        x1, h = _out_proj(x2, y, yb, w_out_bf, vec(norm_ffn_g[i]), glu_bf, vec(ssm_glu_b[i]),
                          vec(out_norm_ssm_g[i]), permt)
        x2 = _ffn(h, x1, w_ffn_in_bf, w_ffn_out_bf)
        x2 = _ple(x2, p[i].reshape(bsz * seq, PLE_DIM), w_gate_bf, vec(b_ple_gate[i]),
                  w_proj_bf, vec(norm_ple_g[i]), vec(final_norm_g),
                  final=(i == depth - 1))
    return x2.reshape(bsz, seq, D_MODEL)
```

```python
import functools

import numpy as np
import jax
import jax.numpy as jnp
from jax import lax
from jax.experimental import pallas as pl
from jax.experimental.pallas import tpu as pltpu

D_MODEL = 2048
D_SSM = 1024
D_SGU = 1024
SSM_GROUP = 16
SSM_GROUPS = 64
SSM_STATE = 64
SGU_CHUNK = 128
SGU_HEADS = 8
SGU_HEAD_DIM = D_SGU // SGU_HEADS
D_FFN = 5632
PLE_DIM = 256
EPS = 1e-6
LAMBDA_RE_MAX = -1e-4

SUBLANES = 8
N_STATE = SSM_GROUPS * SSM_STATE
N_BLK = 4
BLK_CH = D_SSM // N_BLK
BLK_ST = N_STATE // N_BLK
T_CHUNK = 256
K_STEPS = T_CHUNK // SUBLANES
S5_ROWS = 512
SCAN_W = 512
N_GRP = BLK_ST // SCAN_W
N_PIECE = 2 * N_GRP

ROWS_IN = 512
ROWS_FFN = 1024
ROWS_PLE = 1024
FFN_BLK = 512
VMEM_LIMIT = 56 * 1024 * 1024
VMEM_LIMIT_FFN = 60 * 1024 * 1024

F32 = jnp.float32
BF16 = jnp.bfloat16


def _rms(x, g):
    r = lax.rsqrt(jnp.mean(x * x, axis=-1, keepdims=True) + EPS)
    return (x * r) * g


def _dot(a, b):
    return jnp.dot(a, b, preferred_element_type=F32)


def _resident(shape):
    nd = len(shape)
    return pl.BlockSpec(shape, lambda *_: (0,) * nd, pipeline_mode=pl.Buffered(1))


def _cast_block_spec(w, steps, index):
    return pl.BlockSpec((w.shape[0] // steps, w.shape[1]), index)


def _cast_out_shape(w):
    return jax.ShapeDtypeStruct(w.shape, BF16)


def _s5_params_kernel(sp_ref, n_ref, cre_ref, cim_ref, w_ref, tab_ref, bd_ref, cd_ref, w_out):
    w_out[...] = w_ref[...].astype(BF16)
    for g in range(N_GRP):
        _s5_params_slice(g, sp_ref, n_ref, cre_ref, cim_ref, tab_ref, bd_ref, cd_ref)


def _s5_params_slice(g, sp_ref, n_ref, cre_ref, cim_ref, tab_ref, bd_ref, cd_ref):
    st = slice(g * SCAN_W, (g + 1) * SCAN_W)
    h2 = 2 * SSM_GROUP
    lr = jnp.minimum(sp_ref[h2:h2 + 1, st], LAMBDA_RE_MAX)
    li = sp_ref[h2 + 1:h2 + 2, st]
    dt = jnp.exp(sp_ref[h2 + 2:h2 + 3, st])
    n = n_ref[...]
    mag = jnp.exp(n * (lr * dt))
    ang = n * (li * dt)
    tab_ref[0, g] = jnp.concatenate([mag * jnp.cos(ang), mag * jnp.sin(ang)], axis=1)
    mag1 = jnp.exp(lr * dt)
    ang1 = li * dt
    nr = mag1 * jnp.cos(ang1) - 1.0
    ni = mag1 * jnp.sin(ang1)
    den = lr * lr + li * li
    q_re = (nr * lr + ni * li) / den
    q_im = (ni * lr - nr * li) / den
    bre = sp_ref[0:SSM_GROUP, st]
    bim = sp_ref[SSM_GROUP:h2, st]
    bbar = (q_re * bre - q_im * bim, q_re * bim + q_im * bre)

    iota = lambda shape, dim: lax.broadcasted_iota(jnp.int32, shape, dim)
    grp_of_ch = lambda i: lax.shift_right_logical(i, 4)
    grp_of_st = lambda i: lax.shift_right_logical(i, 6)
    grp_per_slice = SCAN_W // SSM_STATE
    first_grp = g * grp_per_slice
    same_b = grp_of_ch(iota((BLK_CH, SCAN_W), 0)) == grp_of_st(iota((BLK_CH, SCAN_W), 1)) + first_grp
    same_c = grp_of_st(iota((SCAN_W, BLK_CH), 0)) + first_grp == grp_of_ch(iota((SCAN_W, BLK_CH), 1))
    spread = (iota((SSM_GROUP, BLK_CH), 0)
              == lax.bitwise_and(iota((SSM_GROUP, BLK_CH), 1), SSM_GROUP - 1)).astype(BF16)
    for part in range(2):
        tiled = jnp.concatenate([bbar[part]] * SSM_GROUP, axis=0)
        bd_ref[0, 2 * g + part] = jnp.where(same_b, tiled, 0.0).astype(BF16)
        c_ref = (cre_ref, cim_ref)[part]
        wide = jnp.concatenate(
            [lax.dot_general(c_ref[q * SSM_GROUP:(q + 1) * SSM_GROUP, :].astype(BF16), spread,
                             (((0,), (0,)), ((), ())), preferred_element_type=F32)
             for q in range(first_grp, first_grp + grp_per_slice)], axis=0)
        wide = wide if part == 0 else -wide
        cd_ref[0, 2 * g + part] = jnp.where(same_c, wide, 0.0).astype(BF16)


def _s5_params(lam_re, lam_im, log_step, b_re, b_im, c_re, c_im, exponents, w_in):
    n_rows = exponents.shape[0]
    steps = N_BLK
    small = jnp.concatenate([
        b_re.transpose(2, 0, 1).reshape(SSM_GROUP, N_STATE),
        b_im.transpose(2, 0, 1).reshape(SSM_GROUP, N_STATE),
        lam_re.reshape(1, N_STATE),
        lam_im.reshape(1, N_STATE),
        jnp.repeat(log_step, SSM_STATE).reshape(1, N_STATE)], axis=0)
    cre = c_re.reshape(SSM_GROUPS * SSM_GROUP, SSM_STATE)
    cim = c_im.reshape(SSM_GROUPS * SSM_GROUP, SSM_STATE)
    lanes = lambda rows: pl.BlockSpec((rows, BLK_ST), lambda s: (0, s))
    c_rows = pl.BlockSpec((BLK_ST // SSM_STATE * SSM_GROUP, SSM_STATE), lambda s: (s, 0))
    slab = lambda shape: pl.BlockSpec((1,) + shape, lambda s: (s, 0, 0, 0))
    w_block = pl.BlockSpec((w_in.shape[0] // steps, w_in.shape[1]), lambda s: (s, 0))
    return pl.pallas_call(
        _s5_params_kernel,
        name="s5_params",
        grid=(steps,),
        in_specs=[lanes(small.shape[0]), _resident(exponents.shape), c_rows, c_rows, w_block],
        out_specs=(slab((N_GRP, n_rows, 2 * SCAN_W)), slab((N_PIECE, BLK_CH, SCAN_W)),
                   slab((N_PIECE, SCAN_W, BLK_CH)),
                   w_block),
        out_shape=(jax.ShapeDtypeStruct((N_BLK, N_GRP, n_rows, 2 * SCAN_W), F32),
                   jax.ShapeDtypeStruct((N_BLK, N_PIECE, BLK_CH, SCAN_W), BF16),
                   jax.ShapeDtypeStruct((N_BLK, N_PIECE, SCAN_W, BLK_CH), BF16),
                   jax.ShapeDtypeStruct(w_in.shape, BF16)),
        compiler_params=pltpu.CompilerParams(
            dimension_semantics=("arbitrary",), vmem_limit_bytes=VMEM_LIMIT),
    )(small, exponents, cre, cim, w_in)


def _in_proj_kernel(x_ref, g_ref, w_ref, perm_ref, lng_ref, lnb_ref, sw_ref, sbt_ref, og_ref,
                    c0_ref, c1_ref, c2_ref, c3_ref, c4_ref,
                    zs_ref, yb_ref, c0_out, c1_out, c2_out, c3_out, c4_out, s_scr):
    c0_out[...] = c0_ref[...].astype(BF16)
    c1_out[...] = c1_ref[...].astype(BF16)
    c2_out[...] = c2_ref[...].astype(BF16)
    c3_out[...] = c3_ref[...].astype(BF16)
    c4_out[...] = c4_ref[...].astype(BF16)
    rows = x_ref.shape[0]
    h = _rms(x_ref[...], g_ref[...]).astype(BF16)
    gv = jax.nn.gelu(_dot(h, w_ref[:, D_SSM + D_SGU:]))
    mu = jnp.mean(gv, axis=-1, keepdims=True)
    xc = gv - mu
    r = lax.rsqrt(jnp.mean(xc * xc, axis=-1, keepdims=True) + EPS)
    v = ((xc * r) * lng_ref[...] + lnb_ref[...]).astype(BF16)
    u = jax.nn.gelu(_dot(h, w_ref[:, D_SSM:D_SSM + D_SGU]))
    ti = lax.broadcasted_iota(jnp.int32, (SGU_CHUNK, SGU_CHUNK), 0)
    si = lax.broadcasted_iota(jnp.int32, (SGU_CHUNK, SGU_CHUNK), 1)
    causal = si <= ti
    sbt = sbt_ref[...]
    for hd in range(SGU_HEADS):
        wm = jnp.where(causal, sw_ref[hd], 0.0).astype(BF16)
        cols = slice(hd * SGU_HEAD_DIM, (hd + 1) * SGU_HEAD_DIM)
        bias = sbt[:, hd:hd + 1]
        for c in range(rows // SGU_CHUNK):
            rs = slice(c * SGU_CHUNK, (c + 1) * SGU_CHUNK)
            s_scr[rs, cols] = _dot(wm, v[rs, cols]) + bias
    yb = u * s_scr[...]
    yb_ref[...] = _rms(yb, og_ref[...]).astype(BF16)
    perm = perm_ref[...]
    hp = jnp.concatenate(
        [_dot(perm, h[c * T_CHUNK:(c + 1) * T_CHUNK]).astype(BF16) for c in range(rows // T_CHUNK)],
        axis=0)
    zs_ref[...] = _dot(hp, w_ref[:, 0:D_SSM])


def _in_proj(x2, g, w_in, perm, ln_g, ln_b, sgu_w, sgu_bt, og, to_cast):
    n = x2.shape[0]
    steps = n // ROWS_IN
    row = lambda i: (i, 0)
    return pl.pallas_call(
        _in_proj_kernel,
        name="in_proj",
        grid=(n // ROWS_IN,),
        in_specs=[
            pl.BlockSpec((ROWS_IN, D_MODEL), row),
            _resident((1, D_MODEL)),
            _resident(w_in.shape),
            _resident(perm.shape),
            _resident((1, D_SGU)),
            _resident((1, D_SGU)),
            _resident(sgu_w.shape),
            _resident(sgu_bt.shape),
            _resident((1, D_SGU)),
        ] + [_cast_block_spec(w, steps, row) for w in to_cast],
        out_specs=(pl.BlockSpec((ROWS_IN, D_SSM), row), pl.BlockSpec((ROWS_IN, D_SGU), row))
        + tuple(_cast_block_spec(w, steps, row) for w in to_cast),
        out_shape=(jax.ShapeDtypeStruct((n, D_SSM), F32), jax.ShapeDtypeStruct((n, D_SGU), BF16))
        + tuple(_cast_out_shape(w) for w in to_cast),
        scratch_shapes=[pltpu.VMEM((ROWS_IN, D_SGU), F32)],
        compiler_params=pltpu.CompilerParams(
            dimension_semantics=("arbitrary",), vmem_limit_bytes=VMEM_LIMIT),
    )(x2, g, w_in, perm, ln_g, ln_b, sgu_w, sgu_bt, og, *to_cast)


def _s5_kernel(zp_ref, zn_ref, bd_ref, cd_ref, tab_ref, d_ref, c0_ref,
               out_ref, c0_out, bu_a, bu_b, sb_a, sb_b, carry_ref):
    for f in range(c0_out.shape[0]):
        gate_cols = slice(f * FFN_BLK, (f + 1) * FFN_BLK)
        up_cols = slice(D_FFN + f * FFN_BLK, D_FFN + (f + 1) * FFN_BLK)
        c0_out[f, :, 0:FFN_BLK] = c0_ref[:, gate_cols].astype(BF16)
        c0_out[f, :, FFN_BLK:] = c0_ref[:, up_cols].astype(BF16)

    @pl.when(pl.program_id(1) == 0)
    def _():
        carry_ref[...] = jnp.zeros_like(carry_ref)

    zp = zp_ref[...]
    zpb = zp.astype(BF16)
    znb = zn_ref[...].astype(BF16)
    bu = (bu_a, bu_b)
    sb = (sb_a, sb_b)
    seg = lax.broadcasted_iota(jnp.int32, (SUBLANES, SCAN_W), 0)
    zero = jnp.zeros((SUBLANES, SCAN_W), F32)
    blk = lambda j: slice(j * BLK_CH, (j + 1) * BLK_CH)

    @pl.when((pl.program_id(0) == 0) & (pl.program_id(1) == 0))
    def _():
        for p in range(N_PIECE):
            bu[0][p] = _dot(zpb[:, blk(0)], bd_ref[0, p])

    for j in range(N_BLK):
        out_ref[:, blk(j)] = d_ref[:, blk(j)] * zp[:, blk(j)]

    for j in range(N_BLK):
        cur, oth = j % 2, (j + 1) % 2

        def body(g, _, j=j, cur=cur, oth=oth):
            for part in range(2):
                p = 2 * g + part
                if j + 1 < N_BLK:
                    bu[oth][p] = _dot(zpb[:, blk(j + 1)], bd_ref[j + 1, p])
                else:
                    bu[oth][p] = _dot(znb, bd_ref[0, p])
                if j >= 1:
                    out_ref[:, blk(j - 1)] += _dot(sb[oth][p], cd_ref[j - 1, p])

            re_p, im_p = 2 * g, 2 * g + 1
            tre, tim = slice(0, SCAN_W), slice(SCAN_W, 2 * SCAN_W)
            ar = jnp.broadcast_to(tab_ref[j, g, 0:1, tre], (SUBLANES, SCAN_W))
            ai = jnp.broadcast_to(tab_ref[j, g, 0:1, tim], (SUBLANES, SCAN_W))

            def step(row0, sr, si):
                rows = pl.ds(row0, SUBLANES)
                return (ar * sr - ai * si + bu[cur][re_p, rows, :],
                        ar * si + ai * sr + bu[cur][im_p, rows, :])

            def start_states(xr, xi):
                for lvl, m in enumerate((1, 2, 4)):
                    mr = tab_ref[j, g, 1 + lvl:2 + lvl, tre]
                    mi = tab_ref[j, g, 1 + lvl:2 + lvl, tim]
                    rr = pltpu.roll(xr, m, 0)
                    ri = pltpu.roll(xi, m, 0)
                    keep = seg >= m
                    xr, xi = (xr + jnp.where(keep, mr * rr - mi * ri, 0.0),
                              xi + jnp.where(keep, mr * ri + mi * rr, 0.0))
                cr = carry_ref[j, g, :, tre]
                ci = carry_ref[j, g, :, tim]
                pr = tab_ref[j, g, 4:4 + SUBLANES, tre]
                pi = tab_ref[j, g, 4:4 + SUBLANES, tim]
                xr, xi = xr + pr * cr - pi * ci, xi + pr * ci + pi * cr
                carry_ref[j, g, :, tre] = jnp.broadcast_to(xr[SUBLANES - 1:SUBLANES], (SUBLANES, SCAN_W))
                carry_ref[j, g, :, tim] = jnp.broadcast_to(xi[SUBLANES - 1:SUBLANES], (SUBLANES, SCAN_W))
                return (jnp.where(seg >= 1, pltpu.roll(xr, 1, 0), cr),
                        jnp.where(seg >= 1, pltpu.roll(xi, 1, 0), ci))

            def emit_pair(base, m, sr, si):
                r1, i1 = step(base + 2 * m * SUBLANES, sr, si)
                sr, si = step(base + (2 * m + 1) * SUBLANES, r1, i1)
                rows = pl.ds(base + m * 2 * SUBLANES, 2 * SUBLANES)
                sb[cur][re_p, rows, :] = jnp.concatenate([r1, sr], axis=0).astype(BF16)
                sb[cur][im_p, rows, :] = jnp.concatenate([i1, si], axis=0).astype(BF16)
                return sr, si

            n_chunk = S5_ROWS // T_CHUNK
            state = None
            for c in range(n_chunk + 1):
                xr, xi = zero, zero
                for m in range(K_STEPS // 2):
                    if c < n_chunk:
                        xr, xi = step(c * T_CHUNK + 2 * m * SUBLANES, xr, xi)
                        xr, xi = step(c * T_CHUNK + (2 * m + 1) * SUBLANES, xr, xi)
                    if c >= 1:
                        state = emit_pair((c - 1) * T_CHUNK, m, *state)
                if c < n_chunk:
                    state = start_states(xr, xi)
            return 0

        lax.fori_loop(0, N_GRP, body, 0)

    last = N_BLK - 1
    for p in range(N_PIECE):
        out_ref[:, blk(last)] += _dot(sb[last % 2][p], cd_ref[last, p])


def _s5(zs, bsz, seq, bd, cd, tab, dvec, to_cast):
    steps = seq // S5_ROWS
    row = lambda b, c: (b * steps + c, 0)
    nxt = lambda b, c: (jnp.minimum(b * steps + c + 1, bsz * steps - 1), 0)
    n_fblk = D_FFN // FFN_BLK
    cast_rows = to_cast.shape[0] // (bsz * steps)
    return pl.pallas_call(
        _s5_kernel,
        name="s5",
        grid=(bsz, steps),
        in_specs=[
            pl.BlockSpec((S5_ROWS, D_SSM), row),
            pl.BlockSpec((S5_ROWS, BLK_CH), nxt),
            _resident(bd.shape), _resident(cd.shape), _resident(tab.shape),
            _resident(dvec.shape),
            _cast_block_spec(to_cast, bsz * steps, row),
        ],
        out_specs=(pl.BlockSpec((S5_ROWS, D_SSM), row),
                   pl.BlockSpec((n_fblk, cast_rows, 2 * FFN_BLK), lambda b, c: (0, b * steps + c, 0))),
        out_shape=(jax.ShapeDtypeStruct((bsz * seq, D_SSM), F32),
                   jax.ShapeDtypeStruct((n_fblk, to_cast.shape[0], 2 * FFN_BLK), BF16)),
        scratch_shapes=[
            pltpu.VMEM((N_PIECE, S5_ROWS, SCAN_W), F32),
            pltpu.VMEM((N_PIECE, S5_ROWS, SCAN_W), F32),
            pltpu.VMEM((N_PIECE, S5_ROWS, SCAN_W), BF16),
            pltpu.VMEM((N_PIECE, S5_ROWS, SCAN_W), BF16),
            pltpu.VMEM((N_BLK, N_GRP, SUBLANES, 2 * SCAN_W), F32),
        ],
        compiler_params=pltpu.CompilerParams(
            dimension_semantics=("arbitrary", "arbitrary"), vmem_limit_bytes=VMEM_LIMIT),
    )(zs, zs, bd, cd, tab, dvec, to_cast)


def _out_proj_kernel(x_ref, y_ref, yb_ref, w_ref, g_ref, gluw_ref, glub_ref, og_ref, permt_ref,
                     x1_ref, h_ref):
    half = D_MODEL // 2
    yb = yb_ref[...]
    y = jax.nn.gelu(y_ref[...])
    x1_lo = x_ref[:, 0:half] + _dot(yb, w_ref[D_SSM:, 0:half])
    gate = jax.nn.sigmoid(_dot(y.astype(BF16), gluw_ref[...]) + glub_ref[...])
    ya = _rms(y * gate, og_ref[...]).astype(BF16)
    x1_hi = x_ref[:, half:] + _dot(yb, w_ref[D_SSM:, half:])
    permt = permt_ref[...]
    ya = jnp.concatenate(
        [_dot(permt, ya[c * T_CHUNK:(c + 1) * T_CHUNK]).astype(BF16)
         for c in range(x_ref.shape[0] // T_CHUNK)], axis=0)
    x1 = jnp.concatenate([x1_lo, x1_hi], axis=1) + _dot(ya, w_ref[0:D_SSM, :])
    x1_ref[...] = x1
    h_ref[...] = _rms(x1, g_ref[...]).astype(BF16)


def _out_proj(x2, y, yb, w_out, g, glu_w, glu_b, og, permt):
    n = x2.shape[0]
    row = lambda i: (i, 0)
    return pl.pallas_call(
        _out_proj_kernel,
        name="out_proj",
        grid=(n // ROWS_IN,),
        in_specs=[
            pl.BlockSpec((ROWS_IN, D_MODEL), row),
            pl.BlockSpec((ROWS_IN, D_SSM), row),
            pl.BlockSpec((ROWS_IN, D_SGU), row),
            _resident(w_out.shape),
            _resident((1, D_MODEL)),
            _resident(glu_w.shape), _resident(glu_b.shape), _resident(og.shape),
            _resident(permt.shape),
        ],
        out_specs=(pl.BlockSpec((ROWS_IN, D_MODEL), row), pl.BlockSpec((ROWS_IN, D_MODEL), row)),
        out_shape=(jax.ShapeDtypeStruct((n, D_MODEL), F32), jax.ShapeDtypeStruct((n, D_MODEL), BF16)),
        compiler_params=pltpu.CompilerParams(
            dimension_semantics=("arbitrary",), vmem_limit_bytes=VMEM_LIMIT),
    )(x2, y, yb, w_out, g, glu_w, glu_b, og, permt)


def _ffn_kernel(h_ref, x1_ref, wgu_ref, wo_ref, out_ref):
    @pl.when(pl.program_id(1) == 0)
    def _():
        out_ref[...] = x1_ref[...]

    gu = _dot(h_ref[...], wgu_ref[...])
    act = jax.nn.silu(gu[:, 0:FFN_BLK]) * gu[:, FFN_BLK:]
    out_ref[...] += _dot(act.astype(BF16), wo_ref[...])


def _ffn(h, x1, w_in, w_out):
    n = h.shape[0]
    nf = D_FFN // FFN_BLK
    return pl.pallas_call(
        _ffn_kernel,
        name="ffn",
        grid=(n // ROWS_FFN, nf),
        in_specs=[
            pl.BlockSpec((ROWS_FFN, D_MODEL), lambda i, f: (i, 0)),
            pl.BlockSpec((ROWS_FFN, D_MODEL), lambda i, f: (i, 0)),
            pl.BlockSpec((None, D_MODEL, 2 * FFN_BLK), lambda i, f: (f, 0, 0)),
            pl.BlockSpec((FFN_BLK, D_MODEL), lambda i, f: (f, 0)),
        ],
        out_specs=pl.BlockSpec((ROWS_FFN, D_MODEL), lambda i, f: (i, 0)),
        out_shape=jax.ShapeDtypeStruct((n, D_MODEL), F32),
        compiler_params=pltpu.CompilerParams(
            dimension_semantics=("arbitrary", "arbitrary"), vmem_limit_bytes=VMEM_LIMIT_FFN),
    )(h, x1, w_in, w_out)


def _ple_kernel(x_ref, p_ref, wg_ref, bg_ref, wp_ref, g_ref, gf_ref, out_ref, *, final):
    x = x_ref[...]
    h = _rms(x, g_ref[...]).astype(BF16)
    pe = _dot(p_ref[...].astype(BF16), wp_ref[...])
    gate = jax.nn.sigmoid(_dot(h, wg_ref[...]) + bg_ref[...])
    x3 = x + gate * pe
    out_ref[...] = _rms(x3, gf_ref[...]) if final else x3


def _ple(x2, p2, w_gate, b_gate, w_proj, g, gf, final):
    n = x2.shape[0]
    row = lambda i: (i, 0)
    return pl.pallas_call(
        functools.partial(_ple_kernel, final=final),
        name="ple",
        grid=(n // ROWS_PLE,),
        in_specs=[
            pl.BlockSpec((ROWS_PLE, D_MODEL), row),
            pl.BlockSpec((ROWS_PLE, PLE_DIM), row),
            _resident(w_gate.shape),
            _resident((1, D_MODEL)),
            _resident(w_proj.shape),
            _resident((1, D_MODEL)),
            _resident((1, D_MODEL)),
        ],
        out_specs=pl.BlockSpec((ROWS_PLE, D_MODEL), row),
        out_shape=jax.ShapeDtypeStruct((n, D_MODEL), F32),
        compiler_params=pltpu.CompilerParams(
            dimension_semantics=("arbitrary",), vmem_limit_bytes=VMEM_LIMIT_FFN),
    )(x2, p2, w_gate, b_gate, w_proj, g, gf)


def _segment_permutation():
    r = np.arange(T_CHUNK)
    src = (r % SUBLANES) * K_STEPS + r // SUBLANES
    perm = np.zeros((T_CHUNK, T_CHUNK), np.float32)
    perm[r, src] = 1.0
    return perm


def _scan_exponents():
    n = [1.0]
    n += [float(K_STEPS * m) for m in (1, 2, 4)]
    n += [float(K_STEPS * (i + 1)) for i in range(SUBLANES)]
    return np.asarray(n, np.float32).reshape(-1, 1)


def kernel(x, p, norm_mix_g, w_in, ssm_lambda_re, ssm_lambda_im, ssm_log_step, ssm_b_re, ssm_b_im, ssm_c_re, ssm_c_im, ssm_d, ssm_glu_w, ssm_glu_b, sgu_ln_g, sgu_ln_b, sgu_w, sgu_b, out_norm_ssm_g, out_norm_sgu_g, w_out, norm_ffn_g, w_ffn_in, w_ffn_out, norm_ple_g, w_ple_gate, b_ple_gate, w_ple_proj, final_norm_g):
    bsz, seq, d_model = x.shape
    depth = w_in.shape[0]
    assert d_model == D_MODEL and x.dtype == F32
    assert seq % S5_ROWS == 0 and (bsz * seq) % ROWS_FFN == 0 and (bsz * seq) % ROWS_PLE == 0
    perm_np = _segment_permutation()
    perm = jnp.asarray(perm_np, BF16)
    permt = jnp.asarray(perm_np.T, BF16)
    exponents = jnp.asarray(_scan_exponents())
    x2 = x.reshape(bsz * seq, D_MODEL)
    vec = lambda a: a.reshape(1, -1)
    for i in range(depth):
        tab, bd, cd, w_in_bf = _s5_params(
            ssm_lambda_re[i], ssm_lambda_im[i], ssm_log_step[i], ssm_b_re[i], ssm_b_im[i],
            ssm_c_re[i], ssm_c_im[i], exponents, w_in[i])

        zs, yb, w_out_bf, w_gate_bf, glu_bf, w_ffn_out_bf, w_proj_bf = _in_proj(
            x2, vec(norm_mix_g[i]), w_in_bf, perm, vec(sgu_ln_g[i]), vec(sgu_ln_b[i]),
            sgu_w[i], sgu_b[i].T, vec(out_norm_sgu_g[i]),
            to_cast=(w_out[i], w_ple_gate[i], ssm_glu_w[i], w_ffn_out[i], w_ple_proj[i]))
        y, w_ffn_in_bf = _s5(zs, bsz, seq, bd, cd, tab, vec(ssm_d[i]), to_cast=w_ffn_in[i])
        x1, h = _out_proj(x2, y, yb, w_out_bf, vec(norm_ffn_g[i]), glu_bf, vec(ssm_glu_b[i]),
                          vec(out_norm_ssm_g[i]), permt)
        x2 = _ffn(h, x1, w_ffn_in_bf, w_ffn_out_bf)
        x2 = _ple(x2, p[i].reshape(bsz * seq, PLE_DIM), w_gate_bf, vec(b_ple_gate[i]),
                  w_proj_bf, vec(norm_ple_g[i]), vec(final_norm_g),
                  final=(i == depth - 1))
    return x2.reshape(bsz, seq, D_MODEL)
```
